```python
import jax, jax.numpy as jnp
from jax import lax
import numpy as np

D_MODEL = 2048
BATCH = 8
SEQ = 4096
DEPTH = 4

CTX_LEN = 256
GRID_W = 64
EPS = 1e-6
F_MIN = 1e-30
N_MOD = 6

HEAD_DIM = 128
ATTN_HEADS = 8
ATTN_KV_HEADS = 2
ATTN_GROUP = ATTN_HEADS // ATTN_KV_HEADS
ATTN_Q_BLOCK = 128
ROPE_THETA = 10000.0
ATTN_Q_W = ATTN_HEADS * HEAD_DIM
ATTN_KV_W = ATTN_KV_HEADS * HEAD_DIM

HG_HEADS = 4
HG_DK = 128
HG_DV = 128
HG_CHUNK = 16
HG_K_W = HG_HEADS * HG_DK
HG_V_W = HG_HEADS * HG_DV

SG_GROUPS = 4
SG_DIM = 128
SG_CHUNK = 128
SG_W = SG_GROUPS * SG_DIM

D_MIX = ATTN_Q_W + HG_V_W + SG_W
IN_SIZES = (ATTN_Q_W, ATTN_KV_W, ATTN_KV_W, HG_K_W, HG_K_W, HG_K_W, HG_V_W, HG_V_W, SG_W, SG_W)
IN_COLS = 5120

D_FF = 5632
CONV_W = 3

kernel_name = "hybrid_parallel_mixer_dit_block"


def rms_norm(x, g):
    xf = x.astype(jnp.float32)
    y = xf * lax.rsqrt(jnp.mean(xf * xf, axis=-1, keepdims=True) + EPS)
    return (y * g.astype(jnp.float32)).astype(x.dtype)


def axial_rope_tables(n_tokens):
    rows = n_tokens // GRID_W
    row = jnp.repeat(jnp.arange(rows, dtype=jnp.float32), GRID_W)
    col = jnp.tile(jnp.arange(GRID_W, dtype=jnp.float32), rows)
    n_freq = HEAD_DIM // 4
    inv = ROPE_THETA ** (-jnp.arange(n_freq, dtype=jnp.float32) / n_freq)
    ang = jnp.concatenate([row[:, None] * inv, col[:, None] * inv], axis=-1)
    return jnp.cos(ang), jnp.sin(ang)


def apply_rope(x, cos, sin):
    xf = x.astype(jnp.float32).reshape(*x.shape[:-1], HEAD_DIM // 2, 2)
    x1, x2 = xf[..., 0], xf[..., 1]
    cs, sn = cos[None, :, None, :], sin[None, :, None, :]
    out = jnp.stack([x1 * cs - x2 * sn, x1 * sn + x2 * cs], axis=-1)
    return out.reshape(x.shape).astype(x.dtype)


def gqa_softmax(q, k, v):
    s = jnp.einsum('bqkgd,bskd->bkgqs', q, k).astype(jnp.float32) * (HEAD_DIM ** -0.5)
    p = jax.nn.softmax(s, axis=-1).astype(v.dtype)
    return jnp.einsum('bkgqs,bskd->bqkgd', p, v)


def latent_attention(q, k_all, v_all):
    B, T, H, Dh = q.shape
    nb = T // ATTN_Q_BLOCK
    qb = q.reshape(B, nb, ATTN_Q_BLOCK, ATTN_KV_HEADS, ATTN_GROUP, Dh).transpose(1, 0, 2, 3, 4, 5)
    o = lax.map(lambda qblk: gqa_softmax(qblk, k_all, v_all), qb)
    return o.transpose(1, 0, 2, 3, 4, 5).reshape(B, T, H * Dh)


def context_attention(q, k, v):
    B, L, H, Dh = q.shape
    o = gqa_softmax(q.reshape(B, L, ATTN_KV_HEADS, ATTN_GROUP, Dh), k, v)
    return o.reshape(B, L, H * Dh)


def hgrn_lower_bounds(lb_param):
    p = jax.nn.softmax(lb_param.astype(jnp.float32), axis=1)
    return jnp.cumsum(p, axis=1) - p[:, :1]


def hgrn2_gates(f_raw, lb):
    z = f_raw.astype(jnp.float32)
    lb = lb.reshape(HG_HEADS, HG_DK)
    f = lb + (1.0 - lb) * jax.nn.sigmoid(z)
    log_f = jnp.log(jnp.maximum(f, F_MIN))
    k = (1.0 - lb) * jax.nn.sigmoid(-z)
    return log_f, k


def gla_chunkwise(q, k, v, log_f, s0):
    B, T, H, K = q.shape
    V = v.shape[-1]
    C = HG_CHUNK
    N = T // C
    q, k, log_f = [a.reshape(B, N, C, H, K) for a in (q, k, log_f)]
    v = v.reshape(B, N, C, H, V)
    b = jnp.cumsum(log_f, axis=2)
    tri = jnp.tril(jnp.ones((C, C), dtype=bool))[None, None, :, :, None, None]
    diff = b[:, :, :, None] - b[:, :, None, :]
    decay = jnp.where(tri, jnp.exp(jnp.where(tri, diff, 0.0)), 0.0)
    scores = jnp.einsum('bnthk,bnshk,bntshk->bnhts', q, k, decay)
    o_intra = jnp.einsum('bnhts,bnshv->bnthv', scores, v)
    b_last = b[:, :, -1]
    q_dec = q * jnp.exp(b)
    k_dec = k * jnp.exp(b_last[:, :, None] - b)

    def step(S, xs):
        qd, kd, vv, dl = xs
        o = jnp.einsum('bthk,bhkv->bthv', qd, S)
        S = S * dl[..., None] + jnp.einsum('bthk,bthv->bhkv', kd, vv)
        return S, o

    xs = tuple(jnp.moveaxis(a, 1, 0) for a in (q_dec, k_dec, v, jnp.exp(b_last)))
    s_fin, o_inter = lax.scan(step, s0, xs)
    o = o_intra + jnp.moveaxis(o_inter, 0, 1)
    return o.reshape(B, T, H, V), s_fin


def hgrn2_direction(q_l, i_l, f_l, q_c, i_c, f_c, lb):
    B = q_l.shape[0]
    s0 = jnp.zeros((B, HG_HEADS, HG_DK, HG_DV), jnp.float32)
    logf_c, k_c = hgrn2_gates(f_c, lb)
    o_c, s_ctx = gla_chunkwise(q_c, k_c, i_c, logf_c, s0)
    logf_l, k_l = hgrn2_gates(f_l, lb)
    o_l, _ = gla_chunkwise(q_l, k_l, i_l, logf_l, s_ctx)
    return o_l, o_c


def spatial_gating(u, v, norm_g, w_s, b_s):
    B, T, _ = u.shape
    N = T // SG_CHUNK
    u = jax.nn.gelu(u)
    v = rms_norm(jax.nn.gelu(v).reshape(B, T, SG_GROUPS, SG_DIM), norm_g.reshape(SG_GROUPS, SG_DIM))
    v = v.reshape(B, N, SG_CHUNK, SG_GROUPS, SG_DIM)
    mixed = jnp.einsum('gts,bnsgd->bntgd', w_s, v) + b_s.T[:, :, None]
    return u * mixed.reshape(B, T, SG_W)


def conv_ffn(h, w_up, conv_w, conv_b, w_down):
    T = h.shape[1]
    up = h @ w_up
    pad = jnp.pad(up, ((0, 0), (CONV_W // 2, CONV_W // 2), (0, 0)))
    y = conv_b + sum(pad[:, j:j + T] * conv_w[j] for j in range(CONV_W))
    gate, val = jnp.split(y, 2, axis=-1)
    return (jax.nn.silu(gate) * val) @ w_down


def token_mixers(h, hc, w_in, q_g, k_g, lb_f, lb_b, hg_g, sg_g, sg_w, sg_b, cos, sin, need_ctx):
    B, T, _ = h.shape
    L = hc.shape[1]
    splits = np.cumsum(IN_SIZES)[:-1].tolist()
    (aq, ak, av, hq, hff, hfb, hi, hgt, su, sv) = jnp.split(h @ w_in, splits, axis=-1)
    (aqc, akc, avc, hqc, hffc, hfbc, hic, hgtc, suc, svc) = jnp.split(hc @ w_in, splits, axis=-1)

    q = apply_rope(rms_norm(aq.reshape(B, T, ATTN_HEADS, HEAD_DIM), q_g), cos, sin)
    k = apply_rope(rms_norm(ak.reshape(B, T, ATTN_KV_HEADS, HEAD_DIM), k_g), cos, sin)
    v = av.reshape(B, T, ATTN_KV_HEADS, HEAD_DIM)
    kc = rms_norm(akc.reshape(B, L, ATTN_KV_HEADS, HEAD_DIM), k_g)
    vc = avc.reshape(B, L, ATTN_KV_HEADS, HEAD_DIM)
    attn = latent_attention(q, jnp.concatenate([k, kc], axis=1), jnp.concatenate([v, vc], axis=1))

    def heads(a, d):
        return a.astype(jnp.float32).reshape(a.shape[0], a.shape[1], -1, d)
    flip = lambda a: jnp.flip(a, axis=1)
    q_l, i_l = jax.nn.silu(heads(hq, HG_DK)), heads(hi, HG_DV)
    q_c, i_c = jax.nn.silu(heads(hqc, HG_DK)), heads(hic, HG_DV)
    ff_l, fb_l, ff_c, fb_c = heads(hff, HG_DK), heads(hfb, HG_DK), heads(hffc, HG_DK), heads(hfbc, HG_DK)
    o_f, oc_f = hgrn2_direction(q_l, i_l, ff_l, q_c, i_c, ff_c, lb_f)
    o_b, oc_b = hgrn2_direction(flip(q_l), flip(i_l), flip(fb_l), flip(q_c), flip(i_c), flip(fb_c), lb_b)
    hg_out = (rms_norm(o_f + flip(o_b), hg_g).reshape(B, T, HG_V_W)
              * jax.nn.silu(hgt.astype(jnp.float32))).astype(h.dtype)

    sg = spatial_gating(su, sv, sg_g, sg_w, sg_b)

    mix = jnp.concatenate([attn, hg_out, sg], axis=-1)
    if not need_ctx:
        return mix, None
    qc = rms_norm(aqc.reshape(B, L, ATTN_HEADS, HEAD_DIM), q_g)
    attn_c = context_attention(qc, kc, vc)
    hg_c = (rms_norm(oc_f + flip(oc_b), hg_g).reshape(B, L, HG_V_W)
            * jax.nn.silu(hgtc.astype(jnp.float32))).astype(hc.dtype)
    sg_c = spatial_gating(suc, svc, sg_g, sg_w, sg_b)
    mix_c = jnp.concatenate([attn_c, hg_c, sg_c], axis=-1)
    return mix, mix_c


def _fwd_setup_inputs(seed: int = 0) -> dict:
    key = jax.random.key(seed)
    ks = jax.random.split(key, 24)
    nrm = lambda k, shape, s: jax.random.normal(k, shape, jnp.float32) * s
    L = DEPTH
    return {
        "x": nrm(ks[0], (BATCH, SEQ, D_MODEL), 1.0),
        "c": nrm(ks[1], (BATCH, D_MODEL), 1.0),
        "ctx": nrm(ks[2], (BATCH, CTX_LEN, D_MODEL), 1.0),
        "c_ctx": nrm(ks[3], (D_MODEL,), 1.0),
        "w_ada": nrm(ks[4], (L, D_MODEL, N_MOD * D_MODEL), 0.5 * D_MODEL ** -0.5),
        "b_ada": nrm(ks[5], (L, N_MOD * D_MODEL), 0.02),
        "norm1_g": 1.0 + nrm(ks[6], (L, D_MODEL), 0.02),
        "w_in": nrm(ks[7], (L, D_MODEL, IN_COLS), D_MODEL ** -0.5),
        "q_norm_g": 1.0 + nrm(ks[8], (L, HEAD_DIM), 0.02),
        "k_norm_g": 1.0 + nrm(ks[9], (L, HEAD_DIM), 0.02),
        "hg_lower_bounds": nrm(ks[10], (2, L, HG_K_W), 0.1),
        "hg_norm_g": 1.0 + nrm(ks[11], (L, HG_DV), 0.02),
        "sg_norm_g": 1.0 + nrm(ks[12], (L, SG_W), 0.02),
        "sg_w": nrm(ks[13], (L, SG_GROUPS, SG_CHUNK, SG_CHUNK), SG_CHUNK ** -0.5),
        "sg_b": 1.0 + nrm(ks[14], (L, SG_GROUPS, SG_CHUNK), 0.02),
        "w_out": nrm(ks[15], (L, D_MIX, D_MODEL), D_MIX ** -0.5),
        "norm2_g": 1.0 + nrm(ks[16], (L, D_MODEL), 0.02),
        "w_up": nrm(ks[17], (L, D_MODEL, 2 * D_FF), D_MODEL ** -0.5),
        "conv_w": nrm(ks[18], (L, CONV_W, 2 * D_FF), CONV_W ** -0.5),
        "conv_b": nrm(ks[19], (L, 2 * D_FF), 0.02),
        "w_down": nrm(ks[20], (L, D_FF, D_MODEL), D_FF ** -0.5),
        "final_norm_g": 1.0 + nrm(ks[21], (D_MODEL,), 0.02),
    }


def _fwd_reference(x, c, ctx, c_ctx, w_ada, b_ada, norm1_g, w_in, q_norm_g, k_norm_g,
              hg_lower_bounds, hg_norm_g, sg_norm_g, sg_w, sg_b, w_out,
              norm2_g, w_up, conv_w, conv_b, w_down, final_norm_g):
    B, T, _ = x.shape
    cos, sin = axial_rope_tables(T)
    lbs = hgrn_lower_bounds(hg_lower_bounds)
    silu_c = jax.nn.silu(c)
    silu_cc = jax.nn.silu(c_ctx)
    cx = ctx
    for l in range(DEPTH):
        need_ctx = l < DEPTH - 1
        mod = (silu_c @ w_ada[l] + b_ada[l]).reshape(B, N_MOD, 1, D_MODEL)
        mod_c = (silu_cc @ w_ada[l] + b_ada[l]).reshape(N_MOD, D_MODEL)

        h = rms_norm(x, norm1_g[l]) * (1.0 + mod[:, 1]) + mod[:, 0]
        hc = rms_norm(cx, norm1_g[l]) * (1.0 + mod_c[1]) + mod_c[0]
        mix, mix_c = token_mixers(h, hc, w_in[l], q_norm_g[l], k_norm_g[l], lbs[0, l], lbs[1, l],
                                  hg_norm_g[l], sg_norm_g[l], sg_w[l], sg_b[l], cos, sin, need_ctx)
        x = x + mod[:, 2] * (mix @ w_out[l])

        h2 = rms_norm(x, norm2_g[l]) * (1.0 + mod[:, 4]) + mod[:, 3]
        x = x + mod[:, 5] * conv_ffn(h2, w_up[l], conv_w[l], conv_b[l], w_down[l])

        if need_ctx:
            cx = cx + mod_c[2] * (mix_c @ w_out[l])
            hc2 = rms_norm(cx, norm2_g[l]) * (1.0 + mod_c[4]) + mod_c[3]
            cx = cx + mod_c[5] * conv_ffn(hc2, w_up[l], conv_w[l], conv_b[l], w_down[l])
    return rms_norm(x, final_norm_g)


import jax as _jax
import jax.numpy as _jnp

TWIN_FORMAT = 'train_step'
FWD_PARAMS = ['x', 'c', 'ctx', 'c_ctx', 'w_ada', 'b_ada', 'norm1_g', 'w_in', 'q_norm_g', 'k_norm_g', 'hg_lower_bounds', 'hg_norm_g', 'sg_norm_g', 'sg_w', 'sg_b', 'w_out', 'norm2_g', 'w_up', 'conv_w', 'conv_b', 'w_down', 'final_norm_g']
TWIN_WEIGHTS = ['c_ctx', 'w_ada', 'b_ada', 'norm1_g', 'w_in', 'q_norm_g', 'k_norm_g', 'hg_lower_bounds', 'hg_norm_g', 'sg_norm_g', 'sg_w', 'sg_b', 'w_out', 'norm2_g', 'w_up', 'conv_w', 'conv_b', 'w_down', 'final_norm_g']
TWIN_DIFF_INPUT = 'x'
TWIN_INPUTS = ['x', 'c', 'ctx', 'c_ctx', 'w_ada', 'b_ada', 'norm1_g', 'w_in', 'q_norm_g', 'k_norm_g', 'hg_lower_bounds', 'hg_norm_g', 'sg_norm_g', 'sg_w', 'sg_b', 'w_out', 'norm2_g', 'w_up', 'conv_w', 'conv_b', 'w_down', 'final_norm_g', 'loss_target', 'm_c_ctx', 'm_w_ada', 'm_b_ada', 'm_norm1_g', 'm_w_in', 'm_q_norm_g', 'm_k_norm_g', 'm_hg_lower_bounds', 'm_hg_norm_g', 'm_sg_norm_g', 'm_sg_w', 'm_sg_b', 'm_w_out', 'm_norm2_g', 'm_w_up', 'm_conv_w', 'm_conv_b', 'm_w_down', 'm_final_norm_g', 'v_c_ctx', 'v_w_ada', 'v_b_ada', 'v_norm1_g', 'v_w_in', 'v_q_norm_g', 'v_k_norm_g', 'v_hg_lower_bounds', 'v_hg_norm_g', 'v_sg_norm_g', 'v_sg_w', 'v_sg_b', 'v_w_out', 'v_norm2_g', 'v_w_up', 'v_conv_w', 'v_conv_b', 'v_w_down', 'v_final_norm_g']
TWIN_OUTPUTS = ['loss', 'grad_x', 'grad_c_ctx', 'grad_w_ada', 'grad_b_ada', 'grad_norm1_g', 'grad_w_in', 'grad_q_norm_g', 'grad_k_norm_g', 'grad_hg_lower_bounds', 'grad_hg_norm_g', 'grad_sg_norm_g', 'grad_sg_w', 'grad_sg_b', 'grad_w_out', 'grad_norm2_g', 'grad_w_up', 'grad_conv_w', 'grad_conv_b', 'grad_w_down', 'grad_final_norm_g', 'delta_c_ctx', 'delta_w_ada', 'delta_b_ada', 'delta_norm1_g', 'delta_w_in', 'delta_q_norm_g', 'delta_k_norm_g', 'delta_hg_lower_bounds', 'delta_hg_norm_g', 'delta_sg_norm_g', 'delta_sg_w', 'delta_sg_b', 'delta_w_out', 'delta_norm2_g', 'delta_w_up', 'delta_conv_w', 'delta_conv_b', 'delta_w_down', 'delta_final_norm_g', 'new_m_c_ctx', 'new_m_w_ada', 'new_m_b_ada', 'new_m_norm1_g', 'new_m_w_in', 'new_m_q_norm_g', 'new_m_k_norm_g', 'new_m_hg_lower_bounds', 'new_m_hg_norm_g', 'new_m_sg_norm_g', 'new_m_sg_w', 'new_m_sg_b', 'new_m_w_out', 'new_m_norm2_g', 'new_m_w_up', 'new_m_conv_w', 'new_m_conv_b', 'new_m_w_down', 'new_m_final_norm_g', 'new_v_c_ctx', 'new_v_w_ada', 'new_v_b_ada', 'new_v_norm1_g', 'new_v_w_in', 'new_v_q_norm_g', 'new_v_k_norm_g', 'new_v_hg_lower_bounds', 'new_v_hg_norm_g', 'new_v_sg_norm_g', 'new_v_sg_w', 'new_v_sg_b', 'new_v_w_out', 'new_v_norm2_g', 'new_v_w_up', 'new_v_conv_w', 'new_v_conv_b', 'new_v_w_down', 'new_v_final_norm_g']
TWIN_LEAF_KINDS = {'loss': 'loss', 'grad_x': 'grad_x', 'grad_c_ctx': 'grad_w', 'grad_w_ada': 'grad_w', 'grad_b_ada': 'grad_w', 'grad_norm1_g': 'grad_w', 'grad_w_in': 'grad_w', 'grad_q_norm_g': 'grad_w', 'grad_k_norm_g': 'grad_w', 'grad_hg_lower_bounds': 'grad_w', 'grad_hg_norm_g': 'grad_w', 'grad_sg_norm_g': 'grad_w', 'grad_sg_w': 'grad_w', 'grad_sg_b': 'grad_w', 'grad_w_out': 'grad_w', 'grad_norm2_g': 'grad_w', 'grad_w_up': 'grad_w', 'grad_conv_w': 'grad_w', 'grad_conv_b': 'grad_w', 'grad_w_down': 'grad_w', 'grad_final_norm_g': 'grad_w', 'delta_c_ctx': 'delta_w', 'delta_w_ada': 'delta_w', 'delta_b_ada': 'delta_w', 'delta_norm1_g': 'delta_w', 'delta_w_in': 'delta_w', 'delta_q_norm_g': 'delta_w', 'delta_k_norm_g': 'delta_w', 'delta_hg_lower_bounds': 'delta_w', 'delta_hg_norm_g': 'delta_w', 'delta_sg_norm_g': 'delta_w', 'delta_sg_w': 'delta_w', 'delta_sg_b': 'delta_w', 'delta_w_out': 'delta_w', 'delta_norm2_g': 'delta_w', 'delta_w_up': 'delta_w', 'delta_conv_w': 'delta_w', 'delta_conv_b': 'delta_w', 'delta_w_down': 'delta_w', 'delta_final_norm_g': 'delta_w', 'new_m_c_ctx': 'new_m', 'new_m_w_ada': 'new_m', 'new_m_b_ada': 'new_m', 'new_m_norm1_g': 'new_m', 'new_m_w_in': 'new_m', 'new_m_q_norm_g': 'new_m', 'new_m_k_norm_g': 'new_m', 'new_m_hg_lower_bounds': 'new_m', 'new_m_hg_norm_g': 'new_m', 'new_m_sg_norm_g': 'new_m', 'new_m_sg_w': 'new_m', 'new_m_sg_b': 'new_m', 'new_m_w_out': 'new_m', 'new_m_norm2_g': 'new_m', 'new_m_w_up': 'new_m', 'new_m_conv_w': 'new_m', 'new_m_conv_b': 'new_m', 'new_m_w_down': 'new_m', 'new_m_final_norm_g': 'new_m', 'new_v_c_ctx': 'new_v', 'new_v_w_ada': 'new_v', 'new_v_b_ada': 'new_v', 'new_v_norm1_g': 'new_v', 'new_v_w_in': 'new_v', 'new_v_q_norm_g': 'new_v', 'new_v_k_norm_g': 'new_v', 'new_v_hg_lower_bounds': 'new_v', 'new_v_hg_norm_g': 'new_v', 'new_v_sg_norm_g': 'new_v', 'new_v_sg_w': 'new_v', 'new_v_sg_b': 'new_v', 'new_v_w_out': 'new_v', 'new_v_norm2_g': 'new_v', 'new_v_w_up': 'new_v', 'new_v_conv_w': 'new_v', 'new_v_conv_b': 'new_v', 'new_v_w_down': 'new_v', 'new_v_final_norm_g': 'new_v'}


def _forward(args):
    return _fwd_reference(*[args[k] for k in FWD_PARAMS])


def _output_shape():
    def fwd():
        inp = _fwd_setup_inputs(0)
        return _fwd_reference(*[inp[k] for k in FWD_PARAMS])
    out = _jax.eval_shape(fwd)
    return out.shape, out.dtype

N_MICROBATCH = 1
ADAM_LR = 0.001
ADAM_B1 = 0.9
ADAM_B2 = 0.999
ADAM_EPS = 1e-08
ADAM_WD = 0.01
ADAM_STEP = 10
PER_EXAMPLE_BATCH_AXIS = {'x': 0, 'c': 0, 'ctx': 0, 'loss_target': 0}
SHARED_INPUTS = []
_WEIGHT_DTYPES = {'c_ctx': _jnp.float32, 'w_ada': _jnp.float32, 'b_ada': _jnp.float32, 'norm1_g': _jnp.float32, 'w_in': _jnp.float32, 'q_norm_g': _jnp.float32, 'k_norm_g': _jnp.float32, 'hg_lower_bounds': _jnp.float32, 'hg_norm_g': _jnp.float32, 'sg_norm_g': _jnp.float32, 'sg_w': _jnp.float32, 'sg_b': _jnp.float32, 'w_out': _jnp.float32, 'norm2_g': _jnp.float32, 'w_up': _jnp.float32, 'conv_w': _jnp.float32, 'conv_b': _jnp.float32, 'w_down': _jnp.float32, 'final_norm_g': _jnp.float32}
MOMENT_SCALE = {'c_ctx': 6.793433e-03, 'w_ada': 2.438623e-02, 'b_ada': 4.153337e-02, 'norm1_g': 1.979228e-02, 'w_in': 1.372525e-02, 'q_norm_g': 5.933013e-03, 'k_norm_g': 6.010354e-03, 'hg_lower_bounds': 6.837688e-04, 'hg_norm_g': 3.402202e-02, 'sg_norm_g': 1.931203e-02, 'sg_w': 1.883430e-02, 'sg_b': 1.871441e-02, 'w_out': 1.685447e-02, 'norm2_g': 2.506786e-02, 'w_up': 1.113929e-02, 'conv_w': 1.122029e-02, 'conv_b': 1.027878e-02, 'w_down': 1.820380e-02, 'final_norm_g': 1.601768e+01}


def _to_microbatches(a, axis):
    t = _jnp.moveaxis(a, axis, 0)
    t = t.reshape((N_MICROBATCH, t.shape[0] // N_MICROBATCH) + t.shape[1:])
    return _jnp.moveaxis(t, 1, axis + 1)


def setup_inputs(seed: int = 0) -> dict:
    inp = _fwd_setup_inputs(seed)
    key = _jax.random.fold_in(_jax.random.key(seed), 7919)
    shape, _ = _output_shape()
    out = dict(inp)
    out["loss_target"] = _jax.random.normal(_jax.random.fold_in(key, 0), shape, _jnp.float32)
    for i, name in enumerate(TWIN_WEIGHTS):
        w = inp[name].astype(_jnp.float32)
        if MOMENT_SCALE is None:
            s = _jnp.sqrt(_jnp.mean(_jnp.square(w)) + 1e-30)
        else:
            s = MOMENT_SCALE[name]
        km, kv = _jax.random.split(_jax.random.fold_in(key, i + 1))
        out[name] = w
        out["m_" + name] = s * _jax.random.normal(km, w.shape, _jnp.float32)
        out["v_" + name] = (s * s) * _jax.random.uniform(kv, w.shape, _jnp.float32, 0.5, 1.5)
    if N_MICROBATCH > 1:
        for name, axis in PER_EXAMPLE_BATCH_AXIS.items():
            out[name] = _to_microbatches(out[name], axis)
    return {'x': out['x'], 'c': out['c'], 'ctx': out['ctx'], 'c_ctx': out['c_ctx'], 'w_ada': out['w_ada'], 'b_ada': out['b_ada'], 'norm1_g': out['norm1_g'], 'w_in': out['w_in'], 'q_norm_g': out['q_norm_g'], 'k_norm_g': out['k_norm_g'], 'hg_lower_bounds': out['hg_lower_bounds'], 'hg_norm_g': out['hg_norm_g'], 'sg_norm_g': out['sg_norm_g'], 'sg_w': out['sg_w'], 'sg_b': out['sg_b'], 'w_out': out['w_out'], 'norm2_g': out['norm2_g'], 'w_up': out['w_up'], 'conv_w': out['conv_w'], 'conv_b': out['conv_b'], 'w_down': out['w_down'], 'final_norm_g': out['final_norm_g'], 'loss_target': out['loss_target'], 'm_c_ctx': out['m_c_ctx'], 'm_w_ada': out['m_w_ada'], 'm_b_ada': out['m_b_ada'], 'm_norm1_g': out['m_norm1_g'], 'm_w_in': out['m_w_in'], 'm_q_norm_g': out['m_q_norm_g'], 'm_k_norm_g': out['m_k_norm_g'], 'm_hg_lower_bounds': out['m_hg_lower_bounds'], 'm_hg_norm_g': out['m_hg_norm_g'], 'm_sg_norm_g': out['m_sg_norm_g'], 'm_sg_w': out['m_sg_w'], 'm_sg_b': out['m_sg_b'], 'm_w_out': out['m_w_out'], 'm_norm2_g': out['m_norm2_g'], 'm_w_up': out['m_w_up'], 'm_conv_w': out['m_conv_w'], 'm_conv_b': out['m_conv_b'], 'm_w_down': out['m_w_down'], 'm_final_norm_g': out['m_final_norm_g'], 'v_c_ctx': out['v_c_ctx'], 'v_w_ada': out['v_w_ada'], 'v_b_ada': out['v_b_ada'], 'v_norm1_g': out['v_norm1_g'], 'v_w_in': out['v_w_in'], 'v_q_norm_g': out['v_q_norm_g'], 'v_k_norm_g': out['v_k_norm_g'], 'v_hg_lower_bounds': out['v_hg_lower_bounds'], 'v_hg_norm_g': out['v_hg_norm_g'], 'v_sg_norm_g': out['v_sg_norm_g'], 'v_sg_w': out['v_sg_w'], 'v_sg_b': out['v_sg_b'], 'v_w_out': out['v_w_out'], 'v_norm2_g': out['v_norm2_g'], 'v_w_up': out['v_w_up'], 'v_conv_w': out['v_conv_w'], 'v_conv_b': out['v_conv_b'], 'v_w_down': out['v_w_down'], 'v_final_norm_g': out['v_final_norm_g']}


def _loss(weights, diff, rest, loss_target):
    with _jax.named_scope("forward"):
        args = {**rest, TWIN_DIFF_INPUT: diff, **{k: w.astype(_WEIGHT_DTYPES[k]) for k, w in weights.items()}}
        y = _forward(args)
    with _jax.named_scope("loss_head"):
        err = _jnp.square(y.astype(_jnp.float32) - loss_target)
        return 0.5 * _jnp.sum(_jnp.mean(err, axis=-1)) if err.ndim else 0.5 * err


def _adamw(w, g, m, v):
    m = ADAM_B1 * m + (1.0 - ADAM_B1) * g
    v = ADAM_B2 * v + (1.0 - ADAM_B2) * _jnp.square(g)
    m_hat = m / (1.0 - ADAM_B1 ** ADAM_STEP)
    v_hat = v / (1.0 - ADAM_B2 ** ADAM_STEP)
    delta = -ADAM_LR * (m_hat / (_jnp.sqrt(v_hat) + ADAM_EPS) + ADAM_WD * w)
    return delta, m, v


def reference(x, c, ctx, c_ctx, w_ada, b_ada, norm1_g, w_in, q_norm_g, k_norm_g, hg_lower_bounds, hg_norm_g, sg_norm_g, sg_w, sg_b, w_out, norm2_g, w_up, conv_w, conv_b, w_down, final_norm_g, loss_target, m_c_ctx, m_w_ada, m_b_ada, m_norm1_g, m_w_in, m_q_norm_g, m_k_norm_g, m_hg_lower_bounds, m_hg_norm_g, m_sg_norm_g, m_sg_w, m_sg_b, m_w_out, m_norm2_g, m_w_up, m_conv_w, m_conv_b, m_w_down, m_final_norm_g, v_c_ctx, v_w_ada, v_b_ada, v_norm1_g, v_w_in, v_q_norm_g, v_k_norm_g, v_hg_lower_bounds, v_hg_norm_g, v_sg_norm_g, v_sg_w, v_sg_b, v_w_out, v_norm2_g, v_w_up, v_conv_w, v_conv_b, v_w_down, v_final_norm_g):
    given = dict(x=x, c=c, ctx=ctx, c_ctx=c_ctx, w_ada=w_ada, b_ada=b_ada, norm1_g=norm1_g, w_in=w_in, q_norm_g=q_norm_g, k_norm_g=k_norm_g, hg_lower_bounds=hg_lower_bounds, hg_norm_g=hg_norm_g, sg_norm_g=sg_norm_g, sg_w=sg_w, sg_b=sg_b, w_out=w_out, norm2_g=norm2_g, w_up=w_up, conv_w=conv_w, conv_b=conv_b, w_down=w_down, final_norm_g=final_norm_g, loss_target=loss_target, m_c_ctx=m_c_ctx, m_w_ada=m_w_ada, m_b_ada=m_b_ada, m_norm1_g=m_norm1_g, m_w_in=m_w_in, m_q_norm_g=m_q_norm_g, m_k_norm_g=m_k_norm_g, m_hg_lower_bounds=m_hg_lower_bounds, m_hg_norm_g=m_hg_norm_g, m_sg_norm_g=m_sg_norm_g, m_sg_w=m_sg_w, m_sg_b=m_sg_b, m_w_out=m_w_out, m_norm2_g=m_norm2_g, m_w_up=m_w_up, m_conv_w=m_conv_w, m_conv_b=m_conv_b, m_w_down=m_w_down, m_final_norm_g=m_final_norm_g, v_c_ctx=v_c_ctx, v_w_ada=v_w_ada, v_b_ada=v_b_ada, v_norm1_g=v_norm1_g, v_w_in=v_w_in, v_q_norm_g=v_q_norm_g, v_k_norm_g=v_k_norm_g, v_hg_lower_bounds=v_hg_lower_bounds, v_hg_norm_g=v_hg_norm_g, v_sg_norm_g=v_sg_norm_g, v_sg_w=v_sg_w, v_sg_b=v_sg_b, v_w_out=v_w_out, v_norm2_g=v_norm2_g, v_w_up=v_w_up, v_conv_w=v_conv_w, v_conv_b=v_conv_b, v_w_down=v_w_down, v_final_norm_g=v_final_norm_g)
    weights = {n: given[n] for n in TWIN_WEIGHTS}
    shared = {n: given[n] for n in SHARED_INPUTS}
    per_example = {n: given[n] for n in ['x', 'c', 'ctx']}
    grad_fn = _jax.value_and_grad(_loss, argnums=(0, 1))

    def one_microbatch(ex, loss_target):
        ex = dict(ex)
        diff = ex.pop(TWIN_DIFF_INPUT)
        return grad_fn(weights, diff, {**shared, **ex}, loss_target)

    if N_MICROBATCH == 1:
        loss, (grad_w, grad_x) = one_microbatch(per_example, given["loss_target"])
    else:
        def body(carry, xs):
            loss_sum, grad_sum = carry
            l_k, (gw_k, gx_k) = one_microbatch(xs[0], xs[1])
            with _jax.named_scope("update"):
                return (loss_sum + l_k, _jax.tree.map(_jnp.add, grad_sum, gw_k)), gx_k

        init = (_jnp.zeros((), _jnp.float32), _jax.tree.map(_jnp.zeros_like, weights))
        (loss, grad_w), grad_x = _jax.lax.scan(body, init, (per_example, given["loss_target"]))
    with _jax.named_scope("update"):
        delta_w, new_m, new_v = {}, {}, {}
        for n in TWIN_WEIGHTS:
            delta_w[n], new_m[n], new_v[n] = _adamw(weights[n], grad_w[n], given["m_" + n], given["v_" + n])
    return (loss, grad_x, *[grad_w[n] for n in TWIN_WEIGHTS], *[delta_w[n] for n in TWIN_WEIGHTS],
            *[new_m[n] for n in TWIN_WEIGHTS], *[new_v[n] for n in TWIN_WEIGHTS])
```

```python
import functools

import jax
import jax.numpy as jnp
from jax import lax
from jax.experimental import pallas as pl
from jax.experimental.pallas import tpu as pltpu

f32 = jnp.float32
bf16 = jnp.bfloat16
HI = lax.Precision.HIGHEST
MESH = pl.DeviceIdType.MESH

EPS = 1e-6
F_MIN = 1e-30
GRID_W = 64
ROPE_THETA = 10000.0
HEAD = 128
ATTN_HEADS, ATTN_KV = 8, 2
ATTN_GROUP = ATTN_HEADS // ATTN_KV
HG_HEADS = 4
SG_GROUPS = 4
SG_CHUNK = 128
HG_CHUNK = 16
HG_GROUP = 16
IN_SIZES = (1024, 256, 256, 512, 512, 512, 512, 512, 512, 512)
N_DEV = 8
ROW_BLOCK = 256
ADAM_LR, ADAM_B1, ADAM_B2, ADAM_EPS, ADAM_WD, ADAM_STEP = 0.001, 0.9, 0.999, 1e-08, 0.01, 10

WEIGHTS = ['c_ctx', 'w_ada', 'b_ada', 'norm1_g', 'w_in', 'q_norm_g', 'k_norm_g', 'hg_lower_bounds', 'hg_norm_g',
           'sg_norm_g', 'sg_w', 'sg_b', 'w_out', 'norm2_g', 'w_up', 'conv_w', 'conv_b', 'w_down', 'final_norm_g']


def _pick(dim, cands):
    for t in cands:
        if dim % t == 0:
            return t
    return dim


def _my_index():
    return 4 * lax.axis_index("x") + 2 * lax.axis_index("y") + lax.axis_index("c")


def _mm_call(a, b, mode, out_dtype, name):
    if mode == "nn":
        (M, K), N = a.shape, b.shape[1]
    elif mode == "nt":
        (M, K), N = a.shape, b.shape[0]
    else:
        (K, M), N = a.shape, b.shape[1]
    tm = _pick(M, (1088, 1024, 512, 256, 128))
    tn = _pick(N, (1024, 512, 256, 128))
    tk = _pick(K, (1088, 512, 256, 128))
    nk = K // tk
    dims = {"nn": (((1,), (0,)), ((), ())), "nt": (((1,), (1,)), ((), ())), "tn": (((0,), (0,)), ((), ()))}[mode]

    def body(a_ref, b_ref, o_ref, acc):
        k = pl.program_id(2)

        @pl.when(k == 0)
        def _():
            acc[...] = jnp.zeros_like(acc)

        acc[...] += lax.dot_general(a_ref[...].astype(bf16), b_ref[...].astype(bf16), dims, preferred_element_type=f32)

        @pl.when(k == nk - 1)
        def _():
            o_ref[...] = acc[...].astype(o_ref.dtype)

    a_spec = pl.BlockSpec((tk, tm), lambda i, j, k: (k, i)) if mode == "tn" else pl.BlockSpec((tm, tk), lambda i, j, k: (i, k))
    b_spec = pl.BlockSpec((tn, tk), lambda i, j, k: (j, k)) if mode == "nt" else pl.BlockSpec((tk, tn), lambda i, j, k: (k, j))
    return pl.pallas_call(
        body, name=name, grid=(M // tm, N // tn, nk),
        in_specs=[a_spec, b_spec], out_specs=pl.BlockSpec((tm, tn), lambda i, j, k: (i, j)),
        out_shape=jax.ShapeDtypeStruct((M, N), out_dtype),
        scratch_shapes=[pltpu.VMEM((tm, tn), f32)],
        compiler_params=pltpu.CompilerParams(dimension_semantics=("parallel", "parallel", "arbitrary")),
    )(a, b)


def mm(a, w, name):
    @jax.custom_vjp
    def op(a, w):
        return _mm_call(a, w, "nn", f32, name + "_fwd")

    def fwd(a, w):
        return op(a, w), (a, w)

    def bwd(res, dy):
        a, w = res
        da = _mm_call(dy, w, "nt", a.dtype, name + "_bwd_a")
        dw = _mm_call(a, dy, "tn", w.dtype, name + "_bwd_w")
        return da, dw

    op.defvjp(fwd, bwd)
    return op(a, w)


def _rowwise_specs(rows, consts, sels, pars, tb, nlb):
    specs = [pl.BlockSpec((tb, a.shape[1]), lambda i: (i, 0)) for a in (*rows, *consts)]
    specs += [pl.BlockSpec((None,) + a.shape[1:], lambda i: (jnp.where(i >= nlb, 1, 0), 0, 0)) for a in sels]
    specs += [pl.BlockSpec(a.shape, functools.partial(lambda i, n: (0,) * n, n=a.ndim)) for a in pars]
    return specs


def rowwise(name, f, rows, consts=(), sels=(), pars=(), outs=(), n_lat=None, tb=ROW_BLOCK):
    rows, consts, sels, pars = tuple(rows), tuple(consts), tuple(sels), tuple(pars)
    R = rows[0].shape[0]
    nb = R // tb
    nlb = nb if n_lat is None else n_lat // tb
    n_in = len(rows) + len(consts) + len(sels) + len(pars)
    n_out = len(outs)
    out_dtypes = [d for _, d in outs]
    out_specs = [pl.BlockSpec((tb, w), lambda i: (i, 0)) for w, _ in outs]
    out_shape = [jax.ShapeDtypeStruct((R, w), d) for w, d in outs]
    sem = pltpu.CompilerParams(dimension_semantics=("arbitrary",))

    def run_fwd(rows, consts, sels, pars):
        def body(*refs):
            res = f(*[r[...] for r in refs[:n_in]])
            for o_ref, r in zip(refs[n_in:], res):
                o_ref[...] = r.astype(o_ref.dtype)

        return tuple(pl.pallas_call(
            body, name=name + "_fwd", grid=(nb,), in_specs=_rowwise_specs(rows, consts, sels, pars, tb, nlb),
            out_specs=out_specs, out_shape=out_shape, compiler_params=sem,
        )(*rows, *consts, *sels, *pars))

    def run_bwd(rows, consts, sels, pars, cts):
        nr, nc, ns, npar = len(rows), len(consts), len(sels), len(pars)

        def body(*refs):
            i = pl.program_id(0)
            ins = [r[...] for r in refs[:n_in]]
            ct = tuple(r[...] for r in refs[n_in:n_in + n_out])
            o_refs = refs[n_in + n_out:]
            cvals = ins[nr:nr + nc]

            def g(*d):
                res = f(*d[:nr], *cvals, *d[nr:])
                return tuple(r.astype(t) for r, t in zip(res, out_dtypes))

            _, vjp = jax.vjp(g, *ins[:nr], *ins[nr + nc:])
            grads = vjp(ct)
            for k in range(nr):
                o_refs[k][...] = grads[k].astype(o_refs[k].dtype)
            for k in range(nr, nr + ns + npar):
                first = (i == 0) | (i == nlb) if k < nr + ns else (i == 0)
                gk = grads[k].astype(f32)

                @pl.when(first)
                def _(k=k, gk=gk):
                    o_refs[k][...] = gk

                @pl.when(jnp.logical_not(first))
                def _(k=k, gk=gk):
                    o_refs[k][...] += gk

        in_specs = _rowwise_specs(rows, consts, sels, pars, tb, nlb) + out_specs
        o_specs = [pl.BlockSpec((tb, a.shape[1]), lambda i: (i, 0)) for a in rows]
        o_specs += [pl.BlockSpec((None,) + a.shape[1:], lambda i: (jnp.where(i >= nlb, 1, 0), 0, 0)) for a in sels]
        o_specs += [pl.BlockSpec(a.shape, functools.partial(lambda i, n: (0,) * n, n=a.ndim)) for a in pars]
        o_shape = [jax.ShapeDtypeStruct(a.shape, a.dtype) for a in rows]
        o_shape += [jax.ShapeDtypeStruct(a.shape, f32) for a in (*sels, *pars)]
        res = pl.pallas_call(
            body, name=name + "_bwd", grid=(nb,), in_specs=in_specs, out_specs=o_specs, out_shape=o_shape,
            compiler_params=sem,
        )(*rows, *consts, *sels, *pars, *cts)
        return tuple(res[:nr]), tuple(res[nr:nr + ns]), tuple(res[nr + ns:])

    @jax.custom_vjp
    def op(rows, consts, sels, pars):
        return run_fwd(rows, consts, sels, pars)

    def fwd(rows, consts, sels, pars):
        return op(rows, consts, sels, pars), (rows, consts, sels, pars)

    def bwd(res, cts):
        rows, consts, sels, pars = res
        drows, dsels, dpars = run_bwd(rows, consts, sels, pars, tuple(cts))
        return drows, tuple(jnp.zeros_like(c) for c in consts), dsels, dpars

    op.defvjp(fwd, bwd)
    return op(rows, consts, sels, pars)


def _rms(x, g):
    return x * lax.rsqrt(jnp.mean(x * x, axis=-1, keepdims=True) + EPS) * g


def _silu(x):
    return x * jax.nn.sigmoid(x)


def f_modnorm(x, mods, g):
    return (_rms(x, g) * (1.0 + mods[1:2]) + mods[0:1],)


def f_resid_modnorm(x, y, mods, g):
    xn = x + mods[0:1] * y
    return xn, _rms(xn, g) * (1.0 + mods[2:3]) + mods[1:2]


def f_resid_final(x, y, tgt, mods, g):
    xn = x + mods[0:1] * y
    err = _rms(xn, g) - tgt
    return (jnp.mean(err * err, axis=-1, keepdims=True),)


def f_qkprep(aq, ak, av, cosf, sinf, qg, kg):
    r = lax.broadcasted_iota(jnp.int32, (HEAD, HEAD), 0)
    c = lax.broadcasted_iota(jnp.int32, (HEAD, HEAD), 1)
    swap = jnp.where((r ^ 1) == c, 1.0, 0.0).astype(f32)

    def head(xh, g):
        y = _rms(xh, g)
        ys = jnp.dot(y, swap, precision=HI, preferred_element_type=f32)
        return y * cosf + ys * sinf

    q = jnp.concatenate([head(aq[:, h * HEAD:(h + 1) * HEAD], qg) for h in range(ATTN_HEADS)], axis=1)
    k = jnp.concatenate([head(ak[:, h * HEAD:(h + 1) * HEAD], kg) for h in range(ATTN_KV)], axis=1)
    return q, k, av


def f_hgout(of, ob, gt, g):
    o = of + ob
    y = jnp.concatenate([_rms(o[:, h * HEAD:(h + 1) * HEAD], g) for h in range(HG_HEADS)], axis=1)
    return (y * _silu(gt),)


def f_sgate(u, v, g, w, b0, b1, b2, b3):
    u = jax.nn.gelu(u)
    v = jax.nn.gelu(v)
    bs = (b0, b1, b2, b3)
    cols = []
    for gi in range(SG_GROUPS):
        sl = slice(gi * HEAD, (gi + 1) * HEAD)
        vg = _rms(v[:, sl], g[:, sl])
        parts = []
        for n in range(v.shape[0] // SG_CHUNK):
            vc = vg[n * SG_CHUNK:(n + 1) * SG_CHUNK]
            parts.append(jnp.dot(w[gi].astype(bf16), vc.astype(bf16), preferred_element_type=f32) + bs[gi])
        cols.append(jnp.concatenate(parts, axis=0))
    return (u * jnp.concatenate(cols, axis=1),)


def _attn_probs(q, k, i, nlb, T, scale):
    s = lax.dot_general(q, k, (((1,), (1,)), ((), ())), preferred_element_type=f32) * scale
    col = lax.broadcasted_iota(jnp.int32, s.shape, 1)
    s = jnp.where((i < nlb) | (col >= T), s, -1e30)
    p = jnp.exp(s - jnp.max(s, axis=-1, keepdims=True))
    return p, jnp.sum(p, axis=-1, keepdims=True)


def attention(q, k, v, T, name, tq=ROW_BLOCK):
    NT = q.shape[0]
    nqb, nlb = NT // tq, T // tq
    scale = HEAD ** -0.5
    q_spec = pl.BlockSpec((tq, HEAD), lambda kv, g, i: (i, kv * ATTN_GROUP + g))
    kv_spec = pl.BlockSpec((NT, HEAD), lambda kv, g, i: (0, kv))
    grid = (ATTN_KV, ATTN_GROUP, nqb)

    def run_fwd(q, k, v):
        def body(q_ref, k_ref, v_ref, o_ref):
            p, l = _attn_probs(q_ref[...], k_ref[...], pl.program_id(2), nlb, T, scale)
            o = jnp.dot(p.astype(bf16), v_ref[...], preferred_element_type=f32) / l
            o_ref[...] = o.astype(o_ref.dtype)

        return pl.pallas_call(
            body, name=name + "_fwd", grid=grid, in_specs=[q_spec, kv_spec, kv_spec], out_specs=q_spec,
            out_shape=jax.ShapeDtypeStruct(q.shape, bf16),
            compiler_params=pltpu.CompilerParams(dimension_semantics=("parallel", "parallel", "arbitrary")),
        )(q, k, v)

    def run_bwd(q, k, v, do):
        def body(q_ref, k_ref, v_ref, do_ref, dq_ref, dk_ref, dv_ref):
            g, i = pl.program_id(1), pl.program_id(2)

            @pl.when((g == 0) & (i == 0))
            def _():
                dk_ref[...] = jnp.zeros_like(dk_ref)
                dv_ref[...] = jnp.zeros_like(dv_ref)

            qb, kb, vb, dob = q_ref[...], k_ref[...], v_ref[...], do_ref[...]
            p, l = _attn_probs(qb, kb, i, nlb, T, scale)
            p = p / l
            dp = lax.dot_general(dob, vb, (((1,), (1,)), ((), ())), preferred_element_type=f32)
            ds = (p * (dp - jnp.sum(p * dp, axis=-1, keepdims=True)) * scale).astype(bf16)
            dq_ref[...] = jnp.dot(ds, kb, preferred_element_type=f32).astype(dq_ref.dtype)
            dk_ref[...] += lax.dot_general(ds, qb, (((0,), (0,)), ((), ())), preferred_element_type=f32)
            dv_ref[...] += lax.dot_general(p.astype(bf16), dob, (((0,), (0,)), ((), ())), preferred_element_type=f32)

        return pl.pallas_call(
            body, name=name + "_bwd", grid=grid, in_specs=[q_spec, kv_spec, kv_spec, q_spec],
            out_specs=[q_spec, kv_spec, kv_spec],
            out_shape=[jax.ShapeDtypeStruct(q.shape, bf16), jax.ShapeDtypeStruct(k.shape, f32), jax.ShapeDtypeStruct(v.shape, f32)],
            compiler_params=pltpu.CompilerParams(dimension_semantics=("parallel", "arbitrary", "arbitrary")),
        )(q, k, v, do)

    @jax.custom_vjp
    def op(q, k, v):
        return run_fwd(q, k, v)

    def fwd(q, k, v):
        return op(q, k, v), (q, k, v)

    def bwd(res, do):
        dq, dk, dv = run_bwd(*res, do)
        return dq, dk.astype(bf16), dv.astype(bf16)

    op.defvjp(fwd, bwd)
    return op(q, k, v)


def _hg_group(St, hq, hf, hi, lb, *, rev):
    G, C = HG_GROUP, HG_CHUNK
    R = G * C
    q = _silu(hq)
    f = lb + (1.0 - lb) * jax.nn.sigmoid(hf)
    logf = jnp.log(jnp.maximum(f, F_MIN))
    kk = (1.0 - lb) * jax.nn.sigmoid(-hf)
    r = lax.broadcasted_iota(jnp.int32, (R, R), 0)
    c = lax.broadcasted_iota(jnp.int32, (R, R), 1)
    tri = (c >= r) if rev else (c <= r)
    cum = jnp.where(((r // C) == (c // C)) & tri, 1.0, 0.0).astype(f32)
    b3 = jnp.dot(cum, logf, precision=HI, preferred_element_type=f32).reshape(G, C, HEAD)
    q3, k3, v3 = q.reshape(G, C, HEAD), kk.reshape(G, C, HEAD), hi.reshape(G, C, HEAD)
    btot = jnp.sum(logf.reshape(G, C, HEAD), axis=1)
    tt = lax.broadcasted_iota(jnp.int32, (G, C, C, HEAD), 1)
    ss = lax.broadcasted_iota(jnp.int32, (G, C, C, HEAD), 2)
    mask = (ss >= tt) if rev else (ss <= tt)
    diff = b3[:, :, None, :] - b3[:, None, :, :]
    dec = jnp.where(mask, jnp.exp(jnp.where(mask, diff, 0.0)), 0.0)
    scores = jnp.sum(q3[:, :, None, :] * k3[:, None, :, :] * dec, axis=-1)
    o_intra = lax.dot_general(scores, v3, (((2,), (1,)), ((0,), (0,))), precision=HI, preferred_element_type=f32)
    q_dec = q3 * jnp.exp(b3)
    k_dec = k3 * jnp.exp(btot[:, None, :] - b3)
    kvt = lax.dot_general(v3, k_dec, (((1,), (1,)), ((0,), (0,))), precision=HI, preferred_element_type=f32)
    dl = jnp.exp(btot)
    states = [None] * G
    for g in (range(G - 1, -1, -1) if rev else range(G)):
        states[g] = St
        St = St * dl[g:g + 1, :] + kvt[g]
    o_inter = lax.dot_general(q_dec, jnp.stack(states), (((2,), (2,)), ((0,), (0,))), precision=HI, preferred_element_type=f32)
    return St, (o_intra + o_inter).reshape(R, HEAD)


def hgrn(hq, hf, hi, lb, T, rev, name):
    NT, W = hq.shape
    R = HG_GROUP * HG_CHUNK
    n_lat, n_ctx = T // R, (NT - T) // R
    nG = n_lat + n_ctx

    def group_of(j):
        if rev:
            return jnp.where(j < n_ctx, nG - 1 - j, n_lat - 1 - (j - n_ctx))
        return jnp.where(j < n_ctx, n_lat + j, j - n_ctx)

    def rows_of(j):
        return pl.ds(pl.multiple_of(group_of(j) * R, R), R)

    col_spec = pl.BlockSpec((NT, HEAD), lambda h: (0, h))
    lb_spec = pl.BlockSpec((1, HEAD), lambda h: (0, h))
    st_spec = pl.BlockSpec((None, nG, HEAD, HEAD), lambda h: (h, 0, 0, 0))
    sem = pltpu.CompilerParams(dimension_semantics=("parallel",))

    def run_fwd(hq, hf, hi, lb):
        def body(hq_ref, hf_ref, hi_ref, lb_ref, o_ref, st_ref):
            def step(j, St):
                st_ref[j] = St
                rows = rows_of(j)
                St, o = _hg_group(St, hq_ref[rows, :], hf_ref[rows, :], hi_ref[rows, :], lb_ref[...], rev=rev)
                o_ref[rows, :] = o
                return St

            lax.fori_loop(0, nG, step, jnp.zeros((HEAD, HEAD), f32))

        return pl.pallas_call(
            body, name=name + "_fwd", grid=(W // HEAD,), in_specs=[col_spec, col_spec, col_spec, lb_spec],
            out_specs=[col_spec, st_spec],
            out_shape=[jax.ShapeDtypeStruct((NT, W), f32), jax.ShapeDtypeStruct((W // HEAD, nG, HEAD, HEAD), f32)],
            compiler_params=sem,
        )(hq, hf, hi, lb)

    def run_bwd(hq, hf, hi, lb, st, do):
        def body(hq_ref, hf_ref, hi_ref, lb_ref, st_ref, do_ref, dq_ref, df_ref, di_ref, dlb_ref):
            def step(jj, carry):
                dS, dlb = carry
                j = nG - 1 - jj
                rows = rows_of(j)
                _, vjp = jax.vjp(functools.partial(_hg_group, rev=rev), st_ref[j], hq_ref[rows, :], hf_ref[rows, :],
                                 hi_ref[rows, :], lb_ref[...])
                dS, dq, df, di, dl = vjp((dS, do_ref[rows, :]))
                dq_ref[rows, :] = dq
                df_ref[rows, :] = df
                di_ref[rows, :] = di
                return dS, dlb + dl

            _, dlb = lax.fori_loop(0, nG, step, (jnp.zeros((HEAD, HEAD), f32), jnp.zeros((1, HEAD), f32)))
            dlb_ref[...] = dlb

        return pl.pallas_call(
            body, name=name + "_bwd", grid=(W // HEAD,),
            in_specs=[col_spec, col_spec, col_spec, lb_spec, st_spec, col_spec],
            out_specs=[col_spec, col_spec, col_spec, lb_spec],
            out_shape=[jax.ShapeDtypeStruct((NT, W), f32)] * 3 + [jax.ShapeDtypeStruct((1, W), f32)],
            compiler_params=sem,
        )(hq, hf, hi, lb, st, do)

    @jax.custom_vjp
    def op(hq, hf, hi, lb):
        return run_fwd(hq, hf, hi, lb)[0]

    def fwd(hq, hf, hi, lb):
        o, st = run_fwd(hq, hf, hi, lb)
        return o, (hq, hf, hi, lb, st)

    def bwd(res, do):
        return tuple(run_bwd(*res, do))

    op.defvjp(fwd, bwd)
    return op(hq, hf, hi, lb)


def lower_bounds(params):
    n = len(params)

    def f(*a):
        m = functools.reduce(jnp.maximum, a)
        e = [jnp.exp(x - m) for x in a]
        s = functools.reduce(lambda u, v: u + v, e)
        p = [x / s for x in e]
        out, run = [], jnp.zeros_like(p[0])
        for l in range(n):
            out.append(run)
            run = run + p[l]
        return tuple(out[l] + p[l] - p[0] for l in range(n))

    shape = [jax.ShapeDtypeStruct(params[0].shape, f32)] * n

    @jax.custom_vjp
    def op(*a):
        def body(*refs):
            for o_ref, r in zip(refs[n:], f(*[x[...] for x in refs[:n]])):
                o_ref[...] = r
        return tuple(pl.pallas_call(body, name="lower_bounds_fwd", out_shape=shape)(*a))

    def fwd(*a):
        return op(*a), a

    def bwd(a, cts):
        def body(*refs):
            _, vjp = jax.vjp(f, *[x[...] for x in refs[:n]])
            for o_ref, r in zip(refs[2 * n:], vjp(tuple(x[...] for x in refs[n:2 * n]))):
                o_ref[...] = r
        return tuple(pl.pallas_call(body, name="lower_bounds_bwd", out_shape=shape)(*a, *cts))

    op.defvjp(fwd, bwd)
    return op(*params)


def _shift_rows(x, d, T):
    n = x.shape[0]
    t = lax.broadcasted_iota(jnp.int32, x.shape, 0)
    y = pltpu.roll(x, d % n, 0)
    edge = ((t == 0) | (t == T)) if d == 1 else ((t == T - 1) | (t == n - 1))
    return jnp.where(edge, 0.0, y)


def _conv(x, w, b, T):
    return b + w[0:1] * _shift_rows(x, 1, T) + w[1:2] * x + w[2:3] * _shift_rows(x, -1, T)


def convact(up, cw, cb, T, name, tc=128):
    NT, F2 = up.shape
    F = F2 // 2
    tc = _pick(F, (tc, 128))
    nf = F // tc
    g_spec = lambda r: pl.BlockSpec((r, tc), lambda j: (0, j))
    v_spec = lambda r: pl.BlockSpec((r, tc), lambda j: (0, j + nf))
    sem = pltpu.CompilerParams(dimension_semantics=("parallel",))

    def run_fwd(up, cw, cb):
        def body(xg_ref, xv_ref, wg_ref, wv_ref, bg_ref, bv_ref, o_ref):
            yg = _conv(xg_ref[...], wg_ref[...], bg_ref[...], T)
            yv = _conv(xv_ref[...], wv_ref[...], bv_ref[...], T)
            o_ref[...] = (_silu(yg) * yv).astype(o_ref.dtype)

        return pl.pallas_call(
            body, name=name + "_fwd", grid=(nf,),
            in_specs=[g_spec(NT), v_spec(NT), g_spec(3), v_spec(3), g_spec(1), v_spec(1)], out_specs=g_spec(NT),
            out_shape=jax.ShapeDtypeStruct((NT, F), bf16), compiler_params=sem,
        )(up, up, cw, cw, cb, cb)

    def run_bwd(up, cw, cb, dact):
        def body(xg_ref, xv_ref, wg_ref, wv_ref, bg_ref, bv_ref, da_ref, dxg_ref, dxv_ref, dwg_ref, dwv_ref, dbg_ref, dbv_ref):
            xg, xv, wg, wv = xg_ref[...], xv_ref[...], wg_ref[...], wv_ref[...]
            yg = _conv(xg, wg, bg_ref[...], T)
            yv = _conv(xv, wv, bv_ref[...], T)
            da = da_ref[...].astype(f32)
            sg = jax.nn.sigmoid(yg)
            dyv = da * yg * sg
            dyg = da * yv * sg * (1.0 + yg * (1.0 - sg))
            for x, w, dy, dx_ref, dw_ref, db_ref in ((xg, wg, dyg, dxg_ref, dwg_ref, dbg_ref), (xv, wv, dyv, dxv_ref, dwv_ref, dbv_ref)):
                dx_ref[...] = w[0:1] * _shift_rows(dy, -1, T) + w[1:2] * dy + w[2:3] * _shift_rows(dy, 1, T)
                dw_ref[...] = jnp.concatenate([
                    jnp.sum(dy * _shift_rows(x, 1, T), axis=0, keepdims=True),
                    jnp.sum(dy * x, axis=0, keepdims=True),
                    jnp.sum(dy * _shift_rows(x, -1, T), axis=0, keepdims=True)], axis=0)
                db_ref[...] = jnp.sum(dy, axis=0, keepdims=True)

        return pl.pallas_call(
            body, name=name + "_bwd", grid=(nf,),
            in_specs=[g_spec(NT), v_spec(NT), g_spec(3), v_spec(3), g_spec(1), v_spec(1), g_spec(NT)],
            out_specs=[g_spec(NT), g_spec(NT), g_spec(3), g_spec(3), g_spec(1), g_spec(1)],
            out_shape=[jax.ShapeDtypeStruct((NT, F), f32)] * 2 + [jax.ShapeDtypeStruct((3, F), f32)] * 2 + [jax.ShapeDtypeStruct((1, F), f32)] * 2,
            compiler_params=sem,
        )(up, up, cw, cw, cb, cb, dact)

    @jax.custom_vjp
    def op(up, cw, cb):
        return run_fwd(up, cw, cb)

    def fwd(up, cw, cb):
        return op(up, cw, cb), (up, cw, cb)

    def bwd(res, dact):
        dxg, dxv, dwg, dwv, dbg, dbv = run_bwd(*res, dact)
        return (jnp.concatenate([dxg, dxv], axis=1), jnp.concatenate([dwg, dwv], axis=1), jnp.concatenate([dbg, dbv], axis=1))

    op.defvjp(fwd, bwd)
    return op(up, cw, cb)


def _peers():
    x, y, c = lax.axis_index("x"), lax.axis_index("y"), lax.axis_index("c")
    return (x, y, c), [(x, y, 1 - c), (1 - x, y, c), (x, 1 - y, c), (1 - x, 1 - y, c),
                       (1 - x, y, 1 - c), (x, 1 - y, 1 - c), (1 - x, 1 - y, 1 - c)]


def _index(dev):
    return 4 * dev[0] + 2 * dev[1] + dev[2]


def allgather_small(x, name):
    m, n = x.shape

    def body(x_ref, out_ref, send_sems, recv_sems, local_sem):
        me, peers = _peers()

        def rows(dev):
            return out_ref.at[pl.ds(pl.multiple_of(_index(dev) * m, 8), m), :]

        mine = pltpu.make_async_copy(x_ref, rows(me), local_sem)
        mine.start()
        sends = [pltpu.make_async_remote_copy(src_ref=x_ref, dst_ref=rows(me), send_sem=send_sems.at[k], recv_sem=recv_sems.at[k],
                                              device_id=p, device_id_type=MESH) for k, p in enumerate(peers)]
        for cp in sends:
            cp.start()
        for k, p in enumerate(peers):
            pltpu.make_async_remote_copy(src_ref=x_ref, dst_ref=rows(p), send_sem=send_sems.at[k], recv_sem=recv_sems.at[k],
                                         device_id=p, device_id_type=MESH).wait_recv()
        for cp in sends:
            cp.wait_send()
        mine.wait()

    return pl.pallas_call(
        body, name=name, out_shape=jax.ShapeDtypeStruct((N_DEV * m, n), x.dtype),
        in_specs=[pl.BlockSpec(memory_space=pltpu.VMEM)], out_specs=pl.BlockSpec(memory_space=pltpu.VMEM),
        scratch_shapes=[pltpu.SemaphoreType.DMA((7,)), pltpu.SemaphoreType.DMA((7,)), pltpu.SemaphoreType.DMA],
    )(x)


def _shard_of(ref, kind, j, width):
    if kind == "col":
        return ref.at[:, :, pl.ds(pl.multiple_of(j * width, 128), width)]
    return ref.at[:, pl.ds(pl.multiple_of(j * width, 8), width), :]


def allgather_big(shards, kinds, name):
    A = len(shards)
    widths = [s.shape[2] if k == "col" else s.shape[1] for s, k in zip(shards, kinds)]
    full = [jax.ShapeDtypeStruct((s.shape[0], s.shape[1] * (1 if k == "col" else N_DEV), s.shape[2] * (N_DEV if k == "col" else 1)), s.dtype)
            for s, k in zip(shards, kinds)]

    def body(*refs):
        ins, outs = refs[:A], refs[A:2 * A]
        send_sems, recv_sems, local_sems = refs[2 * A:]
        x, y, c = lax.axis_index("x"), lax.axis_index("y"), lax.axis_index("c")
        me, sibling = (x, y, c), (x, y, 1 - c)
        chips = [(1 - x, y), (x, 1 - y), (1 - x, 1 - y)]

        def blk(a, dev):
            return _shard_of(outs[a], kinds[a], _index(dev), widths[a])

        def copy(a, k, block, to, src=None):
            return pltpu.make_async_remote_copy(
                src_ref=blk(a, block) if src is None else src, dst_ref=blk(a, block),
                send_sem=send_sems.at[a, k], recv_sem=recv_sems.at[a, k], device_id=to, device_id_type=MESH)

        mine = [pltpu.make_async_copy(ins[a], blk(a, me), local_sems.at[a]) for a in range(A)]
        for cp in mine:
            cp.start()
        first = []
        for a in range(A):
            first.append(copy(a, 0, me, sibling, src=ins[a]))
            first += [copy(a, 1 + j, me, (*chip, c), src=ins[a]) for j, chip in enumerate(chips)]
        for cp in first:
            cp.start()
        passed = []
        for j, chip in enumerate(chips):
            for a in range(A):
                copy(a, 1 + j, (*chip, c), me).wait_recv()
                cp = copy(a, 4 + j, (*chip, c), sibling)
                cp.start()
                passed.append(cp)
        for a in range(A):
            copy(a, 0, sibling, me).wait_recv()
            for j, chip in enumerate(chips):
                copy(a, 4 + j, (*chip, 1 - c), me).wait_recv()
        for cp in first + passed:
            cp.wait_send()
        for cp in mine:
            cp.wait()

    any_spec = pl.BlockSpec(memory_space=pl.ANY)
    return pl.pallas_call(
        body, name=name, out_shape=full, in_specs=[any_spec] * A, out_specs=[any_spec] * A,
        scratch_shapes=[pltpu.SemaphoreType.DMA((A, 7)), pltpu.SemaphoreType.DMA((A, 7)), pltpu.SemaphoreType.DMA((A,))],
    )(*shards)


def alltoall_big(fulls, kinds, name):
    A = len(fulls)
    widths = [(f.shape[2] if k == "col" else f.shape[1]) // N_DEV for f, k in zip(fulls, kinds)]
    out = [jax.ShapeDtypeStruct((N_DEV, f.shape[0], f.shape[1] // (1 if k == "col" else N_DEV), f.shape[2] // (N_DEV if k == "col" else 1)), f.dtype)
           for f, k in zip(fulls, kinds)]

    def body(*refs):
        ins, outs = refs[:A], refs[A:2 * A]
        send_sems, recv_sems, local_sems = refs[2 * A:]
        me, peers = _peers()
        my = _index(me)
        mine = [pltpu.make_async_copy(_shard_of(ins[a], kinds[a], my, widths[a]), outs[a].at[my], local_sems.at[a]) for a in range(A)]
        for cp in mine:
            cp.start()
        sends = []
        for k, p in enumerate(peers):
            for a in range(A):
                sends.append(pltpu.make_async_remote_copy(
                    src_ref=_shard_of(ins[a], kinds[a], _index(p), widths[a]), dst_ref=outs[a].at[my],
                    send_sem=send_sems.at[a, k], recv_sem=recv_sems.at[a, k], device_id=p, device_id_type=MESH))
        for cp in sends:
            cp.start()
        for k, p in enumerate(peers):
            for a in range(A):
                pltpu.make_async_remote_copy(
                    src_ref=_shard_of(ins[a], kinds[a], my, widths[a]), dst_ref=outs[a].at[_index(p)],
                    send_sem=send_sems.at[a, k], recv_sem=recv_sems.at[a, k], device_id=p, device_id_type=MESH).wait_recv()
        for cp in sends:
            cp.wait_send()
        for cp in mine:
            cp.wait()

    any_spec = pl.BlockSpec(memory_space=pl.ANY)
    return pl.pallas_call(
        body, name=name, out_shape=out, in_specs=[any_spec] * A, out_specs=[any_spec] * A,
        scratch_shapes=[pltpu.SemaphoreType.DMA((A, 7)), pltpu.SemaphoreType.DMA((A, 7)), pltpu.SemaphoreType.DMA((A,))],
    )(*fulls)


def sum_slots(x, name):
    _, R, C = x.shape
    tr = _pick(R, (256, 128, 64, 32, 16, 8))

    def body(x_ref, o_ref):
        acc = x_ref[0].astype(f32)
        for d in range(1, N_DEV):
            acc = acc + x_ref[d].astype(f32)
        o_ref[...] = acc

    return pl.pallas_call(
        body, name=name, grid=(R // tr,), in_specs=[pl.BlockSpec((N_DEV, tr, C), lambda i: (0, i, 0))],
        out_specs=pl.BlockSpec((tr, C), lambda i: (i, 0)), out_shape=jax.ShapeDtypeStruct((R, C), f32),
        compiler_params=pltpu.CompilerParams(dimension_semantics=("parallel",)),
    )(x)


def gather_weights(shards, kinds):
    @jax.custom_vjp
    def op(*shards):
        return tuple(allgather_big([s.astype(bf16) for s in shards], kinds, "gather_weights"))

    def fwd(*shards):
        return op(*shards), None

    def bwd(_, dfull):
        slots = alltoall_big(list(dfull), kinds, "scatter_grads")
        return tuple(sum_slots(s.reshape(N_DEV, -1, s.shape[-1]), "sum_grads_%d" % a).reshape(s.shape[1:]) for a, s in enumerate(slots))

    op.defvjp(fwd, bwd)
    return op(*shards)


def allreduce_small(vals, name):
    flat = jnp.concatenate([v.reshape(-1) for v in vals])
    n = flat.shape[0]
    cols = 1024
    m = -(-n // (cols * 8)) * 8
    packed = jnp.pad(flat, (0, m * cols - n)).reshape(m, cols)
    total = sum_slots(allgather_small(packed, name).reshape(N_DEV, m, cols), name + "_sum").reshape(-1)
    out, off = [], 0
    for v in vals:
        out.append(total[off:off + v.size].reshape(v.shape))
        off += v.size
    return out


def ada_mod(c_all, c_ctx, w_ada, b_ada):
    L, D, S = w_ada.shape
    me = _my_index()

    def stacked(c_ctx):
        return jnp.concatenate([c_all, jnp.broadcast_to(c_ctx, (N_DEV, D))], axis=0)

    ts = _pick(S, (512, 384, 256, 128, 64))
    w_spec = pl.BlockSpec((None, D, ts), lambda l, j: (l, 0, j))
    c_spec = pl.BlockSpec((16, D), lambda l, j: (0, 0))
    p_spec = pl.BlockSpec((None, 16, ts), lambda l, j: (l, 0, j))

    def run_fwd(cin, w_ada):
        def body(c_ref, w_ref, o_ref):
            o_ref[...] = jnp.dot(_silu(c_ref[...]).astype(bf16), w_ref[...].astype(bf16), preferred_element_type=f32)

        return pl.pallas_call(
            body, name="ada_fwd", grid=(L, S // ts), in_specs=[c_spec, w_spec], out_specs=p_spec,
            out_shape=jax.ShapeDtypeStruct((L, 16, S), f32),
            compiler_params=pltpu.CompilerParams(dimension_semantics=("parallel", "parallel")),
        )(cin, w_ada)

    def run_bwd(cin, w_ada, dm):
        def body(c_ref, w_ref, dm_ref, gw_ref, dc_ref):
            first = (pl.program_id(0) == 0) & (pl.program_id(1) == 0)
            cv = c_ref[...]
            sg = jax.nn.sigmoid(cv)
            dmv = dm_ref[...].astype(bf16)
            gw_ref[...] = lax.dot_general((cv * sg).astype(bf16), dmv, (((0,), (0,)), ((), ())), preferred_element_type=f32)
            ds = lax.dot_general(dmv, w_ref[...].astype(bf16), (((1,), (1,)), ((), ())), preferred_element_type=f32)
            dc = ds * sg * (1.0 + cv * (1.0 - sg))

            @pl.when(first)
            def _():
                dc_ref[...] = dc

            @pl.when(jnp.logical_not(first))
            def _():
                dc_ref[...] += dc

        return pl.pallas_call(
            body, name="ada_bwd", grid=(L, S // ts), in_specs=[c_spec, w_spec, p_spec], out_specs=[w_spec, c_spec],
            out_shape=[jax.ShapeDtypeStruct((L, D, S), f32), jax.ShapeDtypeStruct((16, D), f32)],
            compiler_params=pltpu.CompilerParams(dimension_semantics=("arbitrary", "arbitrary")),
        )(cin, w_ada, dm)

    def bias_grad(dm_full):
        def body(x_ref, o_ref):
            o_ref[...] = jnp.sum(x_ref[...], axis=0, keepdims=True)

        return pl.pallas_call(
            body, name="ada_bias_grad", grid=(L,), in_specs=[pl.BlockSpec((None, 16, 6 * D), lambda l: (l, 0, 0))],
            out_specs=pl.BlockSpec((None, 1, 6 * D), lambda l: (l, 0, 0)), out_shape=jax.ShapeDtypeStruct((L, 1, 6 * D), f32),
        )(dm_full).reshape(L, 6 * D)

    @jax.custom_vjp
    def op(c_ctx, w_ada, b_ada):
        prod = run_fwd(stacked(c_ctx), w_ada)
        allp = allgather_small(prod.reshape(L * 16, S), "ada_gather").reshape(N_DEV, L, 16, S)
        allp = allp.transpose(1, 2, 0, 3).reshape(L, 16, N_DEV * S)
        mine = lax.dynamic_index_in_dim(allp, me, axis=1, keepdims=False) + b_ada
        ctx = allp[:, N_DEV] + b_ada
        return jnp.stack([mine, ctx], axis=1).reshape(L, 2, 6, D)

    def fwd(c_ctx, w_ada, b_ada):
        return op(c_ctx, w_ada, b_ada), (c_ctx, w_ada)

    def bwd(res, dmod):
        c_ctx, w_ada = res
        dm = dmod.reshape(L * 2, 6 * D)
        gathered = allgather_small(jnp.pad(dm, ((0, (-2 * L) % 8), (0, 0))), "ada_grad_gather")
        gathered = gathered.reshape(N_DEV, -1, 6 * D)[:, :2 * L].reshape(N_DEV, L, 2, 6 * D)
        dm_full = gathered.transpose(1, 2, 0, 3).reshape(L, 16, 6 * D)
        dm_mine = lax.dynamic_slice_in_dim(dm_full, me * S, S, axis=2)
        gw, dc = run_bwd(stacked(c_ctx), w_ada, dm_mine)
        d_cctx = jnp.sum(dc[N_DEV:], axis=0, keepdims=True)
        return d_cctx, gw, bias_grad(dm_full)

    op.defvjp(fwd, bwd)
    return op(c_ctx, w_ada, b_ada)


def adamw(w, g, m, v, name):
    shape = w.shape
    C = shape[-1]
    R = w.size // C
    tr = _pick(R, (256, 128, 64, 32, 16, 8)) if R * C * 4 > (1 << 20) else R
    c1 = 1.0 / (1.0 - ADAM_B1 ** ADAM_STEP)
    c2 = 1.0 / (1.0 - ADAM_B2 ** ADAM_STEP)

    def body(w_ref, g_ref, m_ref, v_ref, d_ref, mo_ref, vo_ref):
        gv = g_ref[...]
        mn = ADAM_B1 * m_ref[...] + (1.0 - ADAM_B1) * gv
        vn = ADAM_B2 * v_ref[...] + (1.0 - ADAM_B2) * gv * gv
        d_ref[...] = -ADAM_LR * ((mn * c1) / (jnp.sqrt(vn * c2) + ADAM_EPS) + ADAM_WD * w_ref[...])
        mo_ref[...] = mn
        vo_ref[...] = vn

    spec = pl.BlockSpec((tr, C), lambda i: (i, 0))
    res = pl.pallas_call(
        body, name=name, grid=(R // tr,), in_specs=[spec] * 4, out_specs=[spec] * 3,
        out_shape=[jax.ShapeDtypeStruct((R, C), f32)] * 3,
        compiler_params=pltpu.CompilerParams(dimension_semantics=("parallel",)),
    )(*[a.reshape(R, C) for a in (w, g, m, v)])
    return tuple(r.reshape(shape) for r in res)


def _rope_tables(T, L):
    rows = T // GRID_W
    row = jnp.repeat(jnp.arange(rows, dtype=f32), GRID_W)
    col = jnp.tile(jnp.arange(GRID_W, dtype=f32), rows)
    n_freq = HEAD // 4
    inv = ROPE_THETA ** (-jnp.arange(n_freq, dtype=f32) / n_freq)
    ang = jnp.concatenate([row[:, None] * inv, col[:, None] * inv], axis=-1)
    cos = jnp.repeat(jnp.cos(ang), 2, axis=-1)
    sin = jnp.repeat(jnp.sin(ang), 2, axis=-1) * jnp.tile(jnp.array([-1.0, 1.0], f32), HEAD // 2)
    return (jnp.concatenate([cos, jnp.ones((L, HEAD), f32)]), jnp.concatenate([sin, jnp.zeros((L, HEAD), f32)]))


def _loss_fn(p, x, ctx, c_all, tgt, cosf, sinf):
    T, D = x.shape
    NT = T + ctx.shape[0]
    depth = p["norm1_g"].shape[0]
    w_in, w_out, w_up, w_down = gather_weights((p["w_in"], p["w_out"], p["w_up"], p["w_down"]), ("col", "row", "col", "row"))
    mod = ada_mod(c_all, p["c_ctx"], p["w_ada"], p["b_ada"])
    lbs = lower_bounds([p["hg_lower_bounds"][:, l] for l in range(depth)])
    sel = lambda l, idx: jnp.stack([mod[l, :, i] for i in idx], axis=1)
    row = lambda a, l: a[l][None, :]

    xs = jnp.concatenate([x, ctx], axis=0)
    (h,) = rowwise("modnorm", f_modnorm, [xs], sels=[sel(0, (0, 1))], pars=[row(p["norm1_g"], 0)], outs=[(D, bf16)], n_lat=T)
    for l in range(depth):
        win = mm(h, w_in[l], "w_in%d" % l)
        parts, off = [], 0
        for s in IN_SIZES:
            parts.append(win[:, off:off + s])
            off += s
        aq, ak, av, hq, hff, hfb, hi, hgt, su, sv = parts
        q, k, v = rowwise("qkprep%d" % l, f_qkprep, [aq, ak, av], consts=[cosf, sinf],
                          pars=[row(p["q_norm_g"], l), row(p["k_norm_g"], l)],
                          outs=[(aq.shape[1], bf16), (ak.shape[1], bf16), (av.shape[1], bf16)], n_lat=T)
        attn = attention(q, k, v, T, "attn%d" % l)
        o_f = hgrn(hq, hff, hi, lbs[l][0:1], T, False, "hgrn_f%d" % l)
        o_b = hgrn(hq, hfb, hi, lbs[l][1:2], T, True, "hgrn_b%d" % l)
        (hg,) = rowwise("hgout%d" % l, f_hgout, [o_f, o_b, hgt], pars=[row(p["hg_norm_g"], l)], outs=[(hgt.shape[1], bf16)], n_lat=T)
        (sg,) = rowwise("sgate%d" % l, f_sgate, [su, sv],
                        pars=[row(p["sg_norm_g"], l), p["sg_w"][l]] + [p["sg_b"][l, gi][:, None] for gi in range(SG_GROUPS)],
                        outs=[(su.shape[1], bf16)], n_lat=T)
        mix = jnp.concatenate([attn, hg, sg], axis=1)
        y = mm(mix, w_out[l], "w_out%d" % l)
        xs, h2 = rowwise("resid_a%d" % l, f_resid_modnorm, [xs, y], sels=[sel(l, (2, 3, 4))], pars=[row(p["norm2_g"], l)],
                         outs=[(D, f32), (D, bf16)], n_lat=T)
        up = mm(h2, w_up[l], "w_up%d" % l)
        act = convact(up, p["conv_w"][l], row(p["conv_b"], l), T, "convact%d" % l)
        z = mm(act, w_down[l], "w_down%d" % l)
        if l + 1 < depth:
            xs, h = rowwise("resid_b%d" % l, f_resid_modnorm, [xs, z], sels=[jnp.concatenate([sel(l, (5,)), sel(l + 1, (0, 1))], axis=1)],
                            pars=[row(p["norm1_g"], l + 1)], outs=[(D, f32), (D, bf16)], n_lat=T)
        else:
            (rowloss,) = rowwise("resid_final", f_resid_final, [xs[:T], z[:T]], consts=[tgt], sels=[sel(l, (5,))[0:1]],
                                 pars=[p["final_norm_g"][None, :]], outs=[(1, f32)])
    return 0.5 * jnp.sum(rowloss)


def kernel(x, c, ctx, c_ctx, w_ada, b_ada, norm1_g, w_in, q_norm_g, k_norm_g, hg_lower_bounds, hg_norm_g, sg_norm_g, sg_w, sg_b, w_out, norm2_g, w_up, conv_w, conv_b, w_down, final_norm_g, loss_target, m_c_ctx, m_w_ada, m_b_ada, m_norm1_g, m_w_in, m_q_norm_g, m_k_norm_g, m_hg_lower_bounds, m_hg_norm_g, m_sg_norm_g, m_sg_w, m_sg_b, m_w_out, m_norm2_g, m_w_up, m_conv_w, m_conv_b, m_w_down, m_final_norm_g, v_c_ctx, v_w_ada, v_b_ada, v_norm1_g, v_w_in, v_q_norm_g, v_k_norm_g, v_hg_lower_bounds, v_hg_norm_g, v_sg_norm_g, v_sg_w, v_sg_b, v_w_out, v_norm2_g, v_w_up, v_conv_w, v_conv_b, v_w_down, v_final_norm_g):
    given = dict(c_ctx=c_ctx, w_ada=w_ada, b_ada=b_ada, norm1_g=norm1_g, w_in=w_in, q_norm_g=q_norm_g, k_norm_g=k_norm_g,
                 hg_lower_bounds=hg_lower_bounds, hg_norm_g=hg_norm_g, sg_norm_g=sg_norm_g, sg_w=sg_w, sg_b=sg_b, w_out=w_out,
                 norm2_g=norm2_g, w_up=w_up, conv_w=conv_w, conv_b=conv_b, w_down=w_down, final_norm_g=final_norm_g)
    moments_m = dict(c_ctx=m_c_ctx, w_ada=m_w_ada, b_ada=m_b_ada, norm1_g=m_norm1_g, w_in=m_w_in, q_norm_g=m_q_norm_g,
                     k_norm_g=m_k_norm_g, hg_lower_bounds=m_hg_lower_bounds, hg_norm_g=m_hg_norm_g, sg_norm_g=m_sg_norm_g,
                     sg_w=m_sg_w, sg_b=m_sg_b, w_out=m_w_out, norm2_g=m_norm2_g, w_up=m_w_up, conv_w=m_conv_w, conv_b=m_conv_b,
                     w_down=m_w_down, final_norm_g=m_final_norm_g)
    moments_v = dict(c_ctx=v_c_ctx, w_ada=v_w_ada, b_ada=v_b_ada, norm1_g=v_norm1_g, w_in=v_w_in, q_norm_g=v_q_norm_g,
                     k_norm_g=v_k_norm_g, hg_lower_bounds=v_hg_lower_bounds, hg_norm_g=v_hg_norm_g, sg_norm_g=v_sg_norm_g,
                     sg_w=v_sg_w, sg_b=v_sg_b, w_out=v_w_out, norm2_g=v_norm2_g, w_up=v_w_up, conv_w=v_conv_w, conv_b=v_conv_b,
                     w_down=v_w_down, final_norm_g=v_final_norm_g)
    T, D = x.shape[1], x.shape[2]
    L = ctx.shape[1]
    me = _my_index()
    axes = ("x", "y", "c")

    c_all = allgather_small(jnp.pad(c, ((0, 7), (0, 0))), "gather_c").reshape(N_DEV, 8, D)[:, 0]
    depth, hw = hg_lower_bounds.shape[1], hg_lower_bounds.shape[2]
    cw = conv_w.shape[2]
    small = jnp.concatenate([jnp.pad(hg_lower_bounds.reshape(2 * depth, hw), ((0, 0), (0, cw - hw))), conv_w.reshape(3 * depth, cw)], axis=0)
    rows_small = small.shape[0]
    small = allgather_small(jnp.pad(small, ((0, (-rows_small) % 8), (0, 0))), "gather_small").reshape(N_DEV, -1, cw)
    hg_full = small[:, :2 * depth, :hw].reshape(N_DEV, 2, depth, hw).transpose(1, 2, 0, 3).reshape(2, depth, N_DEV * hw)
    cw_full = small[:, 2 * depth:2 * depth + 3 * depth].reshape(N_DEV, depth, 3, cw).transpose(1, 2, 0, 3).reshape(depth, 3, N_DEV * cw)

    p = dict(given, hg_lower_bounds=hg_full, conv_w=cw_full, c_ctx=c_ctx[None, :])
    cosf, sinf = _rope_tables(T, L)
    loss, (gp, gx) = jax.value_and_grad(_loss_fn, argnums=(0, 1))(p, x[0], ctx[0], c_all, loss_target[0], cosf, sinf)
    loss = lax.psum(loss, axes)

    partial = ['c_ctx', 'norm1_g', 'q_norm_g', 'k_norm_g', 'hg_lower_bounds', 'hg_norm_g', 'sg_norm_g', 'sg_w', 'sg_b',
               'norm2_g', 'conv_w', 'conv_b', 'final_norm_g']
    summed = dict(zip(partial, allreduce_small([gp[n] for n in partial], "reduce_small")))
    grads = dict(gp, **summed)
    grads['c_ctx'] = grads['c_ctx'][0]
    grads['hg_lower_bounds'] = lax.dynamic_slice_in_dim(grads['hg_lower_bounds'], me * hw, hw, axis=2)
    grads['conv_w'] = lax.dynamic_slice_in_dim(grads['conv_w'], me * cw, cw, axis=2)

    delta, new_m, new_v = {}, {}, {}
    for n in WEIGHTS:
        delta[n], new_m[n], new_v[n] = adamw(given[n], grads[n], moments_m[n], moments_v[n], "adamw_" + n)
    return (loss, gx[None], *[grads[n] for n in WEIGHTS], *[delta[n] for n in WEIGHTS],
            *[new_m[n] for n in WEIGHTS], *[new_v[n] for n in WEIGHTS])
```

```python
import functools

import jax
import jax.numpy as jnp
from jax import lax
from jax.experimental import pallas as pl
from jax.experimental.pallas import tpu as pltpu

f32 = jnp.float32
bf16 = jnp.bfloat16
HI = lax.Precision.HIGHEST
MESH = pl.DeviceIdType.MESH

EPS = 1e-6
F_MIN = 1e-30
GRID_W = 64
ROPE_THETA = 10000.0
HEAD = 128
ATTN_HEADS, ATTN_KV = 8, 2
ATTN_GROUP = ATTN_HEADS // ATTN_KV
HG_HEADS = 4
SG_GROUPS = 4
SG_CHUNK = 128
HG_CHUNK = 16
HG_GROUP = 16
IN_SIZES = (1024, 256, 256, 512, 512, 512, 512, 512, 512, 512)
N_DEV = 8
ROW_BLOCK = 256
MAX_TK = 2816
ADAM_LR, ADAM_B1, ADAM_B2, ADAM_EPS, ADAM_WD, ADAM_STEP = 0.001, 0.9, 0.999, 1e-08, 0.01, 10

WEIGHTS = ['c_ctx', 'w_ada', 'b_ada', 'norm1_g', 'w_in', 'q_norm_g', 'k_norm_g', 'hg_lower_bounds', 'hg_norm_g',
           'sg_norm_g', 'sg_w', 'sg_b', 'w_out', 'norm2_g', 'w_up', 'conv_w', 'conv_b', 'w_down', 'final_norm_g']


def _pick(dim, cands):
    for t in cands:
        if dim % t == 0:
            return t
    return dim


def _my_index():
    return 4 * lax.axis_index("x") + 2 * lax.axis_index("y") + lax.axis_index("c")


def _mm_call(a, b, mode, out_dtype, name):
    if mode == "nn":
        (M, K), N = a.shape, b.shape[1]
    elif mode == "nt":
        (M, K), N = a.shape, b.shape[0]
    else:
        (K, M), N = a.shape, b.shape[1]
    tm = _pick(M, (1088, 1024, 512, 256, 128))
    tn = _pick(N, (1024, 512, 256, 128))
    tk = K if K <= MAX_TK else _pick(K, (2816, 2560, 2176, 2048, 1408, 1088, 1024, 512, 256, 128))
    nk = K // tk
    dims = {"nn": (((1,), (0,)), ((), ())), "nt": (((1,), (1,)), ((), ())), "tn": (((0,), (0,)), ((), ()))}[mode]

    def body(a_ref, b_ref, o_ref, *acc):
        prod = lax.dot_general(a_ref[...].astype(bf16), b_ref[...].astype(bf16), dims, preferred_element_type=f32)
        if nk == 1:
            o_ref[...] = prod.astype(o_ref.dtype)
            return
        k = pl.program_id(2)

        @pl.when(k == 0)
        def _():
            acc[0][...] = prod

        @pl.when((k > 0) & (k < nk - 1))
        def _():
            acc[0][...] += prod

        @pl.when(k == nk - 1)
        def _():
            o_ref[...] = (acc[0][...] + prod).astype(o_ref.dtype)

    a_spec = pl.BlockSpec((tk, tm), lambda i, j, k: (k, i)) if mode == "tn" else pl.BlockSpec((tm, tk), lambda i, j, k: (i, k))
    b_spec = pl.BlockSpec((tn, tk), lambda i, j, k: (j, k)) if mode == "nt" else pl.BlockSpec((tk, tn), lambda i, j, k: (k, j))
    return pl.pallas_call(
        body, name=name, grid=(M // tm, N // tn, nk),
        in_specs=[a_spec, b_spec], out_specs=pl.BlockSpec((tm, tn), lambda i, j, k: (i, j)),
        out_shape=jax.ShapeDtypeStruct((M, N), out_dtype),
        scratch_shapes=[pltpu.VMEM((tm, tn), f32)] if nk > 1 else [],
        compiler_params=pltpu.CompilerParams(dimension_semantics=("parallel", "parallel", "arbitrary")),
    )(a, b)


def mm(a, w, name):
    @jax.custom_vjp
    def op(a, w):
        return _mm_call(a, w, "nn", f32, name + "_fwd")

    def fwd(a, w):
        return op(a, w), (a, w)

    def bwd(res, dy):
        a, w = res
        dy = dy.astype(bf16)
        da = _mm_call(dy, w, "nt", a.dtype, name + "_bwd_a")
        dw = _mm_call(a, dy, "tn", w.dtype, name + "_bwd_w")
        return da, dw

    op.defvjp(fwd, bwd)
    return op(a, w)


def _rowwise_specs(rows, consts, sels, pars, tb, nlb):
    specs = [pl.BlockSpec((tb, a.shape[1]), lambda i: (i, 0)) for a in (*rows, *consts)]
    specs += [pl.BlockSpec((None,) + a.shape[1:], lambda i: (jnp.where(i >= nlb, 1, 0), 0, 0)) for a in sels]
    specs += [pl.BlockSpec(a.shape, functools.partial(lambda i, n: (0,) * n, n=a.ndim)) for a in pars]
    return specs


def rowwise(name, f, rows, consts=(), sels=(), pars=(), outs=(), n_lat=None, tb=ROW_BLOCK):
    rows, consts, sels, pars = tuple(rows), tuple(consts), tuple(sels), tuple(pars)
    R = rows[0].shape[0]
    nb = R // tb
    nlb = nb if n_lat is None else n_lat // tb
    n_in = len(rows) + len(consts) + len(sels) + len(pars)
    n_out = len(outs)
    out_dtypes = [d for _, d in outs]
    out_specs = [pl.BlockSpec((tb, w), lambda i: (i, 0)) for w, _ in outs]
    out_shape = [jax.ShapeDtypeStruct((R, w), d) for w, d in outs]
    sem = pltpu.CompilerParams(dimension_semantics=("arbitrary",))

    def run_fwd(rows, consts, sels, pars):
        def body(*refs):
            res = f(*[r[...] for r in refs[:n_in]])
            for o_ref, r in zip(refs[n_in:], res):
                o_ref[...] = r.astype(o_ref.dtype)

        return tuple(pl.pallas_call(
            body, name=name + "_fwd", grid=(nb,), in_specs=_rowwise_specs(rows, consts, sels, pars, tb, nlb),
            out_specs=out_specs, out_shape=out_shape, compiler_params=sem,
        )(*rows, *consts, *sels, *pars))

    def run_bwd(rows, consts, sels, pars, cts):
        nr, nc, ns, npar = len(rows), len(consts), len(sels), len(pars)

        def body(*refs):
            i = pl.program_id(0)
            ins = [r[...] for r in refs[:n_in]]
            ct = tuple(r[...] for r in refs[n_in:n_in + n_out])
            o_refs = refs[n_in + n_out:]
            cvals = ins[nr:nr + nc]

            def g(*d):
                res = f(*d[:nr], *cvals, *d[nr:])
                return tuple(r.astype(t) for r, t in zip(res, out_dtypes))

            _, vjp = jax.vjp(g, *ins[:nr], *ins[nr + nc:])
            grads = vjp(ct)
            for k in range(nr):
                o_refs[k][...] = grads[k].astype(o_refs[k].dtype)
            for k in range(nr, nr + ns + npar):
                first = (i == 0) | (i == nlb) if k < nr + ns else (i == 0)
                gk = grads[k].astype(f32)

                @pl.when(first)
                def _(k=k, gk=gk):
                    o_refs[k][...] = gk

                @pl.when(jnp.logical_not(first))
                def _(k=k, gk=gk):
                    o_refs[k][...] += gk

        in_specs = _rowwise_specs(rows, consts, sels, pars, tb, nlb) + out_specs
        o_specs = [pl.BlockSpec((tb, a.shape[1]), lambda i: (i, 0)) for a in rows]
        o_specs += [pl.BlockSpec((None,) + a.shape[1:], lambda i: (jnp.where(i >= nlb, 1, 0), 0, 0)) for a in sels]
        o_specs += [pl.BlockSpec(a.shape, functools.partial(lambda i, n: (0,) * n, n=a.ndim)) for a in pars]
        o_shape = [jax.ShapeDtypeStruct(a.shape, a.dtype) for a in rows]
        o_shape += [jax.ShapeDtypeStruct(a.shape, f32) for a in (*sels, *pars)]
        res = pl.pallas_call(
            body, name=name + "_bwd", grid=(nb,), in_specs=in_specs, out_specs=o_specs, out_shape=o_shape,
            compiler_params=sem,
        )(*rows, *consts, *sels, *pars, *cts)
        return tuple(res[:nr]), tuple(res[nr:nr + ns]), tuple(res[nr + ns:])

    @jax.custom_vjp
    def op(rows, consts, sels, pars):
        return run_fwd(rows, consts, sels, pars)

    def fwd(rows, consts, sels, pars):
        return op(rows, consts, sels, pars), (rows, consts, sels, pars)

    def bwd(res, cts):
        rows, consts, sels, pars = res
        drows, dsels, dpars = run_bwd(rows, consts, sels, pars, tuple(cts))
        return drows, tuple(jnp.zeros_like(c) for c in consts), dsels, dpars

    op.defvjp(fwd, bwd)
    return op(rows, consts, sels, pars)


def _rms(x, g):
    return x * lax.rsqrt(jnp.mean(x * x, axis=-1, keepdims=True) + EPS) * g


def _silu(x):
    return x * jax.nn.sigmoid(x)


def f_modnorm(x, mods, g):
    return (_rms(x, g) * (1.0 + mods[1:2]) + mods[0:1],)


def f_resid_modnorm(x, y, mods, g):
    xn = x + mods[0:1] * y
    return xn, _rms(xn, g) * (1.0 + mods[2:3]) + mods[1:2]


def f_resid_final(x, y, tgt, mods, g):
    xn = x + mods[0:1] * y
    err = _rms(xn, g) - tgt
    return (jnp.mean(err * err, axis=-1, keepdims=True),)


def f_qkprep(aq, ak, av, cosf, sinf, qg, kg):
    r = lax.broadcasted_iota(jnp.int32, (HEAD, HEAD), 0)
    c = lax.broadcasted_iota(jnp.int32, (HEAD, HEAD), 1)
    swap = jnp.where((r ^ 1) == c, 1.0, 0.0).astype(f32)

    def head(xh, g):
        y = _rms(xh, g)
        ys = jnp.dot(y, swap, precision=HI, preferred_element_type=f32)
        return y * cosf + ys * sinf

    q = jnp.concatenate([head(aq[:, h * HEAD:(h + 1) * HEAD], qg) for h in range(ATTN_HEADS)], axis=1)
    k = jnp.concatenate([head(ak[:, h * HEAD:(h + 1) * HEAD], kg) for h in range(ATTN_KV)], axis=1)
    return q, k, av


def f_hgout(of, ob, gt, g):
    o = of + ob
    y = jnp.concatenate([_rms(o[:, h * HEAD:(h + 1) * HEAD], g) for h in range(HG_HEADS)], axis=1)
    return (y * _silu(gt),)


def f_sgate(u, v, g, w, b0, b1, b2, b3):
    u = jax.nn.gelu(u)
    v = jax.nn.gelu(v)
    bs = (b0, b1, b2, b3)
    cols = []
    for gi in range(SG_GROUPS):
        sl = slice(gi * HEAD, (gi + 1) * HEAD)
        vg = _rms(v[:, sl], g[:, sl])
        parts = []
        for n in range(v.shape[0] // SG_CHUNK):
            vc = vg[n * SG_CHUNK:(n + 1) * SG_CHUNK]
            parts.append(jnp.dot(w[gi].astype(bf16), vc.astype(bf16), preferred_element_type=f32) + bs[gi])
        cols.append(jnp.concatenate(parts, axis=0))
    return (u * jnp.concatenate(cols, axis=1),)


def attention(q, k, v, T, name, tq=ROW_BLOCK):
    NT = q.shape[0]
    nqb, nlb = NT // tq, T // tq
    scale = HEAD ** -0.5
    q_spec = pl.BlockSpec((tq, HEAD), lambda kv, g, i: (i, kv * ATTN_GROUP + g))
    kv_spec = pl.BlockSpec((NT, HEAD), lambda kv, g, i: (0, kv))
    lse_spec = pl.BlockSpec((None, tq, 1), lambda kv, g, i: (kv * ATTN_GROUP + g, i, 0))
    grid = (ATTN_KV, ATTN_GROUP, nqb)
    nt_dims = (((1,), (1,)), ((), ()))
    tn_dims = (((0,), (0,)), ((), ()))

    def on_keys(i, fn):
        @pl.when(i < nlb)
        def _():
            fn(pl.ds(0, NT))

        @pl.when(i >= nlb)
        def _():
            fn(pl.ds(T, NT - T))

    def run_fwd(q, k, v):
        def body(q_ref, k_ref, v_ref, o_ref, lse_ref):
            def run(rows):
                s = lax.dot_general(q_ref[...], k_ref[rows, :], nt_dims, preferred_element_type=f32)
                m = jnp.max(s, axis=-1, keepdims=True) * scale
                p = jnp.exp(s * scale - m)
                l = jnp.sum(p, axis=-1, keepdims=True)
                o = jnp.dot(p.astype(bf16), v_ref[rows, :], preferred_element_type=f32) / l
                o_ref[...] = o.astype(o_ref.dtype)
                lse_ref[...] = m + jnp.log(l)

            on_keys(pl.program_id(2), run)

        return pl.pallas_call(
            body, name=name + "_fwd", grid=grid, in_specs=[q_spec, kv_spec, kv_spec], out_specs=[q_spec, lse_spec],
            out_shape=[jax.ShapeDtypeStruct(q.shape, bf16), jax.ShapeDtypeStruct((ATTN_HEADS, NT, 1), f32)],
            compiler_params=pltpu.CompilerParams(dimension_semantics=("parallel", "parallel", "arbitrary")),
        )(q, k, v)

    def run_bwd(q, k, v, lse, do):
        def body(q_ref, k_ref, v_ref, lse_ref, do_ref, dq_ref, dk_ref, dv_ref):
            g, i = pl.program_id(1), pl.program_id(2)

            @pl.when((g == 0) & (i == 0))
            def _():
                dk_ref[...] = jnp.zeros_like(dk_ref)
                dv_ref[...] = jnp.zeros_like(dv_ref)

            def run(rows):
                qb, kb, vb, dob = q_ref[...], k_ref[rows, :], v_ref[rows, :], do_ref[...]
                s = lax.dot_general(qb, kb, nt_dims, preferred_element_type=f32)
                p = jnp.exp(s * scale - lse_ref[...])
                dp = lax.dot_general(dob, vb, nt_dims, preferred_element_type=f32)
                ds = (p * (dp - jnp.sum(p * dp, axis=-1, keepdims=True)) * scale).astype(bf16)
                dq_ref[...] = jnp.dot(ds, kb, preferred_element_type=f32).astype(dq_ref.dtype)
                dk_ref[rows, :] += lax.dot_general(ds, qb, tn_dims, preferred_element_type=f32)
                dv_ref[rows, :] += lax.dot_general(p.astype(bf16), dob, tn_dims, preferred_element_type=f32)

            on_keys(i, run)

        return pl.pallas_call(
            body, name=name + "_bwd", grid=grid, in_specs=[q_spec, kv_spec, kv_spec, lse_spec, q_spec],
            out_specs=[q_spec, kv_spec, kv_spec],
            out_shape=[jax.ShapeDtypeStruct(q.shape, bf16), jax.ShapeDtypeStruct(k.shape, f32), jax.ShapeDtypeStruct(v.shape, f32)],
            compiler_params=pltpu.CompilerParams(dimension_semantics=("parallel", "arbitrary", "arbitrary")),
        )(q, k, v, lse, do)

    @jax.custom_vjp
    def op(q, k, v):
        return run_fwd(q, k, v)[0]

    def fwd(q, k, v):
        o, lse = run_fwd(q, k, v)
        return o, (q, k, v, lse)

    def bwd(res, do):
        dq, dk, dv = run_bwd(*res, do)
        return dq, dk.astype(bf16), dv.astype(bf16)

    op.defvjp(fwd, bwd)
    return op(q, k, v)


def _bdot(a, b, ca, cb):
    fa, fb = 3 - ca, 3 - cb

    def dot(x, y, cx, cy):
        return lax.dot_general(x.astype(bf16), y.astype(bf16), (((cx,), (cy,)), ((0,), (0,))), preferred_element_type=f32)

    @jax.custom_vjp
    def op(a, b):
        return dot(a, b, ca, cb)

    def fwd(a, b):
        return op(a, b), (a, b)

    def bwd(res, ct):
        a, b = res
        da = dot(ct, b, 2, fb) if ca == 2 else dot(b, ct, fb, 2)
        db = dot(a, ct, fa, 1) if cb == 1 else dot(ct, a, 1, fa)
        return da, db

    op.defvjp(fwd, bwd)
    return op(a, b)


def _chunk_cumsum(x, rev):
    def impl(x, rev):
        n = x.shape[0]
        pos = lax.broadcasted_iota(jnp.int32, x.shape, 0) % HG_CHUNK
        s = 1
        while s < HG_CHUNK:
            if rev:
                x = x + jnp.where(pos < HG_CHUNK - s, pltpu.roll(x, n - s, 0), 0.0)
            else:
                x = x + jnp.where(pos >= s, pltpu.roll(x, s, 0), 0.0)
            s *= 2
        return x

    @jax.custom_vjp
    def op(x):
        return impl(x, rev)

    op.defvjp(lambda x: (op(x), None), lambda _, ct: (impl(ct, not rev),))
    return op(x)


def _hg_group(St, hq, hf, hi, lb, *, rev):
    G, C = HG_GROUP, HG_CHUNK
    R = G * C
    q = _silu(hq)
    f = lb + (1.0 - lb) * jax.nn.sigmoid(hf)
    logf = jnp.log(jnp.maximum(f, F_MIN))
    kk = (1.0 - lb) * jax.nn.sigmoid(-hf)
    b3 = _chunk_cumsum(logf, rev).reshape(G, C, HEAD)
    q3, k3, v3 = q.reshape(G, C, HEAD), kk.reshape(G, C, HEAD), hi.reshape(G, C, HEAD)
    btot = jnp.sum(logf.reshape(G, C, HEAD), axis=1)
    tt = lax.broadcasted_iota(jnp.int32, (G, C, C, HEAD), 1)
    ss = lax.broadcasted_iota(jnp.int32, (G, C, C, HEAD), 2)
    mask = (ss >= tt) if rev else (ss <= tt)
    diff = b3[:, :, None, :] - b3[:, None, :, :]
    dec = jnp.where(mask, jnp.exp(jnp.where(mask, diff, 0.0)), 0.0)
    scores = jnp.sum(q3[:, :, None, :] * k3[:, None, :, :] * dec, axis=-1)
    o_intra = _bdot(scores, v3, 2, 1)
    q_dec = q3 * jnp.exp(b3)
    k_dec = k3 * jnp.exp(btot[:, None, :] - b3)
    kvt = _bdot(v3, k_dec, 1, 1)
    dl = jnp.exp(btot)
    states = [None] * G
    for g in (range(G - 1, -1, -1) if rev else range(G)):
        states[g] = St
        St = St * dl[g:g + 1, :] + kvt[g]
    o_inter = _bdot(q_dec, jnp.stack(states), 2, 2)
    return St, (o_intra + o_inter).reshape(R, HEAD)


def hgrn(hq, hf, hi, lb, T, rev, name):
    NT, W = hq.shape
    R = HG_GROUP * HG_CHUNK
    n_lat, n_ctx = T // R, (NT - T) // R
    nG = n_lat + n_ctx

    def group_of(j):
        if rev:
            return jnp.where(j < n_ctx, nG - 1 - j, n_lat - 1 - (j - n_ctx))
        return jnp.where(j < n_ctx, n_lat + j, j - n_ctx)

    def rows_of(j):
        return pl.ds(pl.multiple_of(group_of(j) * R, R), R)

    col_spec = pl.BlockSpec((NT, HEAD), lambda h: (0, h))
    lb_spec = pl.BlockSpec((1, HEAD), lambda h: (0, h))
    st_spec = pl.BlockSpec((None, nG, HEAD, HEAD), lambda h: (h, 0, 0, 0))
    sem = pltpu.CompilerParams(dimension_semantics=("parallel",))

    def run_fwd(hq, hf, hi, lb):
        def body(hq_ref, hf_ref, hi_ref, lb_ref, o_ref, st_ref):
            def step(j, St):
                st_ref[j] = St
                rows = rows_of(j)
                St, o = _hg_group(St, hq_ref[rows, :], hf_ref[rows, :], hi_ref[rows, :], lb_ref[...], rev=rev)
                o_ref[rows, :] = o
                return St

            lax.fori_loop(0, nG, step, jnp.zeros((HEAD, HEAD), f32))

        return pl.pallas_call(
            body, name=name + "_fwd", grid=(W // HEAD,), in_specs=[col_spec, col_spec, col_spec, lb_spec],
            out_specs=[col_spec, st_spec],
            out_shape=[jax.ShapeDtypeStruct((NT, W), f32), jax.ShapeDtypeStruct((W // HEAD, nG, HEAD, HEAD), f32)],
            compiler_params=sem,
        )(hq, hf, hi, lb)

    def run_bwd(hq, hf, hi, lb, st, do):
        def body(hq_ref, hf_ref, hi_ref, lb_ref, st_ref, do_ref, dq_ref, df_ref, di_ref, dlb_ref):
            def step(jj, carry):
                dS, dlb = carry
                j = nG - 1 - jj
                rows = rows_of(j)
                _, vjp = jax.vjp(functools.partial(_hg_group, rev=rev), st_ref[j], hq_ref[rows, :], hf_ref[rows, :],
                                 hi_ref[rows, :], lb_ref[...])
                dS, dq, df, di, dl = vjp((dS, do_ref[rows, :]))
                dq_ref[rows, :] = dq
                df_ref[rows, :] = df
                di_ref[rows, :] = di
                return dS, dlb + dl

            _, dlb = lax.fori_loop(0, nG, step, (jnp.zeros((HEAD, HEAD), f32), jnp.zeros((1, HEAD), f32)))
            dlb_ref[...] = dlb

        return pl.pallas_call(
            body, name=name + "_bwd", grid=(W // HEAD,),
            in_specs=[col_spec, col_spec, col_spec, lb_spec, st_spec, col_spec],
            out_specs=[col_spec, col_spec, col_spec, lb_spec],
            out_shape=[jax.ShapeDtypeStruct((NT, W), f32)] * 3 + [jax.ShapeDtypeStruct((1, W), f32)],
            compiler_params=sem,
        )(hq, hf, hi, lb, st, do)

    @jax.custom_vjp
    def op(hq, hf, hi, lb):
        return run_fwd(hq, hf, hi, lb)[0]

    def fwd(hq, hf, hi, lb):
        o, st = run_fwd(hq, hf, hi, lb)
        return o, (hq, hf, hi, lb, st)

    def bwd(res, do):
        return tuple(run_bwd(*res, do))

    op.defvjp(fwd, bwd)
    return op(hq, hf, hi, lb)


def lower_bounds(params):
    n = len(params)

    def f(*a):
        m = functools.reduce(jnp.maximum, a)
        e = [jnp.exp(x - m) for x in a]
        s = functools.reduce(lambda u, v: u + v, e)
        p = [x / s for x in e]
        out, run = [], jnp.zeros_like(p[0])
        for l in range(n):
            out.append(run)
            run = run + p[l]
        return tuple(out[l] + p[l] - p[0] for l in range(n))

    shape = [jax.ShapeDtypeStruct(params[0].shape, f32)] * n

    @jax.custom_vjp
    def op(*a):
        def body(*refs):
            for o_ref, r in zip(refs[n:], f(*[x[...] for x in refs[:n]])):
                o_ref[...] = r
        return tuple(pl.pallas_call(body, name="lower_bounds_fwd", out_shape=shape)(*a))

    def fwd(*a):
        return op(*a), a

    def bwd(a, cts):
        def body(*refs):
            _, vjp = jax.vjp(f, *[x[...] for x in refs[:n]])
            for o_ref, r in zip(refs[2 * n:], vjp(tuple(x[...] for x in refs[n:2 * n]))):
                o_ref[...] = r
        return tuple(pl.pallas_call(body, name="lower_bounds_bwd", out_shape=shape)(*a, *cts))

    op.defvjp(fwd, bwd)
    return op(*params)


def _shift_rows(x, d, T):
    n = x.shape[0]
    t = lax.broadcasted_iota(jnp.int32, x.shape, 0)
    y = pltpu.roll(x, d % n, 0)
    edge = ((t == 0) | (t == T)) if d == 1 else ((t == T - 1) | (t == n - 1))
    return jnp.where(edge, 0.0, y)


def _conv(x, w, b, T):
    return b + w[0:1] * _shift_rows(x, 1, T) + w[1:2] * x + w[2:3] * _shift_rows(x, -1, T)


def convact(up, cw, cb, T, name, tc=128):
    NT, F2 = up.shape
    F = F2 // 2
    tc = _pick(F, (tc, 128))
    nf = F // tc
    g_spec = lambda r: pl.BlockSpec((r, tc), lambda j: (0, j))
    v_spec = lambda r: pl.BlockSpec((r, tc), lambda j: (0, j + nf))
    sem = pltpu.CompilerParams(dimension_semantics=("parallel",))

    def run_fwd(up, cw, cb):
        def body(xg_ref, xv_ref, wg_ref, wv_ref, bg_ref, bv_ref, o_ref):
            yg = _conv(xg_ref[...], wg_ref[...], bg_ref[...], T)
            yv = _conv(xv_ref[...], wv_ref[...], bv_ref[...], T)
            o_ref[...] = (_silu(yg) * yv).astype(o_ref.dtype)

        return pl.pallas_call(
            body, name=name + "_fwd", grid=(nf,),
            in_specs=[g_spec(NT), v_spec(NT), g_spec(3), v_spec(3), g_spec(1), v_spec(1)], out_specs=g_spec(NT),
            out_shape=jax.ShapeDtypeStruct((NT, F), bf16), compiler_params=sem,
        )(up, up, cw, cw, cb, cb)

    def run_bwd(up, cw, cb, dact):
        def body(xg_ref, xv_ref, wg_ref, wv_ref, bg_ref, bv_ref, da_ref, dxg_ref, dxv_ref, dwg_ref, dwv_ref, dbg_ref, dbv_ref):
            xg, xv, wg, wv = xg_ref[...], xv_ref[...], wg_ref[...], wv_ref[...]
            yg = _conv(xg, wg, bg_ref[...], T)
            yv = _conv(xv, wv, bv_ref[...], T)
            da = da_ref[...].astype(f32)
            sg = jax.nn.sigmoid(yg)
            dyv = da * yg * sg
            dyg = da * yv * sg * (1.0 + yg * (1.0 - sg))
            for x, w, dy, dx_ref, dw_ref, db_ref in ((xg, wg, dyg, dxg_ref, dwg_ref, dbg_ref), (xv, wv, dyv, dxv_ref, dwv_ref, dbv_ref)):
                dx_ref[...] = w[0:1] * _shift_rows(dy, -1, T) + w[1:2] * dy + w[2:3] * _shift_rows(dy, 1, T)
                dw_ref[...] = jnp.concatenate([
                    jnp.sum(dy * _shift_rows(x, 1, T), axis=0, keepdims=True),
                    jnp.sum(dy * x, axis=0, keepdims=True),
                    jnp.sum(dy * _shift_rows(x, -1, T), axis=0, keepdims=True)], axis=0)
                db_ref[...] = jnp.sum(dy, axis=0, keepdims=True)

        return pl.pallas_call(
            body, name=name + "_bwd", grid=(nf,),
            in_specs=[g_spec(NT), v_spec(NT), g_spec(3), v_spec(3), g_spec(1), v_spec(1), g_spec(NT)],
            out_specs=[g_spec(NT), g_spec(NT), g_spec(3), g_spec(3), g_spec(1), g_spec(1)],
            out_shape=[jax.ShapeDtypeStruct((NT, F), f32)] * 2 + [jax.ShapeDtypeStruct((3, F), f32)] * 2 + [jax.ShapeDtypeStruct((1, F), f32)] * 2,
            compiler_params=sem,
        )(up, up, cw, cw, cb, cb, dact)

    @jax.custom_vjp
    def op(up, cw, cb):
        return run_fwd(up, cw, cb)

    def fwd(up, cw, cb):
        return op(up, cw, cb), (up, cw, cb)

    def bwd(res, dact):
        dxg, dxv, dwg, dwv, dbg, dbv = run_bwd(*res, dact)
        return (jnp.concatenate([dxg, dxv], axis=1), jnp.concatenate([dwg, dwv], axis=1), jnp.concatenate([dbg, dbv], axis=1))

    op.defvjp(fwd, bwd)
    return op(up, cw, cb)


def _peers():
    x, y, c = lax.axis_index("x"), lax.axis_index("y"), lax.axis_index("c")
    return (x, y, c), [(x, y, 1 - c), (1 - x, y, c), (x, 1 - y, c), (1 - x, 1 - y, c),
                       (1 - x, y, 1 - c), (x, 1 - y, 1 - c), (1 - x, 1 - y, 1 - c)]


def _index(dev):
    return 4 * dev[0] + 2 * dev[1] + dev[2]


def allgather_small(x, name):
    m, n = x.shape

    def body(x_ref, out_ref, send_sems, recv_sems, local_sem):
        me, peers = _peers()

        def rows(dev):
            return out_ref.at[pl.ds(pl.multiple_of(_index(dev) * m, 8), m), :]

        mine = pltpu.make_async_copy(x_ref, rows(me), local_sem)
        mine.start()
        sends = [pltpu.make_async_remote_copy(src_ref=x_ref, dst_ref=rows(me), send_sem=send_sems.at[k], recv_sem=recv_sems.at[k],
                                              device_id=p, device_id_type=MESH) for k, p in enumerate(peers)]
        for cp in sends:
            cp.start()
        for k, p in enumerate(peers):
            pltpu.make_async_remote_copy(src_ref=x_ref, dst_ref=rows(p), send_sem=send_sems.at[k], recv_sem=recv_sems.at[k],
                                         device_id=p, device_id_type=MESH).wait_recv()
        for cp in sends:
            cp.wait_send()
        mine.wait()

    return pl.pallas_call(
        body, name=name, out_shape=jax.ShapeDtypeStruct((N_DEV * m, n), x.dtype),
        in_specs=[pl.BlockSpec(memory_space=pltpu.VMEM)], out_specs=pl.BlockSpec(memory_space=pltpu.VMEM),
        scratch_shapes=[pltpu.SemaphoreType.DMA((7,)), pltpu.SemaphoreType.DMA((7,)), pltpu.SemaphoreType.DMA],
    )(x)


def _shard_of(ref, kind, j, width):
    if kind == "col":
        return ref.at[:, :, pl.ds(pl.multiple_of(j * width, 128), width)]
    return ref.at[:, pl.ds(pl.multiple_of(j * width, 8), width), :]


def allgather_big(shards, kinds, name):
    A = len(shards)
    widths = [s.shape[2] if k == "col" else s.shape[1] for s, k in zip(shards, kinds)]
    full = [jax.ShapeDtypeStruct((s.shape[0], s.shape[1] * (1 if k == "col" else N_DEV), s.shape[2] * (N_DEV if k == "col" else 1)), s.dtype)
            for s, k in zip(shards, kinds)]

    def body(*refs):
        ins, outs = refs[:A], refs[A:2 * A]
        send_sems, recv_sems, local_sems = refs[2 * A:]
        x, y, c = lax.axis_index("x"), lax.axis_index("y"), lax.axis_index("c")
        me, sibling = (x, y, c), (x, y, 1 - c)
        chips = [(1 - x, y), (x, 1 - y), (1 - x, 1 - y)]

        def blk(a, dev):
            return _shard_of(outs[a], kinds[a], _index(dev), widths[a])

        def copy(a, k, block, to, src=None):
            return pltpu.make_async_remote_copy(
                src_ref=blk(a, block) if src is None else src, dst_ref=blk(a, block),
                send_sem=send_sems.at[a, k], recv_sem=recv_sems.at[a, k], device_id=to, device_id_type=MESH)

        mine = [pltpu.make_async_copy(ins[a], blk(a, me), local_sems.at[a]) for a in range(A)]
        for cp in mine:
            cp.start()
        first = []
        for a in range(A):
            first.append(copy(a, 0, me, sibling, src=ins[a]))
            first += [copy(a, 1 + j, me, (*chip, c), src=ins[a]) for j, chip in enumerate(chips)]
        for cp in first:
            cp.start()
        passed = []
        for j, chip in enumerate(chips):
            for a in range(A):
                copy(a, 1 + j, (*chip, c), me).wait_recv()
                cp = copy(a, 4 + j, (*chip, c), sibling)
                cp.start()
                passed.append(cp)
        for a in range(A):
            copy(a, 0, sibling, me).wait_recv()
            for j, chip in enumerate(chips):
                copy(a, 4 + j, (*chip, 1 - c), me).wait_recv()
        for cp in first + passed:
            cp.wait_send()
        for cp in mine:
            cp.wait()

    any_spec = pl.BlockSpec(memory_space=pl.ANY)
    return pl.pallas_call(
        body, name=name, out_shape=full, in_specs=[any_spec] * A, out_specs=[any_spec] * A,
        scratch_shapes=[pltpu.SemaphoreType.DMA((A, 7)), pltpu.SemaphoreType.DMA((A, 7)), pltpu.SemaphoreType.DMA((A,))],
    )(*shards)


def alltoall_big(fulls, kinds, name):
    A = len(fulls)
    widths = [(f.shape[2] if k == "col" else f.shape[1]) // N_DEV for f, k in zip(fulls, kinds)]
    out = [jax.ShapeDtypeStruct((N_DEV, f.shape[0], f.shape[1] // (1 if k == "col" else N_DEV), f.shape[2] // (N_DEV if k == "col" else 1)), f.dtype)
           for f, k in zip(fulls, kinds)]

    def body(*refs):
        ins, outs = refs[:A], refs[A:2 * A]
        send_sems, recv_sems, local_sems = refs[2 * A:]
        me, peers = _peers()
        my = _index(me)
        mine = [pltpu.make_async_copy(_shard_of(ins[a], kinds[a], my, widths[a]), outs[a].at[my], local_sems.at[a]) for a in range(A)]
        for cp in mine:
            cp.start()
        sends = []
        for k, p in enumerate(peers):
            for a in range(A):
                sends.append(pltpu.make_async_remote_copy(
                    src_ref=_shard_of(ins[a], kinds[a], _index(p), widths[a]), dst_ref=outs[a].at[my],
                    send_sem=send_sems.at[a, k], recv_sem=recv_sems.at[a, k], device_id=p, device_id_type=MESH))
        for cp in sends:
            cp.start()
        for k, p in enumerate(peers):
            for a in range(A):
                pltpu.make_async_remote_copy(
                    src_ref=_shard_of(ins[a], kinds[a], my, widths[a]), dst_ref=outs[a].at[_index(p)],
                    send_sem=send_sems.at[a, k], recv_sem=recv_sems.at[a, k], device_id=p, device_id_type=MESH).wait_recv()
        for cp in sends:
            cp.wait_send()
        for cp in mine:
            cp.wait()

    any_spec = pl.BlockSpec(memory_space=pl.ANY)
    return pl.pallas_call(
        body, name=name, out_shape=out, in_specs=[any_spec] * A, out_specs=[any_spec] * A,
        scratch_shapes=[pltpu.SemaphoreType.DMA((A, 7)), pltpu.SemaphoreType.DMA((A, 7)), pltpu.SemaphoreType.DMA((A,))],
    )(*fulls)


def sum_slots(x, name):
    _, R, C = x.shape
    tr = _pick(R, (256, 128, 64, 32, 16, 8))

    def body(x_ref, o_ref):
        acc = x_ref[0].astype(f32)
        for d in range(1, N_DEV):
            acc = acc + x_ref[d].astype(f32)
        o_ref[...] = acc

    return pl.pallas_call(
        body, name=name, grid=(R // tr,), in_specs=[pl.BlockSpec((N_DEV, tr, C), lambda i: (0, i, 0))],
        out_specs=pl.BlockSpec((tr, C), lambda i: (i, 0)), out_shape=jax.ShapeDtypeStruct((R, C), f32),
        compiler_params=pltpu.CompilerParams(dimension_semantics=("parallel",)),
    )(x)


def gather_weights(shards, kinds):
    @jax.custom_vjp
    def op(*shards):
        return tuple(allgather_big([s.astype(bf16) for s in shards], kinds, "gather_weights"))

    def fwd(*shards):
        return op(*shards), None

    def bwd(_, dfull):
        slots = alltoall_big(list(dfull), kinds, "scatter_grads")
        return tuple(sum_slots(s.reshape(N_DEV, -1, s.shape[-1]), "sum_grads_%d" % a).reshape(s.shape[1:]) for a, s in enumerate(slots))

    op.defvjp(fwd, bwd)
    return op(*shards)


def allreduce_small(vals, name):
    flat = jnp.concatenate([v.reshape(-1) for v in vals])
    n = flat.shape[0]
    cols = 1024
    m = -(-n // (cols * 8)) * 8
    packed = jnp.pad(flat, (0, m * cols - n)).reshape(m, cols)
    total = sum_slots(allgather_small(packed, name).reshape(N_DEV, m, cols), name + "_sum").reshape(-1)
    out, off = [], 0
    for v in vals:
        out.append(total[off:off + v.size].reshape(v.shape))
        off += v.size
    return out


def ada_mod(c_all, c_ctx, w_ada, b_ada):
    L, D, S = w_ada.shape
    me = _my_index()

    def stacked(c_ctx):
        return jnp.concatenate([c_all, jnp.broadcast_to(c_ctx, (N_DEV, D))], axis=0)

    ts = _pick(S, (512, 384, 256, 128, 64))
    w_spec = pl.BlockSpec((None, D, ts), lambda l, j: (l, 0, j))
    c_spec = pl.BlockSpec((16, D), lambda l, j: (0, 0))
    p_spec = pl.BlockSpec((None, 16, ts), lambda l, j: (l, 0, j))

    def run_fwd(cin, w_ada):
        def body(c_ref, w_ref, o_ref):
            o_ref[...] = jnp.dot(_silu(c_ref[...]).astype(bf16), w_ref[...].astype(bf16), preferred_element_type=f32)

        return pl.pallas_call(
            body, name="ada_fwd", grid=(L, S // ts), in_specs=[c_spec, w_spec], out_specs=p_spec,
            out_shape=jax.ShapeDtypeStruct((L, 16, S), f32),
            compiler_params=pltpu.CompilerParams(dimension_semantics=("parallel", "parallel")),
        )(cin, w_ada)

    def run_bwd(cin, w_ada, dm):
        def body(c_ref, w_ref, dm_ref, gw_ref, dc_ref):
            first = (pl.program_id(0) == 0) & (pl.program_id(1) == 0)
            cv = c_ref[...]
            sg = jax.nn.sigmoid(cv)
            dmv = dm_ref[...].astype(bf16)
            gw_ref[...] = lax.dot_general((cv * sg).astype(bf16), dmv, (((0,), (0,)), ((), ())), preferred_element_type=f32)
            ds = lax.dot_general(dmv, w_ref[...].astype(bf16), (((1,), (1,)), ((), ())), preferred_element_type=f32)
            dc = ds * sg * (1.0 + cv * (1.0 - sg))

            @pl.when(first)
            def _():
                dc_ref[...] = dc

            @pl.when(jnp.logical_not(first))
            def _():
                dc_ref[...] += dc

        return pl.pallas_call(
            body, name="ada_bwd", grid=(L, S // ts), in_specs=[c_spec, w_spec, p_spec], out_specs=[w_spec, c_spec],
            out_shape=[jax.ShapeDtypeStruct((L, D, S), f32), jax.ShapeDtypeStruct((16, D), f32)],
            compiler_params=pltpu.CompilerParams(dimension_semantics=("arbitrary", "arbitrary")),
        )(cin, w_ada, dm)

    def bias_grad(dm_full):
        def body(x_ref, o_ref):
            o_ref[...] = jnp.sum(x_ref[...], axis=0, keepdims=True)

        return pl.pallas_call(
            body, name="ada_bias_grad", grid=(L,), in_specs=[pl.BlockSpec((None, 16, 6 * D), lambda l: (l, 0, 0))],
            out_specs=pl.BlockSpec((None, 1, 6 * D), lambda l: (l, 0, 0)), out_shape=jax.ShapeDtypeStruct((L, 1, 6 * D), f32),
        )(dm_full).reshape(L, 6 * D)

    @jax.custom_vjp
    def op(c_ctx, w_ada, b_ada):
        prod = run_fwd(stacked(c_ctx), w_ada)
        allp = allgather_small(prod.reshape(L * 16, S), "ada_gather").reshape(N_DEV, L, 16, S)
        allp = allp.transpose(1, 2, 0, 3).reshape(L, 16, N_DEV * S)
        mine = lax.dynamic_index_in_dim(allp, me, axis=1, keepdims=False) + b_ada
        ctx = allp[:, N_DEV] + b_ada
        return jnp.stack([mine, ctx], axis=1).reshape(L, 2, 6, D)

    def fwd(c_ctx, w_ada, b_ada):
        return op(c_ctx, w_ada, b_ada), (c_ctx, w_ada)

    def bwd(res, dmod):
        c_ctx, w_ada = res
        dm = dmod.reshape(L * 2, 6 * D)
        gathered = allgather_small(jnp.pad(dm, ((0, (-2 * L) % 8), (0, 0))), "ada_grad_gather")
        gathered = gathered.reshape(N_DEV, -1, 6 * D)[:, :2 * L].reshape(N_DEV, L, 2, 6 * D)
        dm_full = gathered.transpose(1, 2, 0, 3).reshape(L, 16, 6 * D)
        dm_mine = lax.dynamic_slice_in_dim(dm_full, me * S, S, axis=2)
        gw, dc = run_bwd(stacked(c_ctx), w_ada, dm_mine)
        d_cctx = jnp.sum(dc[N_DEV:], axis=0, keepdims=True)
        return d_cctx, gw, bias_grad(dm_full)

    op.defvjp(fwd, bwd)
    return op(c_ctx, w_ada, b_ada)


def adamw(w, g, m, v, name):
    shape = w.shape
    C = shape[-1]
    R = w.size // C
    tr = _pick(R, (256, 128, 64, 32, 16, 8)) if R * C * 4 > (1 << 20) else R
    c1 = 1.0 / (1.0 - ADAM_B1 ** ADAM_STEP)
    c2 = 1.0 / (1.0 - ADAM_B2 ** ADAM_STEP)

    def body(w_ref, g_ref, m_ref, v_ref, d_ref, mo_ref, vo_ref):
        gv = g_ref[...]
        mn = ADAM_B1 * m_ref[...] + (1.0 - ADAM_B1) * gv
        vn = ADAM_B2 * v_ref[...] + (1.0 - ADAM_B2) * gv * gv
        d_ref[...] = -ADAM_LR * ((mn * c1) / (jnp.sqrt(vn * c2) + ADAM_EPS) + ADAM_WD * w_ref[...])
        mo_ref[...] = mn
        vo_ref[...] = vn

    spec = pl.BlockSpec((tr, C), lambda i: (i, 0))
    res = pl.pallas_call(
        body, name=name, grid=(R // tr,), in_specs=[spec] * 4, out_specs=[spec] * 3,
        out_shape=[jax.ShapeDtypeStruct((R, C), f32)] * 3,
        compiler_params=pltpu.CompilerParams(dimension_semantics=("parallel",)),
    )(*[a.reshape(R, C) for a in (w, g, m, v)])
    return tuple(r.reshape(shape) for r in res)


def _rope_tables(T, L):
    rows = T // GRID_W
    row = jnp.repeat(jnp.arange(rows, dtype=f32), GRID_W)
    col = jnp.tile(jnp.arange(GRID_W, dtype=f32), rows)
    n_freq = HEAD // 4
    inv = ROPE_THETA ** (-jnp.arange(n_freq, dtype=f32) / n_freq)
    ang = jnp.concatenate([row[:, None] * inv, col[:, None] * inv], axis=-1)
    cos = jnp.repeat(jnp.cos(ang), 2, axis=-1)
    sin = jnp.repeat(jnp.sin(ang), 2, axis=-1) * jnp.tile(jnp.array([-1.0, 1.0], f32), HEAD // 2)
    return (jnp.concatenate([cos, jnp.ones((L, HEAD), f32)]), jnp.concatenate([sin, jnp.zeros((L, HEAD), f32)]))


def _loss_fn(p, x, ctx, c_all, tgt, cosf, sinf):
    T, D = x.shape
    NT = T + ctx.shape[0]
    depth = p["norm1_g"].shape[0]
    w_in, w_out, w_up, w_down = gather_weights((p["w_in"], p["w_out"], p["w_up"], p["w_down"]), ("col", "row", "col", "row"))
    mod = ada_mod(c_all, p["c_ctx"], p["w_ada"], p["b_ada"])
    lbs = lower_bounds([p["hg_lower_bounds"][:, l] for l in range(depth)])
    sel = lambda l, idx: jnp.stack([mod[l, :, i] for i in idx], axis=1)
    row = lambda a, l: a[l][None, :]

    xs = jnp.concatenate([x, ctx], axis=0)
    (h,) = rowwise("modnorm", f_modnorm, [xs], sels=[sel(0, (0, 1))], pars=[row(p["norm1_g"], 0)], outs=[(D, bf16)], n_lat=T)
    for l in range(depth):
        win = mm(h, w_in[l], "w_in%d" % l)
        parts, off = [], 0
        for s in IN_SIZES:
            parts.append(win[:, off:off + s])
            off += s
        aq, ak, av, hq, hff, hfb, hi, hgt, su, sv = parts
        q, k, v = rowwise("qkprep%d" % l, f_qkprep, [aq, ak, av], consts=[cosf, sinf],
                          pars=[row(p["q_norm_g"], l), row(p["k_norm_g"], l)],
                          outs=[(aq.shape[1], bf16), (ak.shape[1], bf16), (av.shape[1], bf16)], n_lat=T)
        attn = attention(q, k, v, T, "attn%d" % l)
        o_f = hgrn(hq, hff, hi, lbs[l][0:1], T, False, "hgrn_f%d" % l)
        o_b = hgrn(hq, hfb, hi, lbs[l][1:2], T, True, "hgrn_b%d" % l)
        (hg,) = rowwise("hgout%d" % l, f_hgout, [o_f, o_b, hgt], pars=[row(p["hg_norm_g"], l)], outs=[(hgt.shape[1], bf16)], n_lat=T)
        (sg,) = rowwise("sgate%d" % l, f_sgate, [su, sv],
                        pars=[row(p["sg_norm_g"], l), p["sg_w"][l]] + [p["sg_b"][l, gi][:, None] for gi in range(SG_GROUPS)],
                        outs=[(su.shape[1], bf16)], n_lat=T)
        mix = jnp.concatenate([attn, hg, sg], axis=1)
        y = mm(mix, w_out[l], "w_out%d" % l)
        xs, h2 = rowwise("resid_a%d" % l, f_resid_modnorm, [xs, y], sels=[sel(l, (2, 3, 4))], pars=[row(p["norm2_g"], l)],
                         outs=[(D, f32), (D, bf16)], n_lat=T)
        up = mm(h2, w_up[l], "w_up%d" % l)
        act = convact(up, p["conv_w"][l], row(p["conv_b"], l), T, "convact%d" % l)
        z = mm(act, w_down[l], "w_down%d" % l)
        if l + 1 < depth:
            xs, h = rowwise("resid_b%d" % l, f_resid_modnorm, [xs, z], sels=[jnp.concatenate([sel(l, (5,)), sel(l + 1, (0, 1))], axis=1)],
                            pars=[row(p["norm1_g"], l + 1)], outs=[(D, f32), (D, bf16)], n_lat=T)
        else:
            (rowloss,) = rowwise("resid_final", f_resid_final, [xs[:T], z[:T]], consts=[tgt], sels=[sel(l, (5,))[0:1]],
                                 pars=[p["final_norm_g"][None, :]], outs=[(1, f32)])
    return 0.5 * jnp.sum(rowloss)


def kernel(x, c, ctx, c_ctx, w_ada, b_ada, norm1_g, w_in, q_norm_g, k_norm_g, hg_lower_bounds, hg_norm_g, sg_norm_g, sg_w, sg_b, w_out, norm2_g, w_up, conv_w, conv_b, w_down, final_norm_g, loss_target, m_c_ctx, m_w_ada, m_b_ada, m_norm1_g, m_w_in, m_q_norm_g, m_k_norm_g, m_hg_lower_bounds, m_hg_norm_g, m_sg_norm_g, m_sg_w, m_sg_b, m_w_out, m_norm2_g, m_w_up, m_conv_w, m_conv_b, m_w_down, m_final_norm_g, v_c_ctx, v_w_ada, v_b_ada, v_norm1_g, v_w_in, v_q_norm_g, v_k_norm_g, v_hg_lower_bounds, v_hg_norm_g, v_sg_norm_g, v_sg_w, v_sg_b, v_w_out, v_norm2_g, v_w_up, v_conv_w, v_conv_b, v_w_down, v_final_norm_g):
    given = dict(c_ctx=c_ctx, w_ada=w_ada, b_ada=b_ada, norm1_g=norm1_g, w_in=w_in, q_norm_g=q_norm_g, k_norm_g=k_norm_g,
                 hg_lower_bounds=hg_lower_bounds, hg_norm_g=hg_norm_g, sg_norm_g=sg_norm_g, sg_w=sg_w, sg_b=sg_b, w_out=w_out,
                 norm2_g=norm2_g, w_up=w_up, conv_w=conv_w, conv_b=conv_b, w_down=w_down, final_norm_g=final_norm_g)
    moments_m = dict(c_ctx=m_c_ctx, w_ada=m_w_ada, b_ada=m_b_ada, norm1_g=m_norm1_g, w_in=m_w_in, q_norm_g=m_q_norm_g,
                     k_norm_g=m_k_norm_g, hg_lower_bounds=m_hg_lower_bounds, hg_norm_g=m_hg_norm_g, sg_norm_g=m_sg_norm_g,
                     sg_w=m_sg_w, sg_b=m_sg_b, w_out=m_w_out, norm2_g=m_norm2_g, w_up=m_w_up, conv_w=m_conv_w, conv_b=m_conv_b,
                     w_down=m_w_down, final_norm_g=m_final_norm_g)
    moments_v = dict(c_ctx=v_c_ctx, w_ada=v_w_ada, b_ada=v_b_ada, norm1_g=v_norm1_g, w_in=v_w_in, q_norm_g=v_q_norm_g,
                     k_norm_g=v_k_norm_g, hg_lower_bounds=v_hg_lower_bounds, hg_norm_g=v_hg_norm_g, sg_norm_g=v_sg_norm_g,
                     sg_w=v_sg_w, sg_b=v_sg_b, w_out=v_w_out, norm2_g=v_norm2_g, w_up=v_w_up, conv_w=v_conv_w, conv_b=v_conv_b,
                     w_down=v_w_down, final_norm_g=v_final_norm_g)
    T, D = x.shape[1], x.shape[2]
    L = ctx.shape[1]
    me = _my_index()
    axes = ("x", "y", "c")

    c_all = allgather_small(jnp.pad(c, ((0, 7), (0, 0))), "gather_c").reshape(N_DEV, 8, D)[:, 0]
    depth, hw = hg_lower_bounds.shape[1], hg_lower_bounds.shape[2]
    cw = conv_w.shape[2]
    small = jnp.concatenate([jnp.pad(hg_lower_bounds.reshape(2 * depth, hw), ((0, 0), (0, cw - hw))), conv_w.reshape(3 * depth, cw)], axis=0)
    rows_small = small.shape[0]
    small = allgather_small(jnp.pad(small, ((0, (-rows_small) % 8), (0, 0))), "gather_small").reshape(N_DEV, -1, cw)
    hg_full = small[:, :2 * depth, :hw].reshape(N_DEV, 2, depth, hw).transpose(1, 2, 0, 3).reshape(2, depth, N_DEV * hw)
    cw_full = small[:, 2 * depth:2 * depth + 3 * depth].reshape(N_DEV, depth, 3, cw).transpose(1, 2, 0, 3).reshape(depth, 3, N_DEV * cw)

    p = dict(given, hg_lower_bounds=hg_full, conv_w=cw_full, c_ctx=c_ctx[None, :])
    cosf, sinf = _rope_tables(T, L)
    loss, (gp, gx) = jax.value_and_grad(_loss_fn, argnums=(0, 1))(p, x[0], ctx[0], c_all, loss_target[0], cosf, sinf)
    loss = lax.psum(loss, axes)

    partial = ['c_ctx', 'norm1_g', 'q_norm_g', 'k_norm_g', 'hg_lower_bounds', 'hg_norm_g', 'sg_norm_g', 'sg_w', 'sg_b',
               'norm2_g', 'conv_w', 'conv_b', 'final_norm_g']
    summed = dict(zip(partial, allreduce_small([gp[n] for n in partial], "reduce_small")))
    grads = dict(gp, **summed)
    grads['c_ctx'] = grads['c_ctx'][0]
    grads['hg_lower_bounds'] = lax.dynamic_slice_in_dim(grads['hg_lower_bounds'], me * hw, hw, axis=2)
    grads['conv_w'] = lax.dynamic_slice_in_dim(grads['conv_w'], me * cw, cw, axis=2)

    delta, new_m, new_v = {}, {}, {}
    for n in WEIGHTS:
        delta[n], new_m[n], new_v[n] = adamw(given[n], grads[n], moments_m[n], moments_v[n], "adamw_" + n)
    return (loss, gx[None], *[grads[n] for n in WEIGHTS], *[delta[n] for n in WEIGHTS],
            *[new_m[n] for n in WEIGHTS], *[new_v[n] for n in WEIGHTS])
```

```python
import functools

import jax
import jax.numpy as jnp
from jax import lax
from jax.experimental import pallas as pl
from jax.experimental.pallas import tpu as pltpu

f32 = jnp.float32
bf16 = jnp.bfloat16
HI = lax.Precision.HIGHEST
MESH = pl.DeviceIdType.MESH

EPS = 1e-6
F_MIN = 1e-30
GRID_W = 64
ROPE_THETA = 10000.0
HEAD = 128
ATTN_HEADS, ATTN_KV = 8, 2
ATTN_GROUP = ATTN_HEADS // ATTN_KV
HG_HEADS = 4
SG_GROUPS = 4
SG_CHUNK = 128
HG_CHUNK = 16
HG_GROUP = 16
IN_SIZES = (1024, 256, 256, 512, 512, 512, 512, 512, 512, 512)
N_DEV = 8
ROW_BLOCK = 256
MAX_TK = 2816
ADAM_LR, ADAM_B1, ADAM_B2, ADAM_EPS, ADAM_WD, ADAM_STEP = 0.001, 0.9, 0.999, 1e-08, 0.01, 10

WEIGHTS = ['c_ctx', 'w_ada', 'b_ada', 'norm1_g', 'w_in', 'q_norm_g', 'k_norm_g', 'hg_lower_bounds', 'hg_norm_g',
           'sg_norm_g', 'sg_w', 'sg_b', 'w_out', 'norm2_g', 'w_up', 'conv_w', 'conv_b', 'w_down', 'final_norm_g']


def _pick(dim, cands):
    for t in cands:
        if dim % t == 0:
            return t
    return dim


def _my_index():
    return 4 * lax.axis_index("x") + 2 * lax.axis_index("y") + lax.axis_index("c")


def _mm_call(a, b, mode, out_dtype, name):
    if mode == "nn":
        (M, K), N = a.shape, b.shape[1]
    elif mode == "nt":
        (M, K), N = a.shape, b.shape[0]
    else:
        (K, M), N = a.shape, b.shape[1]
    tm = _pick(M, (1088, 1024, 512, 256, 128))
    tn = _pick(N, (1024, 512, 256, 128))
    tk = K if K <= MAX_TK else _pick(K, (2816, 2560, 2176, 2048, 1408, 1088, 1024, 512, 256, 128))
    nk = K // tk
    dims = {"nn": (((1,), (0,)), ((), ())), "nt": (((1,), (1,)), ((), ())), "tn": (((0,), (0,)), ((), ()))}[mode]

    def body(a_ref, b_ref, o_ref, *acc):
        prod = lax.dot_general(a_ref[...].astype(bf16), b_ref[...].astype(bf16), dims, preferred_element_type=f32)
        if nk == 1:
            o_ref[...] = prod.astype(o_ref.dtype)
            return
        k = pl.program_id(2)

        @pl.when(k == 0)
        def _():
            acc[0][...] = prod

        @pl.when((k > 0) & (k < nk - 1))
        def _():
            acc[0][...] += prod

        @pl.when(k == nk - 1)
        def _():
            o_ref[...] = (acc[0][...] + prod).astype(o_ref.dtype)

    a_spec = pl.BlockSpec((tk, tm), lambda i, j, k: (k, i)) if mode == "tn" else pl.BlockSpec((tm, tk), lambda i, j, k: (i, k))
    b_spec = pl.BlockSpec((tn, tk), lambda i, j, k: (j, k)) if mode == "nt" else pl.BlockSpec((tk, tn), lambda i, j, k: (k, j))
    return pl.pallas_call(
        body, name=name, grid=(M // tm, N // tn, nk),
        in_specs=[a_spec, b_spec], out_specs=pl.BlockSpec((tm, tn), lambda i, j, k: (i, j)),
        out_shape=jax.ShapeDtypeStruct((M, N), out_dtype),
        scratch_shapes=[pltpu.VMEM((tm, tn), f32)] if nk > 1 else [],
        compiler_params=pltpu.CompilerParams(dimension_semantics=("parallel", "parallel", "arbitrary")),
    )(a, b)


def mm(a, w, name):
    @jax.custom_vjp
    def op(a, w):
        return _mm_call(a, w, "nn", f32, name + "_fwd")

    def fwd(a, w):
        return op(a, w), (a, w)

    def bwd(res, dy):
        a, w = res
        dy = dy.astype(bf16)
        return _mm_call(dy, w, "nt", a.dtype, name + "_bwd_a"), _mm_call(a, dy, "tn", w.dtype, name + "_bwd_w")

    op.defvjp(fwd, bwd)
    return op(a, w)


def _rowwise_specs(rows, consts, sels, pars, tb, nlb):
    specs = [pl.BlockSpec((tb, a.shape[1]), lambda i: (i, 0)) for a in (*rows, *consts)]
    specs += [pl.BlockSpec((None,) + a.shape[1:], lambda i: (jnp.where(i >= nlb, 1, 0), 0, 0)) for a in sels]
    specs += [pl.BlockSpec(a.shape, functools.partial(lambda i, n: (0,) * n, n=a.ndim)) for a in pars]
    return specs


def rowwise(name, f, rows, consts=(), sels=(), pars=(), outs=(), n_lat=None, tb=ROW_BLOCK):
    rows, consts, sels, pars = tuple(rows), tuple(consts), tuple(sels), tuple(pars)
    R = rows[0].shape[0]
    nb = R // tb
    nlb = nb if n_lat is None else n_lat // tb
    n_in = len(rows) + len(consts) + len(sels) + len(pars)
    n_out = len(outs)
    out_dtypes = [d for _, d in outs]
    out_specs = [pl.BlockSpec((tb, w), lambda i: (i, 0)) for w, _ in outs]
    out_shape = [jax.ShapeDtypeStruct((R, w), d) for w, d in outs]
    sem = pltpu.CompilerParams(dimension_semantics=("arbitrary",))

    def run_fwd(rows, consts, sels, pars):
        def body(*refs):
            res = f(*[r[...] for r in refs[:n_in]])
            for o_ref, r in zip(refs[n_in:], res):
                o_ref[...] = r.astype(o_ref.dtype)

        return tuple(pl.pallas_call(
            body, name=name + "_fwd", grid=(nb,), in_specs=_rowwise_specs(rows, consts, sels, pars, tb, nlb),
            out_specs=out_specs, out_shape=out_shape, compiler_params=sem,
        )(*rows, *consts, *sels, *pars))

    def run_bwd(rows, consts, sels, pars, cts):
        nr, nc, ns, npar = len(rows), len(consts), len(sels), len(pars)

        def body(*refs):
            i = pl.program_id(0)
            ins = [r[...] for r in refs[:n_in]]
            ct = tuple(r[...] for r in refs[n_in:n_in + n_out])
            o_refs = refs[n_in + n_out:]
            cvals = ins[nr:nr + nc]

            def g(*d):
                res = f(*d[:nr], *cvals, *d[nr:])
                return tuple(r.astype(t) for r, t in zip(res, out_dtypes))

            _, vjp = jax.vjp(g, *ins[:nr], *ins[nr + nc:])
            grads = vjp(ct)
            for k in range(nr):
                o_refs[k][...] = grads[k].astype(o_refs[k].dtype)
            for k in range(nr, nr + ns + npar):
                first = (i == 0) | (i == nlb) if k < nr + ns else (i == 0)
                gk = grads[k].astype(f32)

                @pl.when(first)
                def _(k=k, gk=gk):
                    o_refs[k][...] = gk

                @pl.when(jnp.logical_not(first))
                def _(k=k, gk=gk):
                    o_refs[k][...] += gk

        in_specs = _rowwise_specs(rows, consts, sels, pars, tb, nlb) + out_specs
        o_specs = [pl.BlockSpec((tb, a.shape[1]), lambda i: (i, 0)) for a in rows]
        o_specs += [pl.BlockSpec((None,) + a.shape[1:], lambda i: (jnp.where(i >= nlb, 1, 0), 0, 0)) for a in sels]
        o_specs += [pl.BlockSpec(a.shape, functools.partial(lambda i, n: (0,) * n, n=a.ndim)) for a in pars]
        o_shape = [jax.ShapeDtypeStruct(a.shape, a.dtype) for a in rows]
        o_shape += [jax.ShapeDtypeStruct(a.shape, f32) for a in (*sels, *pars)]
        res = pl.pallas_call(
            body, name=name + "_bwd", grid=(nb,), in_specs=in_specs, out_specs=o_specs, out_shape=o_shape,
            compiler_params=sem,
        )(*rows, *consts, *sels, *pars, *cts)
        return tuple(res[:nr]), tuple(res[nr:nr + ns]), tuple(res[nr + ns:])

    @jax.custom_vjp
    def op(rows, consts, sels, pars):
        return run_fwd(rows, consts, sels, pars)

    def fwd(rows, consts, sels, pars):
        return op(rows, consts, sels, pars), (rows, consts, sels, pars)

    def bwd(res, cts):
        rows, consts, sels, pars = res
        drows, dsels, dpars = run_bwd(rows, consts, sels, pars, tuple(cts))
        return drows, tuple(jnp.zeros_like(c) for c in consts), dsels, dpars

    op.defvjp(fwd, bwd)
    return op(rows, consts, sels, pars)


def _rms(x, g):
    return x * lax.rsqrt(jnp.mean(x * x, axis=-1, keepdims=True) + EPS) * g


def _silu(x):
    return x * jax.nn.sigmoid(x)


def f_modnorm(x, mods, g):
    return (_rms(x, g) * (1.0 + mods[1:2]) + mods[0:1],)


def f_resid_modnorm(x, y, mods, g):
    xn = x + mods[0:1] * y
    return xn, _rms(xn, g) * (1.0 + mods[2:3]) + mods[1:2]


def f_resid_final(x, y, tgt, mods, g):
    xn = x + mods[0:1] * y
    err = _rms(xn, g) - tgt
    return (jnp.mean(err * err, axis=-1, keepdims=True),)


def f_qkprep(aq, ak, av, cosf, sinf, qg, kg):
    r = lax.broadcasted_iota(jnp.int32, (HEAD, HEAD), 0)
    c = lax.broadcasted_iota(jnp.int32, (HEAD, HEAD), 1)
    swap = jnp.where((r ^ 1) == c, 1.0, 0.0).astype(f32)

    def head(xh, g):
        y = _rms(xh, g)
        ys = jnp.dot(y, swap, precision=HI, preferred_element_type=f32)
        return y * cosf + ys * sinf

    q = jnp.concatenate([head(aq[:, h * HEAD:(h + 1) * HEAD], qg) for h in range(ATTN_HEADS)], axis=1)
    k = jnp.concatenate([head(ak[:, h * HEAD:(h + 1) * HEAD], kg) for h in range(ATTN_KV)], axis=1)
    return q, k, av


def f_hgout(of, ob, gt, g):
    o = of + ob
    y = jnp.concatenate([_rms(o[:, h * HEAD:(h + 1) * HEAD], g) for h in range(HG_HEADS)], axis=1)
    return (y * _silu(gt),)


def f_sgate(u, v, g, w, b0, b1, b2, b3):
    u = jax.nn.gelu(u)
    v = jax.nn.gelu(v)
    bs = (b0, b1, b2, b3)
    cols = []
    for gi in range(SG_GROUPS):
        sl = slice(gi * HEAD, (gi + 1) * HEAD)
        vg = _rms(v[:, sl], g[:, sl])
        parts = []
        for n in range(v.shape[0] // SG_CHUNK):
            vc = vg[n * SG_CHUNK:(n + 1) * SG_CHUNK]
            parts.append(jnp.dot(w[gi].astype(bf16), vc.astype(bf16), preferred_element_type=f32) + bs[gi])
        cols.append(jnp.concatenate(parts, axis=0))
    return (u * jnp.concatenate(cols, axis=1),)


def attention(q, k, v, T, name, tq=ROW_BLOCK):
    NT = q.shape[0]
    nqb, nlb = NT // tq, T // tq
    scale = HEAD ** -0.5
    q_spec = pl.BlockSpec((tq, HEAD), lambda kv, g, i: (i, kv * ATTN_GROUP + g))
    kv_spec = pl.BlockSpec((NT, HEAD), lambda kv, g, i: (0, kv))
    lse_spec = pl.BlockSpec((None, tq, 1), lambda kv, g, i: (kv * ATTN_GROUP + g, i, 0))
    grid = (ATTN_KV, ATTN_GROUP, nqb)
    nt_dims = (((1,), (1,)), ((), ()))
    tn_dims = (((0,), (0,)), ((), ()))

    def on_keys(i, fn):
        @pl.when(i < nlb)
        def _():
            fn(pl.ds(0, NT))

        @pl.when(i >= nlb)
        def _():
            fn(pl.ds(T, NT - T))

    def run_fwd(q, k, v):
        def body(q_ref, k_ref, v_ref, o_ref, lse_ref):
            def run(rows):
                s = lax.dot_general(q_ref[...], k_ref[rows, :], nt_dims, preferred_element_type=f32)
                m = jnp.max(s, axis=-1, keepdims=True) * scale
                p = jnp.exp(s * scale - m)
                l = jnp.sum(p, axis=-1, keepdims=True)
                o = jnp.dot(p.astype(bf16), v_ref[rows, :], preferred_element_type=f32) / l
                o_ref[...] = o.astype(o_ref.dtype)
                lse_ref[...] = m + jnp.log(l)

            on_keys(pl.program_id(2), run)

        return pl.pallas_call(
            body, name=name + "_fwd", grid=grid, in_specs=[q_spec, kv_spec, kv_spec], out_specs=[q_spec, lse_spec],
            out_shape=[jax.ShapeDtypeStruct(q.shape, bf16), jax.ShapeDtypeStruct((ATTN_HEADS, NT, 1), f32)],
            compiler_params=pltpu.CompilerParams(dimension_semantics=("parallel", "parallel", "arbitrary")),
        )(q, k, v)

    def run_bwd(q, k, v, lse, do):
        def body(q_ref, k_ref, v_ref, lse_ref, do_ref, dq_ref, dk_ref, dv_ref):
            g, i = pl.program_id(1), pl.program_id(2)

            @pl.when((g == 0) & (i == 0))
            def _():
                dk_ref[...] = jnp.zeros_like(dk_ref)
                dv_ref[...] = jnp.zeros_like(dv_ref)

            def run(rows):
                qb, kb, vb, dob = q_ref[...], k_ref[rows, :], v_ref[rows, :], do_ref[...]
                s = lax.dot_general(qb, kb, nt_dims, preferred_element_type=f32)
                p = jnp.exp(s * scale - lse_ref[...])
                dp = lax.dot_general(dob, vb, nt_dims, preferred_element_type=f32)
                ds = (p * (dp - jnp.sum(p * dp, axis=-1, keepdims=True)) * scale).astype(bf16)
                dq_ref[...] = jnp.dot(ds, kb, preferred_element_type=f32).astype(dq_ref.dtype)
                dk_ref[rows, :] += lax.dot_general(ds, qb, tn_dims, preferred_element_type=f32)
                dv_ref[rows, :] += lax.dot_general(p.astype(bf16), dob, tn_dims, preferred_element_type=f32)

            on_keys(i, run)

        return pl.pallas_call(
            body, name=name + "_bwd", grid=grid, in_specs=[q_spec, kv_spec, kv_spec, lse_spec, q_spec],
            out_specs=[q_spec, kv_spec, kv_spec],
            out_shape=[jax.ShapeDtypeStruct(q.shape, bf16), jax.ShapeDtypeStruct(k.shape, f32), jax.ShapeDtypeStruct(v.shape, f32)],
            compiler_params=pltpu.CompilerParams(dimension_semantics=("parallel", "arbitrary", "arbitrary")),
        )(q, k, v, lse, do)

    @jax.custom_vjp
    def op(q, k, v):
        return run_fwd(q, k, v)[0]

    def fwd(q, k, v):
        o, lse = run_fwd(q, k, v)
        return o, (q, k, v, lse)

    def bwd(res, do):
        dq, dk, dv = run_bwd(*res, do)
        return dq, dk.astype(bf16), dv.astype(bf16)

    op.defvjp(fwd, bwd)
    return op(q, k, v)


def _bdot(a, b, ca, cb):
    fa, fb = 3 - ca, 3 - cb

    def dot(x, y, cx, cy):
        return lax.dot_general(x.astype(bf16), y.astype(bf16), (((cx,), (cy,)), ((0,), (0,))), preferred_element_type=f32)

    @jax.custom_vjp
    def op(a, b):
        return dot(a, b, ca, cb)

    def fwd(a, b):
        return op(a, b), (a, b)

    def bwd(res, ct):
        a, b = res
        da = dot(ct, b, 2, fb) if ca == 2 else dot(b, ct, fb, 2)
        db = dot(a, ct, fa, 1) if cb == 1 else dot(ct, a, 1, fa)
        return da, db

    op.defvjp(fwd, bwd)
    return op(a, b)


def _chunk_cumsum(x, rev):
    def impl(x, rev):
        n = x.shape[0]
        pos = lax.broadcasted_iota(jnp.int32, x.shape, 0) % HG_CHUNK
        s = 1
        while s < HG_CHUNK:
            if rev:
                x = x + jnp.where(pos < HG_CHUNK - s, pltpu.roll(x, n - s, 0), 0.0)
            else:
                x = x + jnp.where(pos >= s, pltpu.roll(x, s, 0), 0.0)
            s *= 2
        return x

    @jax.custom_vjp
    def op(x):
        return impl(x, rev)

    op.defvjp(lambda x: (op(x), None), lambda _, ct: (impl(ct, not rev),))
    return op(x)


def _hg_group(St, hq, hf, hi, lb, *, rev):
    G, C = HG_GROUP, HG_CHUNK
    R = G * C
    q = _silu(hq)
    f = lb + (1.0 - lb) * jax.nn.sigmoid(hf)
    logf = jnp.log(jnp.maximum(f, F_MIN))
    kk = (1.0 - lb) * jax.nn.sigmoid(-hf)
    b3 = _chunk_cumsum(logf, rev).reshape(G, C, HEAD)
    q3, k3, v3 = q.reshape(G, C, HEAD), kk.reshape(G, C, HEAD), hi.reshape(G, C, HEAD)
    btot = jnp.sum(logf.reshape(G, C, HEAD), axis=1)
    tt = lax.broadcasted_iota(jnp.int32, (G, C, C, HEAD), 1)
    ss = lax.broadcasted_iota(jnp.int32, (G, C, C, HEAD), 2)
    mask = (ss >= tt) if rev else (ss <= tt)
    diff = b3[:, :, None, :] - b3[:, None, :, :]
    dec = jnp.where(mask, jnp.exp(jnp.where(mask, diff, 0.0)), 0.0)
    scores = jnp.sum(q3[:, :, None, :] * k3[:, None, :, :] * dec, axis=-1)
    o_intra = _bdot(scores, v3, 2, 1)
    q_dec = q3 * jnp.exp(b3)
    k_dec = k3 * jnp.exp(btot[:, None, :] - b3)
    kvt = _bdot(v3, k_dec, 1, 1)
    dl = jnp.exp(btot)
    states = [None] * G
    for g in (range(G - 1, -1, -1) if rev else range(G)):
        states[g] = St
        St = St * dl[g:g + 1, :] + kvt[g]
    o_inter = _bdot(q_dec, jnp.stack(states), 2, 2)
    return St, (o_intra + o_inter).reshape(R, HEAD)


def hgrn(hq, hf, hi, lb, T, rev, name):
    NT, W = hq.shape
    R = HG_GROUP * HG_CHUNK
    n_lat, n_ctx = T // R, (NT - T) // R
    nG = n_lat + n_ctx

    def group_of(j):
        if rev:
            return jnp.where(j < n_ctx, nG - 1 - j, n_lat - 1 - (j - n_ctx))
        return jnp.where(j < n_ctx, n_lat + j, j - n_ctx)

    def rows_of(j):
        return pl.ds(pl.multiple_of(group_of(j) * R, R), R)

    col_spec = pl.BlockSpec((NT, HEAD), lambda h: (0, h))
    lb_spec = pl.BlockSpec((1, HEAD), lambda h: (0, h))
    st_spec = pl.BlockSpec((None, nG, HEAD, HEAD), lambda h: (h, 0, 0, 0))
    sem = pltpu.CompilerParams(dimension_semantics=("parallel",))

    def run_fwd(hq, hf, hi, lb):
        def body(hq_ref, hf_ref, hi_ref, lb_ref, o_ref, st_ref):
            def step(j, St):
                st_ref[j] = St
                rows = rows_of(j)
                St, o = _hg_group(St, hq_ref[rows, :], hf_ref[rows, :], hi_ref[rows, :], lb_ref[...], rev=rev)
                o_ref[rows, :] = o
                return St

            lax.fori_loop(0, nG, step, jnp.zeros((HEAD, HEAD), f32))

        return pl.pallas_call(
            body, name=name + "_fwd", grid=(W // HEAD,), in_specs=[col_spec, col_spec, col_spec, lb_spec],
            out_specs=[col_spec, st_spec],
            out_shape=[jax.ShapeDtypeStruct((NT, W), f32), jax.ShapeDtypeStruct((W // HEAD, nG, HEAD, HEAD), f32)],
            compiler_params=sem,
        )(hq, hf, hi, lb)

    def run_bwd(hq, hf, hi, lb, st, do):
        def body(hq_ref, hf_ref, hi_ref, lb_ref, st_ref, do_ref, dq_ref, df_ref, di_ref, dlb_ref):
            def step(jj, carry):
                dS, dlb = carry
                j = nG - 1 - jj
                rows = rows_of(j)
                _, vjp = jax.vjp(functools.partial(_hg_group, rev=rev), st_ref[j], hq_ref[rows, :], hf_ref[rows, :],
                                 hi_ref[rows, :], lb_ref[...])
                dS, dq, df, di, dl = vjp((dS, do_ref[rows, :]))
                dq_ref[rows, :] = dq
                df_ref[rows, :] = df
                di_ref[rows, :] = di
                return dS, dlb + dl

            _, dlb = lax.fori_loop(0, nG, step, (jnp.zeros((HEAD, HEAD), f32), jnp.zeros((1, HEAD), f32)))
            dlb_ref[...] = dlb

        return pl.pallas_call(
            body, name=name + "_bwd", grid=(W // HEAD,),
            in_specs=[col_spec, col_spec, col_spec, lb_spec, st_spec, col_spec],
            out_specs=[col_spec, col_spec, col_spec, lb_spec],
            out_shape=[jax.ShapeDtypeStruct((NT, W), f32)] * 3 + [jax.ShapeDtypeStruct((1, W), f32)],
            compiler_params=sem,
        )(hq, hf, hi, lb, st, do)

    @jax.custom_vjp
    def op(hq, hf, hi, lb):
        return run_fwd(hq, hf, hi, lb)[0]

    def fwd(hq, hf, hi, lb):
        o, st = run_fwd(hq, hf, hi, lb)
        return o, (hq, hf, hi, lb, st)

    def bwd(res, do):
        return tuple(run_bwd(*res, do))

    op.defvjp(fwd, bwd)
    return op(hq, hf, hi, lb)


def lower_bounds(params):
    n = len(params)

    def f(*a):
        m = functools.reduce(jnp.maximum, a)
        e = [jnp.exp(x - m) for x in a]
        s = functools.reduce(lambda u, v: u + v, e)
        p = [x / s for x in e]
        out, run = [], jnp.zeros_like(p[0])
        for l in range(n):
            out.append(run)
            run = run + p[l]
        return tuple(out[l] + p[l] - p[0] for l in range(n))

    shape = [jax.ShapeDtypeStruct(params[0].shape, f32)] * n

    @jax.custom_vjp
    def op(*a):
        def body(*refs):
            for o_ref, r in zip(refs[n:], f(*[x[...] for x in refs[:n]])):
                o_ref[...] = r
        return tuple(pl.pallas_call(body, name="lower_bounds_fwd", out_shape=shape)(*a))

    def fwd(*a):
        return op(*a), a

    def bwd(a, cts):
        def body(*refs):
            _, vjp = jax.vjp(f, *[x[...] for x in refs[:n]])
            for o_ref, r in zip(refs[2 * n:], vjp(tuple(x[...] for x in refs[n:2 * n]))):
                o_ref[...] = r
        return tuple(pl.pallas_call(body, name="lower_bounds_bwd", out_shape=shape)(*a, *cts))

    op.defvjp(fwd, bwd)
    return op(*params)


def _shift_rows(x, d, T):
    n = x.shape[0]
    t = lax.broadcasted_iota(jnp.int32, x.shape, 0)
    y = pltpu.roll(x, d % n, 0)
    edge = ((t == 0) | (t == T)) if d == 1 else ((t == T - 1) | (t == n - 1))
    return jnp.where(edge, 0.0, y)


def _conv(x, w, b, T):
    return b + w[0:1] * _shift_rows(x, 1, T) + w[1:2] * x + w[2:3] * _shift_rows(x, -1, T)


def convact(up, cw, cb, T, name, tc=128):
    NT, F2 = up.shape
    F = F2 // 2
    tc = _pick(F, (tc, 128))
    nf = F // tc
    g_spec = lambda r: pl.BlockSpec((r, tc), lambda j: (0, j))
    v_spec = lambda r: pl.BlockSpec((r, tc), lambda j: (0, j + nf))
    sem = pltpu.CompilerParams(dimension_semantics=("parallel",))

    def run_fwd(up, cw, cb):
        def body(xg_ref, xv_ref, wg_ref, wv_ref, bg_ref, bv_ref, o_ref):
            yg = _conv(xg_ref[...], wg_ref[...], bg_ref[...], T)
            yv = _conv(xv_ref[...], wv_ref[...], bv_ref[...], T)
            o_ref[...] = (_silu(yg) * yv).astype(o_ref.dtype)

        return pl.pallas_call(
            body, name=name + "_fwd", grid=(nf,),
            in_specs=[g_spec(NT), v_spec(NT), g_spec(3), v_spec(3), g_spec(1), v_spec(1)], out_specs=g_spec(NT),
            out_shape=jax.ShapeDtypeStruct((NT, F), bf16), compiler_params=sem,
        )(up, up, cw, cw, cb, cb)

    def run_bwd(up, cw, cb, dact):
        def body(xg_ref, xv_ref, wg_ref, wv_ref, bg_ref, bv_ref, da_ref, dxg_ref, dxv_ref, dwg_ref, dwv_ref, dbg_ref, dbv_ref):
            xg, xv, wg, wv = xg_ref[...], xv_ref[...], wg_ref[...], wv_ref[...]
            yg = _conv(xg, wg, bg_ref[...], T)
            yv = _conv(xv, wv, bv_ref[...], T)
            da = da_ref[...].astype(f32)
            sg = jax.nn.sigmoid(yg)
            dyv = da * yg * sg
            dyg = da * yv * sg * (1.0 + yg * (1.0 - sg))
            for x, w, dy, dx_ref, dw_ref, db_ref in ((xg, wg, dyg, dxg_ref, dwg_ref, dbg_ref), (xv, wv, dyv, dxv_ref, dwv_ref, dbv_ref)):
                dx_ref[...] = w[0:1] * _shift_rows(dy, -1, T) + w[1:2] * dy + w[2:3] * _shift_rows(dy, 1, T)
                dw_ref[...] = jnp.concatenate([
                    jnp.sum(dy * _shift_rows(x, 1, T), axis=0, keepdims=True),
                    jnp.sum(dy * x, axis=0, keepdims=True),
                    jnp.sum(dy * _shift_rows(x, -1, T), axis=0, keepdims=True)], axis=0)
                db_ref[...] = jnp.sum(dy, axis=0, keepdims=True)

        return pl.pallas_call(
            body, name=name + "_bwd", grid=(nf,),
            in_specs=[g_spec(NT), v_spec(NT), g_spec(3), v_spec(3), g_spec(1), v_spec(1), g_spec(NT)],
            out_specs=[g_spec(NT), g_spec(NT), g_spec(3), g_spec(3), g_spec(1), g_spec(1)],
            out_shape=[jax.ShapeDtypeStruct((NT, F), f32)] * 2 + [jax.ShapeDtypeStruct((3, F), f32)] * 2 + [jax.ShapeDtypeStruct((1, F), f32)] * 2,
            compiler_params=sem,
        )(up, up, cw, cw, cb, cb, dact)

    @jax.custom_vjp
    def op(up, cw, cb):
        return run_fwd(up, cw, cb)

    def fwd(up, cw, cb):
        return op(up, cw, cb), (up, cw, cb)

    def bwd(res, dact):
        dxg, dxv, dwg, dwv, dbg, dbv = run_bwd(*res, dact)
        return (jnp.concatenate([dxg, dxv], axis=1), jnp.concatenate([dwg, dwv], axis=1), jnp.concatenate([dbg, dbv], axis=1))

    op.defvjp(fwd, bwd)
    return op(up, cw, cb)


def _peers():
    x, y, c = lax.axis_index("x"), lax.axis_index("y"), lax.axis_index("c")
    return (x, y, c), [(x, y, 1 - c), (1 - x, y, c), (x, 1 - y, c), (1 - x, 1 - y, c),
                       (1 - x, y, 1 - c), (x, 1 - y, 1 - c), (1 - x, 1 - y, 1 - c)]


def _index(dev):
    return 4 * dev[0] + 2 * dev[1] + dev[2]


def allgather_small(x, name):
    m, n = x.shape

    def body(x_ref, out_ref, send_sems, recv_sems, local_sem):
        me, peers = _peers()

        def rows(dev):
            return out_ref.at[pl.ds(pl.multiple_of(_index(dev) * m, 8), m), :]

        mine = pltpu.make_async_copy(x_ref, rows(me), local_sem)
        mine.start()
        sends = [pltpu.make_async_remote_copy(src_ref=x_ref, dst_ref=rows(me), send_sem=send_sems.at[k], recv_sem=recv_sems.at[k],
                                              device_id=p, device_id_type=MESH) for k, p in enumerate(peers)]
        for cp in sends:
            cp.start()
        for k, p in enumerate(peers):
            pltpu.make_async_remote_copy(src_ref=x_ref, dst_ref=rows(p), send_sem=send_sems.at[k], recv_sem=recv_sems.at[k],
                                         device_id=p, device_id_type=MESH).wait_recv()
        for cp in sends:
            cp.wait_send()
        mine.wait()

    return pl.pallas_call(
        body, name=name, out_shape=jax.ShapeDtypeStruct((N_DEV * m, n), x.dtype),
        in_specs=[pl.BlockSpec(memory_space=pltpu.VMEM)], out_specs=pl.BlockSpec(memory_space=pltpu.VMEM),
        scratch_shapes=[pltpu.SemaphoreType.DMA((7,)), pltpu.SemaphoreType.DMA((7,)), pltpu.SemaphoreType.DMA],
    )(x)


HBM_SPEC = pl.BlockSpec(memory_space=pltpu.HBM)
SEM_SPEC = pl.BlockSpec(memory_space=pltpu.SEMAPHORE)
SPLIT_PARAMS = dict(compiler_params=pltpu.CompilerParams(has_side_effects=pltpu.SideEffectType.DATAFLOW_SIDE_EFFECTING))
N_PEERS = N_DEV - 1


def _part(ref, kind, j, width):
    if kind == "col":
        return ref.at[:, pl.ds(pl.multiple_of(j * width, 128), width)]
    return ref.at[pl.ds(pl.multiple_of(j * width, 8), width), :]


def _in_hbm(a):
    return pltpu.with_memory_space_constraint(a, pltpu.HBM)


def gather_start(shards, lands, kinds):
    n = len(shards)
    widths = [s.shape[1] if k == "col" else s.shape[0] for s, k in zip(shards, kinds)]

    def body(*refs):
        srcs, lnds, send_sems, recv_sems, token = refs[:n], refs[n:2 * n], refs[2 * n], refs[2 * n + 1], refs[-1]
        me, peers = _peers()
        for a in range(n):
            for k, p in enumerate(peers):
                pltpu.make_async_remote_copy(
                    src_ref=srcs[a], dst_ref=_part(lnds[a], kinds[a], _index(me), widths[a]),
                    send_sem=send_sems.at[N_PEERS * a + k], recv_sem=recv_sems.at[N_PEERS * a + k],
                    device_id=p, device_id_type=MESH).start()
        token[...] = jnp.zeros_like(token)

    res = pl.pallas_call(
        body, name="gather_start",
        out_shape=(pltpu.SemaphoreType.DMA((N_PEERS * n,)), pltpu.SemaphoreType.DMA((N_PEERS * n,)),
                   *[pltpu.HBM(a.shape, a.dtype) for a in (*shards, *lands)], jax.ShapeDtypeStruct((8, 128), f32)),
        in_specs=[HBM_SPEC] * (2 * n), out_specs=(SEM_SPEC, SEM_SPEC, *[HBM_SPEC] * (2 * n), pl.BlockSpec(memory_space=pltpu.VMEM)),
        input_output_aliases={i: 2 + i for i in range(2 * n)}, **SPLIT_PARAMS,
    )(*[_in_hbm(a) for a in (*shards, *lands)])
    return res[0], res[1], res[2:2 + n], res[2 + n:2 + 2 * n], res[-1]


def gather_wait(a, shard, land, kind, send_sems, recv_sems, after, name):
    width = shard.shape[1] if kind == "col" else shard.shape[0]

    def body(src_ref, land_ref, send_ref, recv_ref, after_ref, src_out, land_out):
        _, peers = _peers()
        for k, p in enumerate(peers):
            cp = pltpu.make_async_remote_copy(
                src_ref=src_ref, dst_ref=_part(land_ref, kind, _index(p), width),
                send_sem=send_ref.at[N_PEERS * a + k], recv_sem=recv_ref.at[N_PEERS * a + k], device_id=p, device_id_type=MESH)
            cp.wait_send()
            cp.wait_recv()

    return pl.pallas_call(
        body, name=name, out_shape=(pltpu.HBM(shard.shape, shard.dtype), pltpu.HBM(land.shape, land.dtype)),
        in_specs=(HBM_SPEC, HBM_SPEC, SEM_SPEC, SEM_SPEC, pl.BlockSpec(memory_space=pl.ANY)), out_specs=(HBM_SPEC, HBM_SPEC),
        input_output_aliases={0: 0, 1: 1}, **SPLIT_PARAMS,
    )(shard, land, send_sems, recv_sems, after)[1]


def scatter_start(full, slots, kind, name):
    width = slots.shape[2] if kind == "col" else slots.shape[1]

    def body(full_ref, slots_ref, send_sems, recv_sems, full_out, slots_out, token):
        me, peers = _peers()
        for k, p in enumerate(peers):
            pltpu.make_async_remote_copy(
                src_ref=_part(full_ref, kind, _index(p), width), dst_ref=slots_ref.at[_index(me)],
                send_sem=send_sems.at[k], recv_sem=recv_sems.at[k], device_id=p, device_id_type=MESH).start()
        token[...] = jnp.zeros_like(token)

    return pl.pallas_call(
        body, name=name,
        out_shape=(pltpu.SemaphoreType.DMA((N_PEERS,)), pltpu.SemaphoreType.DMA((N_PEERS,)), pltpu.HBM(full.shape, full.dtype),
                   pltpu.HBM(slots.shape, slots.dtype), jax.ShapeDtypeStruct((8, 128), f32)),
        in_specs=(HBM_SPEC, HBM_SPEC), out_specs=(SEM_SPEC, SEM_SPEC, HBM_SPEC, HBM_SPEC, pl.BlockSpec(memory_space=pltpu.VMEM)),
        input_output_aliases={0: 2, 1: 3}, **SPLIT_PARAMS,
    )(_in_hbm(full), _in_hbm(slots))


def scatter_wait(full, slots, kind, send_sems, recv_sems, after, name):
    width = slots.shape[2] if kind == "col" else slots.shape[1]

    def body(full_ref, slots_ref, send_ref, recv_ref, after_ref, full_out, slots_out):
        me, peers = _peers()
        for k, p in enumerate(peers):
            cp = pltpu.make_async_remote_copy(
                src_ref=_part(full_ref, kind, _index(p), width), dst_ref=slots_ref.at[_index(p)],
                send_sem=send_ref.at[k], recv_sem=recv_ref.at[k], device_id=p, device_id_type=MESH)
            cp.wait_send()
            cp.wait_recv()

    return pl.pallas_call(
        body, name=name, out_shape=(pltpu.HBM(full.shape, full.dtype), pltpu.HBM(slots.shape, slots.dtype)),
        in_specs=(HBM_SPEC, HBM_SPEC, SEM_SPEC, SEM_SPEC, pl.BlockSpec(memory_space=pl.ANY)), out_specs=(HBM_SPEC, HBM_SPEC),
        input_output_aliases={0: 0, 1: 1}, **SPLIT_PARAMS,
    )(full, slots, send_sems, recv_sems, after)[1]


def sum_slots(x, name):
    _, R, C = x.shape
    tr = _pick(R, (256, 128, 64, 32, 16, 8))

    def body(x_ref, o_ref):
        acc = x_ref[0].astype(f32)
        for d in range(1, N_DEV):
            acc = acc + x_ref[d].astype(f32)
        o_ref[...] = acc

    return pl.pallas_call(
        body, name=name, grid=(R // tr,), in_specs=[pl.BlockSpec((N_DEV, tr, C), lambda i: (0, i, 0))],
        out_specs=pl.BlockSpec((tr, C), lambda i: (i, 0)), out_shape=jax.ShapeDtypeStruct((R, C), f32),
        compiler_params=pltpu.CompilerParams(dimension_semantics=("parallel",)),
    )(x)


BIG = (("w_in", "col"), ("w_out", "row"), ("w_up", "col"), ("w_down", "row"))


class BigWeights:
    def __init__(self, given, me):
        self.me = me
        self.kinds = dict(BIG)
        depth = given["w_in"].shape[0]
        self.keys = [(l, n) for l in range(depth) for n, _ in BIG]
        shards, lands = [], []
        for l, n in self.keys:
            s = given[n][l].astype(bf16)
            r, c = s.shape
            if self.kinds[n] == "col":
                land = lax.dynamic_update_slice(lax.empty((r, N_DEV * c), bf16), s, (0, me * c))
            else:
                land = lax.dynamic_update_slice(lax.empty((N_DEV * r, c), bf16), s, (me * r, 0))
            shards.append(s)
            lands.append(land)
        self.send, self.recv, self.shards, self.lands, _ = gather_start(shards, lands, [self.kinds[n] for _, n in self.keys])
        self.pending = {}

    def get(self, l, n, after):
        a = self.keys.index((l, n))
        return gather_wait(a, self.shards[a], self.lands[a], self.kinds[n], self.send, self.recv, after, "gather_wait_%s%d" % (n, l))

    def start_scatter(self, l, n, dw):
        kind = self.kinds[n]
        R, C = dw.shape
        r, c = (R, C // N_DEV) if kind == "col" else (R // N_DEV, C)
        own = lax.dynamic_slice(dw, (0, self.me * c) if kind == "col" else (self.me * r, 0), (r, c))
        slots = lax.dynamic_update_slice(lax.empty((N_DEV, r, c), bf16), own[None], (self.me, 0, 0))
        send, recv, full, slots, token = scatter_start(dw, slots, kind, "scatter_start_%s%d" % (n, l))
        self.pending[(l, n)] = (full, slots, send, recv)
        return token

    def grads(self, after):
        out = {}
        for n, kind in BIG:
            per_layer = []
            for l in sorted(k[0] for k in self.keys if k[1] == n):
                full, slots, send, recv = self.pending[(l, n)]
                slots = scatter_wait(full, slots, kind, send, recv, after, "scatter_wait_%s%d" % (n, l))
                per_layer.append(sum_slots(slots, "sum_grads_%s%d" % (n, l)))
            out[n] = jnp.stack(per_layer)
        return out


def allreduce_small(vals, name):
    flat = jnp.concatenate([v.reshape(-1) for v in vals])
    n = flat.shape[0]
    cols = 1024
    m = -(-n // (cols * 8)) * 8
    packed = jnp.pad(flat, (0, m * cols - n)).reshape(m, cols)
    total = sum_slots(allgather_small(packed, name).reshape(N_DEV, m, cols), name + "_sum").reshape(-1)
    out, off = [], 0
    for v in vals:
        out.append(total[off:off + v.size].reshape(v.shape))
        off += v.size
    return out


def ada_mod(c_all, c_ctx, w_ada, b_ada):
    L, D, S = w_ada.shape
    me = _my_index()

    def stacked(c_ctx):
        return jnp.concatenate([c_all, jnp.broadcast_to(c_ctx, (N_DEV, D))], axis=0)

    ts = _pick(S, (512, 384, 256, 128, 64))
    w_spec = pl.BlockSpec((None, D, ts), lambda l, j: (l, 0, j))
    c_spec = pl.BlockSpec((16, D), lambda l, j: (0, 0))
    p_spec = pl.BlockSpec((None, 16, ts), lambda l, j: (l, 0, j))

    def run_fwd(cin, w_ada):
        def body(c_ref, w_ref, o_ref):
            o_ref[...] = jnp.dot(_silu(c_ref[...]).astype(bf16), w_ref[...].astype(bf16), preferred_element_type=f32)

        return pl.pallas_call(
            body, name="ada_fwd", grid=(L, S // ts), in_specs=[c_spec, w_spec], out_specs=p_spec,
            out_shape=jax.ShapeDtypeStruct((L, 16, S), f32),
            compiler_params=pltpu.CompilerParams(dimension_semantics=("parallel", "parallel")),
        )(cin, w_ada)

    def run_bwd(cin, w_ada, dm):
        def body(c_ref, w_ref, dm_ref, gw_ref, dc_ref):
            first = (pl.program_id(0) == 0) & (pl.program_id(1) == 0)
            cv = c_ref[...]
            sg = jax.nn.sigmoid(cv)
            dmv = dm_ref[...].astype(bf16)
            gw_ref[...] = lax.dot_general((cv * sg).astype(bf16), dmv, (((0,), (0,)), ((), ())), preferred_element_type=f32)
            ds = lax.dot_general(dmv, w_ref[...].astype(bf16), (((1,), (1,)), ((), ())), preferred_element_type=f32)
            dc = ds * sg * (1.0 + cv * (1.0 - sg))

            @pl.when(first)
            def _():
                dc_ref[...] = dc

            @pl.when(jnp.logical_not(first))
            def _():
                dc_ref[...] += dc

        return pl.pallas_call(
            body, name="ada_bwd", grid=(L, S // ts), in_specs=[c_spec, w_spec, p_spec], out_specs=[w_spec, c_spec],
            out_shape=[jax.ShapeDtypeStruct((L, D, S), f32), jax.ShapeDtypeStruct((16, D), f32)],
            compiler_params=pltpu.CompilerParams(dimension_semantics=("arbitrary", "arbitrary")),
        )(cin, w_ada, dm)

    def bias_grad(dm_full):
        def body(x_ref, o_ref):
            o_ref[...] = jnp.sum(x_ref[...], axis=0, keepdims=True)

        return pl.pallas_call(
            body, name="ada_bias_grad", grid=(L,), in_specs=[pl.BlockSpec((None, 16, 6 * D), lambda l: (l, 0, 0))],
            out_specs=pl.BlockSpec((None, 1, 6 * D), lambda l: (l, 0, 0)), out_shape=jax.ShapeDtypeStruct((L, 1, 6 * D), f32),
        )(dm_full).reshape(L, 6 * D)

    @jax.custom_vjp
    def op(c_ctx, w_ada, b_ada):
        prod = run_fwd(stacked(c_ctx), w_ada)
        allp = allgather_small(prod.reshape(L * 16, S), "ada_gather").reshape(N_DEV, L, 16, S)
        allp = allp.transpose(1, 2, 0, 3).reshape(L, 16, N_DEV * S)
        mine = lax.dynamic_index_in_dim(allp, me, axis=1, keepdims=False) + b_ada
        ctx = allp[:, N_DEV] + b_ada
        return jnp.stack([mine, ctx], axis=1).reshape(L, 2, 6, D)

    def fwd(c_ctx, w_ada, b_ada):
        return op(c_ctx, w_ada, b_ada), (c_ctx, w_ada)

    def bwd(res, dmod):
        c_ctx, w_ada = res
        dm = dmod.reshape(L * 2, 6 * D)
        gathered = allgather_small(jnp.pad(dm, ((0, (-2 * L) % 8), (0, 0))), "ada_grad_gather")
        gathered = gathered.reshape(N_DEV, -1, 6 * D)[:, :2 * L].reshape(N_DEV, L, 2, 6 * D)
        dm_full = gathered.transpose(1, 2, 0, 3).reshape(L, 16, 6 * D)
        dm_mine = lax.dynamic_slice_in_dim(dm_full, me * S, S, axis=2)
        gw, dc = run_bwd(stacked(c_ctx), w_ada, dm_mine)
        d_cctx = jnp.sum(dc[N_DEV:], axis=0, keepdims=True)
        return d_cctx, gw, bias_grad(dm_full)

    op.defvjp(fwd, bwd)
    return op(c_ctx, w_ada, b_ada)


def adamw(w, g, m, v, name):
    shape = w.shape
    C = shape[-1]
    R = w.size // C
    tr = _pick(R, (256, 128, 64, 32, 16, 8)) if R * C * 4 > (1 << 20) else R
    c1 = 1.0 / (1.0 - ADAM_B1 ** ADAM_STEP)
    c2 = 1.0 / (1.0 - ADAM_B2 ** ADAM_STEP)

    def body(w_ref, g_ref, m_ref, v_ref, d_ref, mo_ref, vo_ref):
        gv = g_ref[...]
        mn = ADAM_B1 * m_ref[...] + (1.0 - ADAM_B1) * gv
        vn = ADAM_B2 * v_ref[...] + (1.0 - ADAM_B2) * gv * gv
        d_ref[...] = -ADAM_LR * ((mn * c1) / (jnp.sqrt(vn * c2) + ADAM_EPS) + ADAM_WD * w_ref[...])
        mo_ref[...] = mn
        vo_ref[...] = vn

    spec = pl.BlockSpec((tr, C), lambda i: (i, 0))
    res = pl.pallas_call(
        body, name=name, grid=(R // tr,), in_specs=[spec] * 4, out_specs=[spec] * 3,
        out_shape=[jax.ShapeDtypeStruct((R, C), f32)] * 3,
        compiler_params=pltpu.CompilerParams(dimension_semantics=("parallel",)),
    )(*[a.reshape(R, C) for a in (w, g, m, v)])
    return tuple(r.reshape(shape) for r in res)


def _rope_tables(T, L):
    rows = T // GRID_W
    row = jnp.repeat(jnp.arange(rows, dtype=f32), GRID_W)
    col = jnp.tile(jnp.arange(GRID_W, dtype=f32), rows)
    n_freq = HEAD // 4
    inv = ROPE_THETA ** (-jnp.arange(n_freq, dtype=f32) / n_freq)
    ang = jnp.concatenate([row[:, None] * inv, col[:, None] * inv], axis=-1)
    cos = jnp.repeat(jnp.cos(ang), 2, axis=-1)
    sin = jnp.repeat(jnp.sin(ang), 2, axis=-1) * jnp.tile(jnp.array([-1.0, 1.0], f32), HEAD // 2)
    return (jnp.concatenate([cos, jnp.ones((L, HEAD), f32)]), jnp.concatenate([sin, jnp.zeros((L, HEAD), f32)]))


def _sel(mod, l, idx):
    return jnp.stack([mod[l, :, i] for i in idx], axis=1)


def _row(a, l):
    return a[l][None, :]


def _first_segment(T, ctx):
    def seg(x, mod, p):
        xs = jnp.concatenate([x, ctx], axis=0)
        (h,) = rowwise("modnorm", f_modnorm, [xs], sels=[_sel(mod, 0, (0, 1))], pars=[_row(p["norm1_g"], 0)],
                       outs=[(x.shape[1], bf16)], n_lat=T)
        return xs, h

    return seg


def _mixer_segment(l, T, cosf, sinf):
    row = _row

    def seg(xs, h, mod, lbs, p, w_in, w_out):
        D = xs.shape[1]
        win = mm(h, w_in, "w_in%d" % l)
        parts, off = [], 0
        for s in IN_SIZES:
            parts.append(win[:, off:off + s])
            off += s
        aq, ak, av, hq, hff, hfb, hi, hgt, su, sv = parts
        q, k, v = rowwise("qkprep%d" % l, f_qkprep, [aq, ak, av], consts=[cosf, sinf],
                          pars=[row(p["q_norm_g"], l), row(p["k_norm_g"], l)],
                          outs=[(aq.shape[1], bf16), (ak.shape[1], bf16), (av.shape[1], bf16)], n_lat=T)
        attn = attention(q, k, v, T, "attn%d" % l)
        o_f = hgrn(hq, hff, hi, lbs[0:1], T, False, "hgrn_f%d" % l)
        o_b = hgrn(hq, hfb, hi, lbs[1:2], T, True, "hgrn_b%d" % l)
        (hg,) = rowwise("hgout%d" % l, f_hgout, [o_f, o_b, hgt], pars=[row(p["hg_norm_g"], l)], outs=[(hgt.shape[1], bf16)], n_lat=T)
        (sg,) = rowwise("sgate%d" % l, f_sgate, [su, sv],
                        pars=[row(p["sg_norm_g"], l), p["sg_w"][l]] + [p["sg_b"][l, gi][:, None] for gi in range(SG_GROUPS)],
                        outs=[(su.shape[1], bf16)], n_lat=T)
        mix = jnp.concatenate([attn, hg, sg], axis=1)
        y = mm(mix, w_out, "w_out%d" % l)
        return rowwise("resid_a%d" % l, f_resid_modnorm, [xs, y], sels=[_sel(mod, l, (2, 3, 4))], pars=[row(p["norm2_g"], l)],
                       outs=[(D, f32), (D, bf16)], n_lat=T)

    return seg


def _ffn_segment(l, depth, T, tgt):
    row = _row

    def seg(xs, h2, mod, p, w_up, w_down):
        D = xs.shape[1]
        up = mm(h2, w_up, "w_up%d" % l)
        act = convact(up, p["conv_w"][l], row(p["conv_b"], l), T, "convact%d" % l)
        z = mm(act, w_down, "w_down%d" % l)
        if l + 1 < depth:
            return rowwise("resid_b%d" % l, f_resid_modnorm, [xs, z],
                           sels=[jnp.concatenate([_sel(mod, l, (5,)), _sel(mod, l + 1, (0, 1))], axis=1)],
                           pars=[row(p["norm1_g"], l + 1)], outs=[(D, f32), (D, bf16)], n_lat=T)
        return rowwise("resid_final", f_resid_final, [xs[:T], z[:T]], consts=[tgt], sels=[_sel(mod, l, (5,))[0:1]],
                       pars=[p["final_norm_g"][None, :]], outs=[(1, f32)])

    return seg


SEGMENT_PARAMS = ("norm1_g", "q_norm_g", "k_norm_g", "hg_norm_g", "sg_norm_g", "sg_w", "sg_b", "norm2_g", "conv_w", "conv_b",
                  "final_norm_g")


def _loss_and_grads(p, big, x, ctx, c_all, tgt, cosf, sinf):
    T = x.shape[0]
    depth = p["norm1_g"].shape[0]
    add = lambda a, b: jax.tree.map(jnp.add, a, b)
    small = {n: p[n] for n in SEGMENT_PARAMS}

    mod, vjp_mod = jax.vjp(lambda cc, wa, ba: ada_mod(c_all, cc, wa, ba), p["c_ctx"], p["w_ada"], p["b_ada"])
    lbs, vjp_lbs = jax.vjp(lambda hg: lower_bounds([hg[:, l] for l in range(depth)]), p["hg_lower_bounds"])
    (xs, h), vjp_first = jax.vjp(_first_segment(T, ctx), x, mod, small)
    vjps = []
    for l in range(depth):
        w_in, w_out = big.get(l, "w_in", h), big.get(l, "w_out", h)
        (xs, h2), vj = jax.vjp(_mixer_segment(l, T, cosf, sinf), xs, h, mod, lbs[l], small, w_in, w_out)
        vjps.append(vj)
        w_up, w_down = big.get(l, "w_up", h2), big.get(l, "w_down", h2)
        out, vj = jax.vjp(_ffn_segment(l, depth, T, tgt), xs, h2, mod, small, w_up, w_down)
        vjps.append(vj)
        if l + 1 < depth:
            xs, h = out
    (rowloss,) = out
    loss = 0.5 * jnp.sum(rowloss)

    ct = (jnp.full(rowloss.shape, 0.5, f32),)
    d_mod, d_small, d_lbs = jnp.zeros_like(mod), jax.tree.map(jnp.zeros_like, small), [None] * depth
    for l in range(depth - 1, -1, -1):
        dxs, dh2, dm, ds, d_up, d_down = vjps[2 * l + 1](ct)
        tokens = (big.start_scatter(l, "w_up", d_up), big.start_scatter(l, "w_down", d_down))
        dxs, dh2, _ = lax.optimization_barrier((dxs, dh2, tokens))
        d_mod, d_small = d_mod + dm, add(d_small, ds)
        dxs, dh, dm, d_lbs[l], ds, d_in, d_out = vjps[2 * l]((dxs, dh2))
        tokens = (big.start_scatter(l, "w_in", d_in), big.start_scatter(l, "w_out", d_out))
        dxs, dh, _ = lax.optimization_barrier((dxs, dh, tokens))
        d_mod, d_small = d_mod + dm, add(d_small, ds)
        ct = (dxs, dh)
    dx, dm, ds = vjp_first(ct)
    d_cc, d_wada, d_bada = vjp_mod(d_mod + dm)
    (d_hg,) = vjp_lbs(tuple(d_lbs))
    grads = dict(add(d_small, ds), c_ctx=d_cc, w_ada=d_wada, b_ada=d_bada, hg_lower_bounds=d_hg)
    return loss, grads, dx


def kernel(x, c, ctx, c_ctx, w_ada, b_ada, norm1_g, w_in, q_norm_g, k_norm_g, hg_lower_bounds, hg_norm_g, sg_norm_g, sg_w, sg_b, w_out, norm2_g, w_up, conv_w, conv_b, w_down, final_norm_g, loss_target, m_c_ctx, m_w_ada, m_b_ada, m_norm1_g, m_w_in, m_q_norm_g, m_k_norm_g, m_hg_lower_bounds, m_hg_norm_g, m_sg_norm_g, m_sg_w, m_sg_b, m_w_out, m_norm2_g, m_w_up, m_conv_w, m_conv_b, m_w_down, m_final_norm_g, v_c_ctx, v_w_ada, v_b_ada, v_norm1_g, v_w_in, v_q_norm_g, v_k_norm_g, v_hg_lower_bounds, v_hg_norm_g, v_sg_norm_g, v_sg_w, v_sg_b, v_w_out, v_norm2_g, v_w_up, v_conv_w, v_conv_b, v_w_down, v_final_norm_g):
    given = dict(c_ctx=c_ctx, w_ada=w_ada, b_ada=b_ada, norm1_g=norm1_g, w_in=w_in, q_norm_g=q_norm_g, k_norm_g=k_norm_g,
                 hg_lower_bounds=hg_lower_bounds, hg_norm_g=hg_norm_g, sg_norm_g=sg_norm_g, sg_w=sg_w, sg_b=sg_b, w_out=w_out,
                 norm2_g=norm2_g, w_up=w_up, conv_w=conv_w, conv_b=conv_b, w_down=w_down, final_norm_g=final_norm_g)
    moments_m = dict(c_ctx=m_c_ctx, w_ada=m_w_ada, b_ada=m_b_ada, norm1_g=m_norm1_g, w_in=m_w_in, q_norm_g=m_q_norm_g,
                     k_norm_g=m_k_norm_g, hg_lower_bounds=m_hg_lower_bounds, hg_norm_g=m_hg_norm_g, sg_norm_g=m_sg_norm_g,
                     sg_w=m_sg_w, sg_b=m_sg_b, w_out=m_w_out, norm2_g=m_norm2_g, w_up=m_w_up, conv_w=m_conv_w, conv_b=m_conv_b,
                     w_down=m_w_down, final_norm_g=m_final_norm_g)
    moments_v = dict(c_ctx=v_c_ctx, w_ada=v_w_ada, b_ada=v_b_ada, norm1_g=v_norm1_g, w_in=v_w_in, q_norm_g=v_q_norm_g,
                     k_norm_g=v_k_norm_g, hg_lower_bounds=v_hg_lower_bounds, hg_norm_g=v_hg_norm_g, sg_norm_g=v_sg_norm_g,
                     sg_w=v_sg_w, sg_b=v_sg_b, w_out=v_w_out, norm2_g=v_norm2_g, w_up=v_w_up, conv_w=v_conv_w, conv_b=v_conv_b,
                     w_down=v_w_down, final_norm_g=v_final_norm_g)
    T, D = x.shape[1], x.shape[2]
    L = ctx.shape[1]
    me = _my_index()
    axes = ("x", "y", "c")

    c_all = allgather_small(jnp.pad(c, ((0, 7), (0, 0))), "gather_c").reshape(N_DEV, 8, D)[:, 0]
    depth, hw = hg_lower_bounds.shape[1], hg_lower_bounds.shape[2]
    cw = conv_w.shape[2]
    small = jnp.concatenate([jnp.pad(hg_lower_bounds.reshape(2 * depth, hw), ((0, 0), (0, cw - hw))), conv_w.reshape(3 * depth, cw)], axis=0)
    rows_small = small.shape[0]
    small = allgather_small(jnp.pad(small, ((0, (-rows_small) % 8), (0, 0))), "gather_small").reshape(N_DEV, -1, cw)
    hg_full = small[:, :2 * depth, :hw].reshape(N_DEV, 2, depth, hw).transpose(1, 2, 0, 3).reshape(2, depth, N_DEV * hw)
    cw_full = small[:, 2 * depth:2 * depth + 3 * depth].reshape(N_DEV, depth, 3, cw).transpose(1, 2, 0, 3).reshape(depth, 3, N_DEV * cw)

    big = BigWeights(given, me)
    p = {n: a for n, a in given.items() if n not in dict(BIG)}
    p.update(hg_lower_bounds=hg_full, conv_w=cw_full, c_ctx=c_ctx[None, :])
    cosf, sinf = _rope_tables(T, L)
    loss, gp, gx = _loss_and_grads(p, big, x[0], ctx[0], c_all, loss_target[0], cosf, sinf)
    loss = lax.psum(loss, axes)
    gp.update(big.grads(gx))

    partial = ['c_ctx', 'norm1_g', 'q_norm_g', 'k_norm_g', 'hg_lower_bounds', 'hg_norm_g', 'sg_norm_g', 'sg_w', 'sg_b',
               'norm2_g', 'conv_w', 'conv_b', 'final_norm_g']
    summed = dict(zip(partial, allreduce_small([gp[n] for n in partial], "reduce_small")))
    grads = dict(gp, **summed)
    grads['c_ctx'] = grads['c_ctx'][0]
    grads['hg_lower_bounds'] = lax.dynamic_slice_in_dim(grads['hg_lower_bounds'], me * hw, hw, axis=2)
    grads['conv_w'] = lax.dynamic_slice_in_dim(grads['conv_w'], me * cw, cw, axis=2)

    delta, new_m, new_v = {}, {}, {}
    for n in WEIGHTS:
        delta[n], new_m[n], new_v[n] = adamw(given[n], grads[n], moments_m[n], moments_v[n], "adamw_" + n)
    return (loss, gx[None], *[grads[n] for n in WEIGHTS], *[delta[n] for n in WEIGHTS],
            *[new_m[n] for n in WEIGHTS], *[new_v[n] for n in WEIGHTS])
```

```python
import functools

import jax
import jax.numpy as jnp
from jax import lax
from jax.experimental import pallas as pl
from jax.experimental.pallas import tpu as pltpu

f32 = jnp.float32
bf16 = jnp.bfloat16
HI = lax.Precision.HIGHEST
MESH = pl.DeviceIdType.MESH

EPS = 1e-6
F_MIN = 1e-30
GRID_W = 64
ROPE_THETA = 10000.0
HEAD = 128
ATTN_HEADS, ATTN_KV = 8, 2
ATTN_GROUP = ATTN_HEADS // ATTN_KV
HG_HEADS = 4
SG_GROUPS = 4
SG_CHUNK = 128
HG_CHUNK = 16
HG_GROUP = 16
IN_SIZES = (1024, 256, 256, 512, 512, 512, 512, 512, 512, 512)
N_DEV = 8
ROW_BLOCK = 256
MAX_TK = 2816
ADAM_LR, ADAM_B1, ADAM_B2, ADAM_EPS, ADAM_WD, ADAM_STEP = 0.001, 0.9, 0.999, 1e-08, 0.01, 10

WEIGHTS = ['c_ctx', 'w_ada', 'b_ada', 'norm1_g', 'w_in', 'q_norm_g', 'k_norm_g', 'hg_lower_bounds', 'hg_norm_g',
           'sg_norm_g', 'sg_w', 'sg_b', 'w_out', 'norm2_g', 'w_up', 'conv_w', 'conv_b', 'w_down', 'final_norm_g']


def _pick(dim, cands):
    for t in cands:
        if dim % t == 0:
            return t
    return dim


def _my_index():
    return 4 * lax.axis_index("x") + 2 * lax.axis_index("y") + lax.axis_index("c")


def _mm_call(a, b, mode, out_dtype, name):
    if mode == "nn":
        (M, K), N = a.shape, b.shape[1]
    elif mode == "nt":
        (M, K), N = a.shape, b.shape[0]
    else:
        (K, M), N = a.shape, b.shape[1]
    tm = _pick(M, (1088, 1024, 512, 256, 128))
    tn = _pick(N, (1024, 512, 256, 128))
    tk = K if K <= MAX_TK else _pick(K, (2816, 2560, 2176, 2048, 1408, 1088, 1024, 512, 256, 128))
    nk = K // tk
    dims = {"nn": (((1,), (0,)), ((), ())), "nt": (((1,), (1,)), ((), ())), "tn": (((0,), (0,)), ((), ()))}[mode]

    def body(a_ref, b_ref, o_ref, *acc):
        prod = lax.dot_general(a_ref[...].astype(bf16), b_ref[...].astype(bf16), dims, preferred_element_type=f32)
        if nk == 1:
            o_ref[...] = prod.astype(o_ref.dtype)
            return
        k = pl.program_id(2)

        @pl.when(k == 0)
        def _():
            acc[0][...] = prod

        @pl.when((k > 0) & (k < nk - 1))
        def _():
            acc[0][...] += prod

        @pl.when(k == nk - 1)
        def _():
            o_ref[...] = (acc[0][...] + prod).astype(o_ref.dtype)

    a_spec = pl.BlockSpec((tk, tm), lambda i, j, k: (k, i)) if mode == "tn" else pl.BlockSpec((tm, tk), lambda i, j, k: (i, k))
    b_spec = pl.BlockSpec((tn, tk), lambda i, j, k: (j, k)) if mode == "nt" else pl.BlockSpec((tk, tn), lambda i, j, k: (k, j))
    return pl.pallas_call(
        body, name=name, grid=(M // tm, N // tn, nk),
        in_specs=[a_spec, b_spec], out_specs=pl.BlockSpec((tm, tn), lambda i, j, k: (i, j)),
        out_shape=jax.ShapeDtypeStruct((M, N), out_dtype),
        scratch_shapes=[pltpu.VMEM((tm, tn), f32)] if nk > 1 else [],
        compiler_params=pltpu.CompilerParams(dimension_semantics=("parallel", "parallel", "arbitrary")),
    )(a, b)


def mm(a, w, name, out_dtype=f32, split=None):
    def parts(y):
        if split is None:
            return y
        offs = [sum(split[:i]) for i in range(len(split))]
        return tuple(y[:, o:o + s] for o, s in zip(offs, split))

    @jax.custom_vjp
    def op(a, w):
        return parts(_mm_call(a, w, "nn", out_dtype, name + "_fwd"))

    def fwd(a, w):
        return op(a, w), (a, w)

    def bwd(res, dy):
        a, w = res
        dy = dy.astype(bf16) if split is None else jnp.concatenate([d.astype(bf16) for d in dy], axis=1)
        return _mm_call(dy, w, "nt", a.dtype, name + "_bwd_a"), _mm_call(a, dy, "tn", w.dtype, name + "_bwd_w")

    op.defvjp(fwd, bwd)
    return op(a, w)


def _rowwise_specs(rows, consts, sels, pars, tb, nlb):
    specs = [pl.BlockSpec((tb, a.shape[1]), lambda i: (i, 0)) for a in (*rows, *consts)]
    specs += [pl.BlockSpec((None,) + a.shape[1:], lambda i: (jnp.where(i >= nlb, 1, 0), 0, 0)) for a in sels]
    specs += [pl.BlockSpec(a.shape, functools.partial(lambda i, n: (0,) * n, n=a.ndim)) for a in pars]
    return specs


def rowwise(name, f, rows, consts=(), sels=(), pars=(), outs=(), n_lat=None, tb=ROW_BLOCK):
    rows, consts, sels, pars = tuple(rows), tuple(consts), tuple(sels), tuple(pars)
    R = rows[0].shape[0]
    nb = R // tb
    nlb = nb if n_lat is None else n_lat // tb
    n_in = len(rows) + len(consts) + len(sels) + len(pars)
    n_out = len(outs)
    out_dtypes = [d for _, d in outs]
    out_specs = [pl.BlockSpec((tb, w), lambda i: (i, 0)) for w, _ in outs]
    out_shape = [jax.ShapeDtypeStruct((R, w), d) for w, d in outs]
    sem = pltpu.CompilerParams(dimension_semantics=("arbitrary",))

    def run_fwd(rows, consts, sels, pars):
        def body(*refs):
            res = f(*[r[...] for r in refs[:n_in]])
            for o_ref, r in zip(refs[n_in:], res):
                o_ref[...] = r.astype(o_ref.dtype)

        return tuple(pl.pallas_call(
            body, name=name + "_fwd", grid=(nb,), in_specs=_rowwise_specs(rows, consts, sels, pars, tb, nlb),
            out_specs=out_specs, out_shape=out_shape, compiler_params=sem,
        )(*rows, *consts, *sels, *pars))

    def run_bwd(rows, consts, sels, pars, cts):
        nr, nc, ns, npar = len(rows), len(consts), len(sels), len(pars)

        def body(*refs):
            i = pl.program_id(0)
            ins = [r[...] for r in refs[:n_in]]
            ct = tuple(r[...] for r in refs[n_in:n_in + n_out])
            o_refs = refs[n_in + n_out:]
            cvals = ins[nr:nr + nc]

            def g(*d):
                res = f(*d[:nr], *cvals, *d[nr:])
                return tuple(r.astype(t) for r, t in zip(res, out_dtypes))

            _, vjp = jax.vjp(g, *ins[:nr], *ins[nr + nc:])
            grads = vjp(ct)
            for k in range(nr):
                o_refs[k][...] = grads[k].astype(o_refs[k].dtype)
            for k in range(nr, nr + ns + npar):
                first = (i == 0) | (i == nlb) if k < nr + ns else (i == 0)
                gk = grads[k].astype(f32)

                @pl.when(first)
                def _(k=k, gk=gk):
                    o_refs[k][...] = gk

                @pl.when(jnp.logical_not(first))
                def _(k=k, gk=gk):
                    o_refs[k][...] += gk

        in_specs = _rowwise_specs(rows, consts, sels, pars, tb, nlb) + out_specs
        o_specs = [pl.BlockSpec((tb, a.shape[1]), lambda i: (i, 0)) for a in rows]
        o_specs += [pl.BlockSpec((None,) + a.shape[1:], lambda i: (jnp.where(i >= nlb, 1, 0), 0, 0)) for a in sels]
        o_specs += [pl.BlockSpec(a.shape, functools.partial(lambda i, n: (0,) * n, n=a.ndim)) for a in pars]
        o_shape = [jax.ShapeDtypeStruct(a.shape, a.dtype) for a in rows]
        o_shape += [jax.ShapeDtypeStruct(a.shape, f32) for a in (*sels, *pars)]
        res = pl.pallas_call(
            body, name=name + "_bwd", grid=(nb,), in_specs=in_specs, out_specs=o_specs, out_shape=o_shape,
            compiler_params=sem,
        )(*rows, *consts, *sels, *pars, *cts)
        return tuple(res[:nr]), tuple(res[nr:nr + ns]), tuple(res[nr + ns:])

    @jax.custom_vjp
    def op(rows, consts, sels, pars):
        return run_fwd(rows, consts, sels, pars)

    def fwd(rows, consts, sels, pars):
        return op(rows, consts, sels, pars), (rows, consts, sels, pars)

    def bwd(res, cts):
        rows, consts, sels, pars = res
        drows, dsels, dpars = run_bwd(rows, consts, sels, pars, tuple(cts))
        return drows, tuple(jnp.zeros_like(c) for c in consts), dsels, dpars

    op.defvjp(fwd, bwd)
    return op(rows, consts, sels, pars)


def _rms(x, g):
    return x * lax.rsqrt(jnp.mean(x * x, axis=-1, keepdims=True) + EPS) * g


def _silu(x):
    return x * jax.nn.sigmoid(x)


def f_modnorm(x, mods, g):
    return (_rms(x, g) * (1.0 + mods[1:2]) + mods[0:1],)


def f_resid_modnorm(x, y, mods, g):
    xn = x + mods[0:1] * y
    return xn, _rms(xn, g) * (1.0 + mods[2:3]) + mods[1:2]


def f_resid_final(x, y, tgt, mods, g):
    xn = x + mods[0:1] * y
    err = _rms(xn, g) - tgt
    return (jnp.mean(err * err, axis=-1, keepdims=True),)


def f_qkprep(aq, ak, av, cosf, sinf, qg, kg):
    r = lax.broadcasted_iota(jnp.int32, (HEAD, HEAD), 0)
    c = lax.broadcasted_iota(jnp.int32, (HEAD, HEAD), 1)
    swap = jnp.where((r ^ 1) == c, 1.0, 0.0).astype(f32)

    def head(xh, g):
        y = _rms(xh, g)
        ys = jnp.dot(y, swap, precision=HI, preferred_element_type=f32)
        return y * cosf + ys * sinf

    q = jnp.concatenate([head(aq[:, h * HEAD:(h + 1) * HEAD], qg) for h in range(ATTN_HEADS)], axis=1)
    k = jnp.concatenate([head(ak[:, h * HEAD:(h + 1) * HEAD], kg) for h in range(ATTN_KV)], axis=1)
    return q, k, av


def f_hgout(of, ob, gt, g):
    o = of + ob
    y = jnp.concatenate([_rms(o[:, h * HEAD:(h + 1) * HEAD], g) for h in range(HG_HEADS)], axis=1)
    return (y * _silu(gt),)


def f_sgate(u, v, g, w, b0, b1, b2, b3):
    u = jax.nn.gelu(u)
    v = jax.nn.gelu(v)
    bs = (b0, b1, b2, b3)
    cols = []
    for gi in range(SG_GROUPS):
        sl = slice(gi * HEAD, (gi + 1) * HEAD)
        vg = _rms(v[:, sl], g[:, sl])
        parts = []
        for n in range(v.shape[0] // SG_CHUNK):
            vc = vg[n * SG_CHUNK:(n + 1) * SG_CHUNK]
            parts.append(jnp.dot(w[gi].astype(bf16), vc.astype(bf16), preferred_element_type=f32) + bs[gi])
        cols.append(jnp.concatenate(parts, axis=0))
    return (u * jnp.concatenate(cols, axis=1),)


def attention(q, k, v, T, name, tq=ROW_BLOCK):
    NT = q.shape[0]
    nqb, nlb = NT // tq, T // tq
    scale = HEAD ** -0.5
    q_spec = pl.BlockSpec((tq, HEAD), lambda kv, g, i: (i, kv * ATTN_GROUP + g))
    kv_spec = pl.BlockSpec((NT, HEAD), lambda kv, g, i: (0, kv))
    lse_spec = pl.BlockSpec((None, tq, 1), lambda kv, g, i: (kv * ATTN_GROUP + g, i, 0))
    grid = (ATTN_KV, ATTN_GROUP, nqb)
    nt_dims = (((1,), (1,)), ((), ()))
    tn_dims = (((0,), (0,)), ((), ()))

    def on_keys(i, fn):
        @pl.when(i < nlb)
        def _():
            fn(pl.ds(0, NT))

        @pl.when(i >= nlb)
        def _():
            fn(pl.ds(T, NT - T))

    def run_fwd(q, k, v):
        def body(q_ref, k_ref, v_ref, o_ref, lse_ref):
            def run(rows):
                s = lax.dot_general(q_ref[...], k_ref[rows, :], nt_dims, preferred_element_type=f32)
                m = jnp.max(s, axis=-1, keepdims=True) * scale
                p = jnp.exp(s * scale - m)
                l = jnp.sum(p, axis=-1, keepdims=True)
                o = jnp.dot(p.astype(bf16), v_ref[rows, :], preferred_element_type=f32) / l
                o_ref[...] = o.astype(o_ref.dtype)
                lse_ref[...] = m + jnp.log(l)

            on_keys(pl.program_id(2), run)

        return pl.pallas_call(
            body, name=name + "_fwd", grid=grid, in_specs=[q_spec, kv_spec, kv_spec], out_specs=[q_spec, lse_spec],
            out_shape=[jax.ShapeDtypeStruct(q.shape, bf16), jax.ShapeDtypeStruct((ATTN_HEADS, NT, 1), f32)],
            compiler_params=pltpu.CompilerParams(dimension_semantics=("parallel", "parallel", "arbitrary")),
        )(q, k, v)

    def run_bwd(q, k, v, lse, do):
        def body(q_ref, k_ref, v_ref, lse_ref, do_ref, dq_ref, dk_ref, dv_ref):
            g, i = pl.program_id(1), pl.program_id(2)

            @pl.when((g == 0) & (i == 0))
            def _():
                dk_ref[...] = jnp.zeros_like(dk_ref)
                dv_ref[...] = jnp.zeros_like(dv_ref)

            def run(rows):
                qb, kb, vb, dob = q_ref[...], k_ref[rows, :], v_ref[rows, :], do_ref[...]
                s = lax.dot_general(qb, kb, nt_dims, preferred_element_type=f32)
                p = jnp.exp(s * scale - lse_ref[...])
                dp = lax.dot_general(dob, vb, nt_dims, preferred_element_type=f32)
                ds = (p * (dp - jnp.sum(p * dp, axis=-1, keepdims=True)) * scale).astype(bf16)
                dq_ref[...] = jnp.dot(ds, kb, preferred_element_type=f32).astype(dq_ref.dtype)
                dk_ref[rows, :] += lax.dot_general(ds, qb, tn_dims, preferred_element_type=f32)
                dv_ref[rows, :] += lax.dot_general(p.astype(bf16), dob, tn_dims, preferred_element_type=f32)

            on_keys(i, run)

        return pl.pallas_call(
            body, name=name + "_bwd", grid=grid, in_specs=[q_spec, kv_spec, kv_spec, lse_spec, q_spec],
            out_specs=[q_spec, kv_spec, kv_spec],
            out_shape=[jax.ShapeDtypeStruct(q.shape, bf16), jax.ShapeDtypeStruct(k.shape, f32), jax.ShapeDtypeStruct(v.shape, f32)],
            compiler_params=pltpu.CompilerParams(dimension_semantics=("parallel", "arbitrary", "arbitrary")),
        )(q, k, v, lse, do)

    @jax.custom_vjp
    def op(q, k, v):
        return run_fwd(q, k, v)[0]

    def fwd(q, k, v):
        o, lse = run_fwd(q, k, v)
        return o, (q, k, v, lse)

    def bwd(res, do):
        dq, dk, dv = run_bwd(*res, do)
        return dq, dk.astype(bf16), dv.astype(bf16)

    op.defvjp(fwd, bwd)
    return op(q, k, v)


def _bdot(a, b, ca, cb):
    fa, fb = 3 - ca, 3 - cb

    def dot(x, y, cx, cy):
        return lax.dot_general(x.astype(bf16), y.astype(bf16), (((cx,), (cy,)), ((0,), (0,))), preferred_element_type=f32)

    @jax.custom_vjp
    def op(a, b):
        return dot(a, b, ca, cb)

    def fwd(a, b):
        return op(a, b), (a, b)

    def bwd(res, ct):
        a, b = res
        da = dot(ct, b, 2, fb) if ca == 2 else dot(b, ct, fb, 2)
        db = dot(a, ct, fa, 1) if cb == 1 else dot(ct, a, 1, fa)
        return da, db

    op.defvjp(fwd, bwd)
    return op(a, b)


def _chunk_cumsum(x, rev):
    def impl(x, rev):
        n = x.shape[0]
        pos = lax.broadcasted_iota(jnp.int32, x.shape, 0) % HG_CHUNK
        s = 1
        while s < HG_CHUNK:
            if rev:
                x = x + jnp.where(pos < HG_CHUNK - s, pltpu.roll(x, n - s, 0), 0.0)
            else:
                x = x + jnp.where(pos >= s, pltpu.roll(x, s, 0), 0.0)
            s *= 2
        return x

    @jax.custom_vjp
    def op(x):
        return impl(x, rev)

    op.defvjp(lambda x: (op(x), None), lambda _, ct: (impl(ct, not rev),))
    return op(x)


def _hg_group(St, hq, hf, hi, lb, *, rev):
    G, C = HG_GROUP, HG_CHUNK
    R = G * C
    q = _silu(hq)
    f = lb + (1.0 - lb) * jax.nn.sigmoid(hf)
    logf = jnp.log(jnp.maximum(f, F_MIN))
    kk = (1.0 - lb) * jax.nn.sigmoid(-hf)
    b3 = _chunk_cumsum(logf, rev).reshape(G, C, HEAD)
    q3, k3, v3 = q.reshape(G, C, HEAD), kk.reshape(G, C, HEAD), hi.reshape(G, C, HEAD)
    btot = jnp.sum(logf.reshape(G, C, HEAD), axis=1)
    tt = lax.broadcasted_iota(jnp.int32, (G, C, C, HEAD), 1)
    ss = lax.broadcasted_iota(jnp.int32, (G, C, C, HEAD), 2)
    mask = (ss >= tt) if rev else (ss <= tt)
    diff = b3[:, :, None, :] - b3[:, None, :, :]
    dec = jnp.where(mask, jnp.exp(jnp.where(mask, diff, 0.0)), 0.0)
    scores = jnp.sum(q3[:, :, None, :] * k3[:, None, :, :] * dec, axis=-1)
    o_intra = _bdot(scores, v3, 2, 1)
    q_dec = q3 * jnp.exp(b3)
    k_dec = k3 * jnp.exp(btot[:, None, :] - b3)
    kvt = _bdot(v3, k_dec, 1, 1)
    dl = jnp.exp(btot)
    states = [None] * G
    for g in (range(G - 1, -1, -1) if rev else range(G)):
        states[g] = St
        St = St * dl[g:g + 1, :] + kvt[g]
    o_inter = _bdot(q_dec, jnp.stack(states), 2, 2)
    return St, (o_intra + o_inter).reshape(R, HEAD)


def hgrn(hq, hf, hi, lb, T, rev, name):
    NT, W = hq.shape
    R = HG_GROUP * HG_CHUNK
    n_lat, n_ctx = T // R, (NT - T) // R
    nG = n_lat + n_ctx

    def group_of(j):
        if rev:
            return jnp.where(j < n_ctx, nG - 1 - j, n_lat - 1 - (j - n_ctx))
        return jnp.where(j < n_ctx, n_lat + j, j - n_ctx)

    def rows_of(j):
        return pl.ds(pl.multiple_of(group_of(j) * R, R), R)

    col_spec = pl.BlockSpec((NT, HEAD), lambda h: (0, h))
    lb_spec = pl.BlockSpec((1, HEAD), lambda h: (0, h))
    st_spec = pl.BlockSpec((None, nG, HEAD, HEAD), lambda h: (h, 0, 0, 0))
    sem = pltpu.CompilerParams(dimension_semantics=("parallel",))

    def run_fwd(hq, hf, hi, lb):
        def body(hq_ref, hf_ref, hi_ref, lb_ref, o_ref, st_ref):
            def step(j, St):
                st_ref[j] = St
                rows = rows_of(j)
                St, o = _hg_group(St, hq_ref[rows, :], hf_ref[rows, :], hi_ref[rows, :], lb_ref[...], rev=rev)
                o_ref[rows, :] = o
                return St

            lax.fori_loop(0, nG, step, jnp.zeros((HEAD, HEAD), f32))

        return pl.pallas_call(
            body, name=name + "_fwd", grid=(W // HEAD,), in_specs=[col_spec, col_spec, col_spec, lb_spec],
            out_specs=[col_spec, st_spec],
            out_shape=[jax.ShapeDtypeStruct((NT, W), f32), jax.ShapeDtypeStruct((W // HEAD, nG, HEAD, HEAD), f32)],
            compiler_params=sem,
        )(hq, hf, hi, lb)

    def run_bwd(hq, hf, hi, lb, st, do):
        def body(hq_ref, hf_ref, hi_ref, lb_ref, st_ref, do_ref, dq_ref, df_ref, di_ref, dlb_ref):
            def step(jj, carry):
                dS, dlb = carry
                j = nG - 1 - jj
                rows = rows_of(j)
                _, vjp = jax.vjp(functools.partial(_hg_group, rev=rev), st_ref[j], hq_ref[rows, :], hf_ref[rows, :],
                                 hi_ref[rows, :], lb_ref[...])
                dS, dq, df, di, dl = vjp((dS, do_ref[rows, :]))
                dq_ref[rows, :] = dq
                df_ref[rows, :] = df
                di_ref[rows, :] = di
                return dS, dlb + dl

            _, dlb = lax.fori_loop(0, nG, step, (jnp.zeros((HEAD, HEAD), f32), jnp.zeros((1, HEAD), f32)))
            dlb_ref[...] = dlb

        return pl.pallas_call(
            body, name=name + "_bwd", grid=(W // HEAD,),
            in_specs=[col_spec, col_spec, col_spec, lb_spec, st_spec, col_spec],
            out_specs=[col_spec, col_spec, col_spec, lb_spec],
            out_shape=[jax.ShapeDtypeStruct((NT, W), f32)] * 3 + [jax.ShapeDtypeStruct((1, W), f32)],
            compiler_params=sem,
        )(hq, hf, hi, lb, st, do)

    @jax.custom_vjp
    def op(hq, hf, hi, lb):
        return run_fwd(hq, hf, hi, lb)[0]

    def fwd(hq, hf, hi, lb):
        o, st = run_fwd(hq, hf, hi, lb)
        return o, (hq, hf, hi, lb, st)

    def bwd(res, do):
        return tuple(run_bwd(*res, do))

    op.defvjp(fwd, bwd)
    return op(hq, hf, hi, lb)


def lower_bounds(params):
    n = len(params)

    def f(*a):
        m = functools.reduce(jnp.maximum, a)
        e = [jnp.exp(x - m) for x in a]
        s = functools.reduce(lambda u, v: u + v, e)
        p = [x / s for x in e]
        out, run = [], jnp.zeros_like(p[0])
        for l in range(n):
            out.append(run)
            run = run + p[l]
        return tuple(out[l] + p[l] - p[0] for l in range(n))

    shape = [jax.ShapeDtypeStruct(params[0].shape, f32)] * n

    @jax.custom_vjp
    def op(*a):
        def body(*refs):
            for o_ref, r in zip(refs[n:], f(*[x[...] for x in refs[:n]])):
                o_ref[...] = r
        return tuple(pl.pallas_call(body, name="lower_bounds_fwd", out_shape=shape)(*a))

    def fwd(*a):
        return op(*a), a

    def bwd(a, cts):
        def body(*refs):
            _, vjp = jax.vjp(f, *[x[...] for x in refs[:n]])
            for o_ref, r in zip(refs[2 * n:], vjp(tuple(x[...] for x in refs[n:2 * n]))):
                o_ref[...] = r
        return tuple(pl.pallas_call(body, name="lower_bounds_bwd", out_shape=shape)(*a, *cts))

    op.defvjp(fwd, bwd)
    return op(*params)


def _shift_rows(x, d, T):
    n = x.shape[0]
    t = lax.broadcasted_iota(jnp.int32, x.shape, 0)
    y = pltpu.roll(x, d % n, 0)
    edge = ((t == 0) | (t == T)) if d == 1 else ((t == T - 1) | (t == n - 1))
    return jnp.where(edge, 0.0, y)


def _conv(x, w, b, T):
    return b + w[0:1] * _shift_rows(x, 1, T) + w[1:2] * x + w[2:3] * _shift_rows(x, -1, T)


def convact(up, cw, cb, T, name, tc=128):
    NT, F2 = up.shape
    F = F2 // 2
    tc = _pick(F, (tc, 128))
    nf = F // tc
    g_spec = lambda r: pl.BlockSpec((r, tc), lambda j: (0, j))
    v_spec = lambda r: pl.BlockSpec((r, tc), lambda j: (0, j + nf))
    sem = pltpu.CompilerParams(dimension_semantics=("parallel",))

    def run_fwd(up, cw, cb):
        def body(xg_ref, xv_ref, wg_ref, wv_ref, bg_ref, bv_ref, o_ref):
            yg = _conv(xg_ref[...].astype(f32), wg_ref[...], bg_ref[...], T)
            yv = _conv(xv_ref[...].astype(f32), wv_ref[...], bv_ref[...], T)
            o_ref[...] = (_silu(yg) * yv).astype(o_ref.dtype)

        return pl.pallas_call(
            body, name=name + "_fwd", grid=(nf,),
            in_specs=[g_spec(NT), v_spec(NT), g_spec(3), v_spec(3), g_spec(1), v_spec(1)], out_specs=g_spec(NT),
            out_shape=jax.ShapeDtypeStruct((NT, F), bf16), compiler_params=sem,
        )(up, up, cw, cw, cb, cb)

    def run_bwd(up, cw, cb, dact):
        def body(xg_ref, xv_ref, wg_ref, wv_ref, bg_ref, bv_ref, da_ref, dxg_ref, dxv_ref, dwg_ref, dwv_ref, dbg_ref, dbv_ref):
            xg, xv, wg, wv = xg_ref[...].astype(f32), xv_ref[...].astype(f32), wg_ref[...], wv_ref[...]
            yg = _conv(xg, wg, bg_ref[...], T)
            yv = _conv(xv, wv, bv_ref[...], T)
            da = da_ref[...].astype(f32)
            sg = jax.nn.sigmoid(yg)
            dyv = da * yg * sg
            dyg = da * yv * sg * (1.0 + yg * (1.0 - sg))
            for x, w, dy, dx_ref, dw_ref, db_ref in ((xg, wg, dyg, dxg_ref, dwg_ref, dbg_ref), (xv, wv, dyv, dxv_ref, dwv_ref, dbv_ref)):
                dx_ref[...] = (w[0:1] * _shift_rows(dy, -1, T) + w[1:2] * dy + w[2:3] * _shift_rows(dy, 1, T)).astype(dx_ref.dtype)
                dw_ref[...] = jnp.concatenate([
                    jnp.sum(dy * _shift_rows(x, 1, T), axis=0, keepdims=True),
                    jnp.sum(dy * x, axis=0, keepdims=True),
                    jnp.sum(dy * _shift_rows(x, -1, T), axis=0, keepdims=True)], axis=0)
                db_ref[...] = jnp.sum(dy, axis=0, keepdims=True)

        return pl.pallas_call(
            body, name=name + "_bwd", grid=(nf,),
            in_specs=[g_spec(NT), v_spec(NT), g_spec(3), v_spec(3), g_spec(1), v_spec(1), g_spec(NT)],
            out_specs=[g_spec(NT), g_spec(NT), g_spec(3), g_spec(3), g_spec(1), g_spec(1)],
            out_shape=[jax.ShapeDtypeStruct((NT, F), up.dtype)] * 2 + [jax.ShapeDtypeStruct((3, F), f32)] * 2 + [jax.ShapeDtypeStruct((1, F), f32)] * 2,
            compiler_params=sem,
        )(up, up, cw, cw, cb, cb, dact)

    @jax.custom_vjp
    def op(up, cw, cb):
        return run_fwd(up, cw, cb)

    def fwd(up, cw, cb):
        return op(up, cw, cb), (up, cw, cb)

    def bwd(res, dact):
        dxg, dxv, dwg, dwv, dbg, dbv = run_bwd(*res, dact)
        return (jnp.concatenate([dxg, dxv], axis=1), jnp.concatenate([dwg, dwv], axis=1), jnp.concatenate([dbg, dbv], axis=1))

    op.defvjp(fwd, bwd)
    return op(up, cw, cb)


def _peers():
    x, y, c = lax.axis_index("x"), lax.axis_index("y"), lax.axis_index("c")
    return (x, y, c), [(x, y, 1 - c), (1 - x, y, c), (x, 1 - y, c), (1 - x, 1 - y, c),
                       (1 - x, y, 1 - c), (x, 1 - y, 1 - c), (1 - x, 1 - y, 1 - c)]


def _index(dev):
    return 4 * dev[0] + 2 * dev[1] + dev[2]


def allgather_small(x, name):
    m, n = x.shape

    def body(x_ref, out_ref, send_sems, recv_sems, local_sem):
        me, peers = _peers()

        def rows(dev):
            return out_ref.at[pl.ds(pl.multiple_of(_index(dev) * m, 8), m), :]

        mine = pltpu.make_async_copy(x_ref, rows(me), local_sem)
        mine.start()
        sends = [pltpu.make_async_remote_copy(src_ref=x_ref, dst_ref=rows(me), send_sem=send_sems.at[k], recv_sem=recv_sems.at[k],
                                              device_id=p, device_id_type=MESH) for k, p in enumerate(peers)]
        for cp in sends:
            cp.start()
        for k, p in enumerate(peers):
            pltpu.make_async_remote_copy(src_ref=x_ref, dst_ref=rows(p), send_sem=send_sems.at[k], recv_sem=recv_sems.at[k],
                                         device_id=p, device_id_type=MESH).wait_recv()
        for cp in sends:
            cp.wait_send()
        mine.wait()

    return pl.pallas_call(
        body, name=name, out_shape=jax.ShapeDtypeStruct((N_DEV * m, n), x.dtype),
        in_specs=[pl.BlockSpec(memory_space=pltpu.VMEM)], out_specs=pl.BlockSpec(memory_space=pltpu.VMEM),
        scratch_shapes=[pltpu.SemaphoreType.DMA((7,)), pltpu.SemaphoreType.DMA((7,)), pltpu.SemaphoreType.DMA],
    )(x)


HBM_SPEC = pl.BlockSpec(memory_space=pltpu.HBM)
SEM_SPEC = pl.BlockSpec(memory_space=pltpu.SEMAPHORE)
SPLIT_PARAMS = dict(compiler_params=pltpu.CompilerParams(has_side_effects=pltpu.SideEffectType.DATAFLOW_SIDE_EFFECTING))
N_PEERS = N_DEV - 1


def _part(ref, kind, j, width):
    if kind == "col":
        return ref.at[:, pl.ds(pl.multiple_of(j * width, 128), width)]
    return ref.at[pl.ds(pl.multiple_of(j * width, 8), width), :]


def _in_hbm(a):
    return pltpu.with_memory_space_constraint(a, pltpu.HBM)


def place_own(shards, kinds):
    n = len(shards)
    widths = [s.shape[1] if k == "col" else s.shape[0] for s, k in zip(shards, kinds)]
    whole = [jax.ShapeDtypeStruct((s.shape[0], N_DEV * s.shape[1]) if k == "col" else (N_DEV * s.shape[0], s.shape[1]), s.dtype)
             for s, k in zip(shards, kinds)]

    def body(*refs):
        srcs, outs, sems = refs[:n], refs[n:2 * n], refs[2 * n]
        me, _ = _peers()
        copies = [pltpu.make_async_copy(srcs[a], _part(outs[a], kinds[a], _index(me), widths[a]), sems.at[a]) for a in range(n)]
        for cp in copies:
            cp.start()
        for cp in copies:
            cp.wait()

    any_spec = pl.BlockSpec(memory_space=pl.ANY)
    return pl.pallas_call(body, name="place_own", out_shape=whole, in_specs=[any_spec] * n, out_specs=[any_spec] * n,
                          scratch_shapes=[pltpu.SemaphoreType.DMA((n,))])(*shards)


def gather_start(shards, lands, kinds):
    n = len(shards)
    widths = [s.shape[1] if k == "col" else s.shape[0] for s, k in zip(shards, kinds)]

    def body(*refs):
        srcs, lnds, send_sems, recv_sems, token = refs[:n], refs[n:2 * n], refs[2 * n], refs[2 * n + 1], refs[-1]
        me, peers = _peers()
        for a in range(n):
            for k, p in enumerate(peers):
                pltpu.make_async_remote_copy(
                    src_ref=srcs[a], dst_ref=_part(lnds[a], kinds[a], _index(me), widths[a]),
                    send_sem=send_sems.at[N_PEERS * a + k], recv_sem=recv_sems.at[N_PEERS * a + k],
                    device_id=p, device_id_type=MESH).start()
        token[...] = jnp.zeros_like(token)

    res = pl.pallas_call(
        body, name="gather_start",
        out_shape=(pltpu.SemaphoreType.DMA((N_PEERS * n,)), pltpu.SemaphoreType.DMA((N_PEERS * n,)),
                   *[pltpu.HBM(a.shape, a.dtype) for a in (*shards, *lands)], jax.ShapeDtypeStruct((8, 128), f32)),
        in_specs=[HBM_SPEC] * (2 * n), out_specs=(SEM_SPEC, SEM_SPEC, *[HBM_SPEC] * (2 * n), pl.BlockSpec(memory_space=pltpu.VMEM)),
        input_output_aliases={i: 2 + i for i in range(2 * n)}, **SPLIT_PARAMS,
    )(*[_in_hbm(a) for a in (*shards, *lands)])
    return res[0], res[1], res[2:2 + n], res[2 + n:2 + 2 * n], res[-1]


def gather_wait(a, shard, land, kind, send_sems, recv_sems, after, name):
    width = shard.shape[1] if kind == "col" else shard.shape[0]

    def body(src_ref, land_ref, send_ref, recv_ref, after_ref, src_out, land_out):
        _, peers = _peers()
        for k, p in enumerate(peers):
            cp = pltpu.make_async_remote_copy(
                src_ref=src_ref, dst_ref=_part(land_ref, kind, _index(p), width),
                send_sem=send_ref.at[N_PEERS * a + k], recv_sem=recv_ref.at[N_PEERS * a + k], device_id=p, device_id_type=MESH)
            cp.wait_send()
            cp.wait_recv()

    return pl.pallas_call(
        body, name=name, out_shape=(pltpu.HBM(shard.shape, shard.dtype), pltpu.HBM(land.shape, land.dtype)),
        in_specs=(HBM_SPEC, HBM_SPEC, SEM_SPEC, SEM_SPEC, pl.BlockSpec(memory_space=pl.ANY)), out_specs=(HBM_SPEC, HBM_SPEC),
        input_output_aliases={0: 0, 1: 1}, **SPLIT_PARAMS,
    )(shard, land, send_sems, recv_sems, after)[1]


def scatter_start(full, slots, kind, name):
    width = slots.shape[2] if kind == "col" else slots.shape[1]

    def body(full_ref, slots_ref, send_sems, recv_sems, full_out, slots_out, token):
        me, peers = _peers()
        for k, p in enumerate(peers):
            pltpu.make_async_remote_copy(
                src_ref=_part(full_ref, kind, _index(p), width), dst_ref=slots_ref.at[_index(me)],
                send_sem=send_sems.at[k], recv_sem=recv_sems.at[k], device_id=p, device_id_type=MESH).start()
        token[...] = jnp.zeros_like(token)

    return pl.pallas_call(
        body, name=name,
        out_shape=(pltpu.SemaphoreType.DMA((N_PEERS,)), pltpu.SemaphoreType.DMA((N_PEERS,)), pltpu.HBM(full.shape, full.dtype),
                   pltpu.HBM(slots.shape, slots.dtype), jax.ShapeDtypeStruct((8, 128), f32)),
        in_specs=(HBM_SPEC, HBM_SPEC), out_specs=(SEM_SPEC, SEM_SPEC, HBM_SPEC, HBM_SPEC, pl.BlockSpec(memory_space=pltpu.VMEM)),
        input_output_aliases={0: 2, 1: 3}, **SPLIT_PARAMS,
    )(_in_hbm(full), _in_hbm(slots))


def scatter_wait(full, slots, kind, send_sems, recv_sems, after, name):
    width = slots.shape[2] if kind == "col" else slots.shape[1]

    def body(full_ref, slots_ref, send_ref, recv_ref, after_ref, full_out, slots_out):
        me, peers = _peers()
        for k, p in enumerate(peers):
            cp = pltpu.make_async_remote_copy(
                src_ref=_part(full_ref, kind, _index(p), width), dst_ref=slots_ref.at[_index(p)],
                send_sem=send_ref.at[k], recv_sem=recv_ref.at[k], device_id=p, device_id_type=MESH)
            cp.wait_send()
            cp.wait_recv()

    return pl.pallas_call(
        body, name=name, out_shape=(pltpu.HBM(full.shape, full.dtype), pltpu.HBM(slots.shape, slots.dtype)),
        in_specs=(HBM_SPEC, HBM_SPEC, SEM_SPEC, SEM_SPEC, pl.BlockSpec(memory_space=pl.ANY)), out_specs=(HBM_SPEC, HBM_SPEC),
        input_output_aliases={0: 0, 1: 1}, **SPLIT_PARAMS,
    )(full, slots, send_sems, recv_sems, after)[1]


def sum_slots(x, name):
    _, R, C = x.shape
    tr = _pick(R, (256, 128, 64, 32, 16, 8))

    def body(x_ref, o_ref):
        acc = x_ref[0].astype(f32)
        for d in range(1, N_DEV):
            acc = acc + x_ref[d].astype(f32)
        o_ref[...] = acc

    return pl.pallas_call(
        body, name=name, grid=(R // tr,), in_specs=[pl.BlockSpec((N_DEV, tr, C), lambda i: (0, i, 0))],
        out_specs=pl.BlockSpec((tr, C), lambda i: (i, 0)), out_shape=jax.ShapeDtypeStruct((R, C), f32),
        compiler_params=pltpu.CompilerParams(dimension_semantics=("parallel",)),
    )(x)


BIG = (("w_in", "col"), ("w_out", "row"), ("w_up", "col"), ("w_down", "row"))


class BigWeights:
    def __init__(self, given, me):
        self.me = me
        self.kinds = dict(BIG)
        depth = given["w_in"].shape[0]
        self.keys = [(l, n) for l in range(depth) for n, _ in BIG]
        kinds = [self.kinds[n] for _, n in self.keys]
        shards = [given[n][l].astype(bf16) for l, n in self.keys]
        lands = place_own(shards, kinds)
        self.send, self.recv, self.shards, self.lands, _ = gather_start(shards, lands, kinds)
        self.pending = {}

    def get(self, l, n, after):
        a = self.keys.index((l, n))
        return gather_wait(a, self.shards[a], self.lands[a], self.kinds[n], self.send, self.recv, after, "gather_wait_%s%d" % (n, l))

    def start_scatter(self, l, n, dw):
        kind = self.kinds[n]
        R, C = dw.shape
        r, c = (R, C // N_DEV) if kind == "col" else (R // N_DEV, C)
        own = lax.dynamic_slice(dw, (0, self.me * c) if kind == "col" else (self.me * r, 0), (r, c))
        slots = lax.dynamic_update_slice(lax.empty((N_DEV, r, c), bf16), own[None], (self.me, 0, 0))
        send, recv, full, slots, token = scatter_start(dw, slots, kind, "scatter_start_%s%d" % (n, l))
        self.pending[(l, n)] = (full, slots, send, recv)
        return token

    def grad(self, n, after):
        per_layer = []
        for l in sorted(k[0] for k in self.keys if k[1] == n):
            full, slots, send, recv = self.pending[(l, n)]
            slots = scatter_wait(full, slots, self.kinds[n], send, recv, after, "scatter_wait_%s%d" % (n, l))
            per_layer.append(sum_slots(slots, "sum_grads_%s%d" % (n, l)))
        return jnp.stack(per_layer)


def allreduce_small(vals, name):
    flat = jnp.concatenate([v.reshape(-1) for v in vals])
    n = flat.shape[0]
    cols = 1024
    m = -(-n // (cols * 8)) * 8
    packed = jnp.pad(flat, (0, m * cols - n)).reshape(m, cols)
    total = sum_slots(allgather_small(packed, name).reshape(N_DEV, m, cols), name + "_sum").reshape(-1)
    out, off = [], 0
    for v in vals:
        out.append(total[off:off + v.size].reshape(v.shape))
        off += v.size
    return out


def ada_mod(c_all, c_ctx, w_ada, b_ada):
    L, D, S = w_ada.shape
    me = _my_index()

    def stacked(c_ctx):
        return jnp.concatenate([c_all, jnp.broadcast_to(c_ctx, (N_DEV, D))], axis=0)

    ts = _pick(S, (512, 384, 256, 128, 64))
    w_spec = pl.BlockSpec((None, D, ts), lambda l, j: (l, 0, j))
    c_spec = pl.BlockSpec((16, D), lambda l, j: (0, 0))
    p_spec = pl.BlockSpec((None, 16, ts), lambda l, j: (l, 0, j))

    def run_fwd(cin, w_ada):
        def body(c_ref, w_ref, o_ref):
            o_ref[...] = jnp.dot(_silu(c_ref[...]).astype(bf16), w_ref[...].astype(bf16), preferred_element_type=f32)

        return pl.pallas_call(
            body, name="ada_fwd", grid=(L, S // ts), in_specs=[c_spec, w_spec], out_specs=p_spec,
            out_shape=jax.ShapeDtypeStruct((L, 16, S), f32),
            compiler_params=pltpu.CompilerParams(dimension_semantics=("parallel", "parallel")),
        )(cin, w_ada)

    def run_bwd(cin, w_ada, dm):
        def body(c_ref, w_ref, dm_ref, gw_ref, dc_ref):
            first = (pl.program_id(0) == 0) & (pl.program_id(1) == 0)
            cv = c_ref[...]
            sg = jax.nn.sigmoid(cv)
            dmv = dm_ref[...].astype(bf16)
            gw_ref[...] = lax.dot_general((cv * sg).astype(bf16), dmv, (((0,), (0,)), ((), ())), preferred_element_type=f32)
            ds = lax.dot_general(dmv, w_ref[...].astype(bf16), (((1,), (1,)), ((), ())), preferred_element_type=f32)
            dc = ds * sg * (1.0 + cv * (1.0 - sg))

            @pl.when(first)
            def _():
                dc_ref[...] = dc

            @pl.when(jnp.logical_not(first))
            def _():
                dc_ref[...] += dc

        return pl.pallas_call(
            body, name="ada_bwd", grid=(L, S // ts), in_specs=[c_spec, w_spec, p_spec], out_specs=[w_spec, c_spec],
            out_shape=[jax.ShapeDtypeStruct((L, D, S), f32), jax.ShapeDtypeStruct((16, D), f32)],
            compiler_params=pltpu.CompilerParams(dimension_semantics=("arbitrary", "arbitrary")),
        )(cin, w_ada, dm)

    def bias_grad(dm_full):
        def body(x_ref, o_ref):
            o_ref[...] = jnp.sum(x_ref[...], axis=0, keepdims=True)

        return pl.pallas_call(
            body, name="ada_bias_grad", grid=(L,), in_specs=[pl.BlockSpec((None, 16, 6 * D), lambda l: (l, 0, 0))],
            out_specs=pl.BlockSpec((None, 1, 6 * D), lambda l: (l, 0, 0)), out_shape=jax.ShapeDtypeStruct((L, 1, 6 * D), f32),
        )(dm_full).reshape(L, 6 * D)

    @jax.custom_vjp
    def op(c_ctx, w_ada, b_ada):
        prod = run_fwd(stacked(c_ctx), w_ada)
        allp = allgather_small(prod.reshape(L * 16, S), "ada_gather").reshape(N_DEV, L, 16, S)
        allp = allp.transpose(1, 2, 0, 3).reshape(L, 16, N_DEV * S)
        mine = lax.dynamic_index_in_dim(allp, me, axis=1, keepdims=False) + b_ada
        ctx = allp[:, N_DEV] + b_ada
        return jnp.stack([mine, ctx], axis=1).reshape(L, 2, 6, D)

    def fwd(c_ctx, w_ada, b_ada):
        return op(c_ctx, w_ada, b_ada), (c_ctx, w_ada)

    def bwd(res, dmod):
        c_ctx, w_ada = res
        dm = dmod.reshape(L * 2, 6 * D)
        gathered = allgather_small(jnp.pad(dm, ((0, (-2 * L) % 8), (0, 0))), "ada_grad_gather")
        gathered = gathered.reshape(N_DEV, -1, 6 * D)[:, :2 * L].reshape(N_DEV, L, 2, 6 * D)
        dm_full = gathered.transpose(1, 2, 0, 3).reshape(L, 16, 6 * D)
        dm_mine = lax.dynamic_slice_in_dim(dm_full, me * S, S, axis=2)
        gw, dc = run_bwd(stacked(c_ctx), w_ada, dm_mine)
        d_cctx = jnp.sum(dc[N_DEV:], axis=0, keepdims=True)
        return d_cctx, gw, bias_grad(dm_full)

    op.defvjp(fwd, bwd)
    return op(c_ctx, w_ada, b_ada)


def adamw(w, g, m, v, name):
    shape = w.shape
    C = shape[-1]
    R = w.size // C
    tr = _pick(R, (256, 128, 64, 32, 16, 8)) if R * C * 4 > (1 << 20) else R
    c1 = 1.0 / (1.0 - ADAM_B1 ** ADAM_STEP)
    c2 = 1.0 / (1.0 - ADAM_B2 ** ADAM_STEP)

    def body(w_ref, g_ref, m_ref, v_ref, d_ref, mo_ref, vo_ref):
        gv = g_ref[...]
        mn = ADAM_B1 * m_ref[...] + (1.0 - ADAM_B1) * gv
        vn = ADAM_B2 * v_ref[...] + (1.0 - ADAM_B2) * gv * gv
        d_ref[...] = -ADAM_LR * ((mn * c1) / (jnp.sqrt(vn * c2) + ADAM_EPS) + ADAM_WD * w_ref[...])
        mo_ref[...] = mn
        vo_ref[...] = vn

    spec = pl.BlockSpec((tr, C), lambda i: (i, 0))
    res = pl.pallas_call(
        body, name=name, grid=(R // tr,), in_specs=[spec] * 4, out_specs=[spec] * 3,
        out_shape=[jax.ShapeDtypeStruct((R, C), f32)] * 3,
        compiler_params=pltpu.CompilerParams(dimension_semantics=("parallel",)),
    )(*[a.reshape(R, C) for a in (w, g, m, v)])
    return tuple(r.reshape(shape) for r in res)


def _rope_tables(T, L):
    rows = T // GRID_W
    row = jnp.repeat(jnp.arange(rows, dtype=f32), GRID_W)
    col = jnp.tile(jnp.arange(GRID_W, dtype=f32), rows)
    n_freq = HEAD // 4
    inv = ROPE_THETA ** (-jnp.arange(n_freq, dtype=f32) / n_freq)
    ang = jnp.concatenate([row[:, None] * inv, col[:, None] * inv], axis=-1)
    cos = jnp.repeat(jnp.cos(ang), 2, axis=-1)
    sin = jnp.repeat(jnp.sin(ang), 2, axis=-1) * jnp.tile(jnp.array([-1.0, 1.0], f32), HEAD // 2)
    return (jnp.concatenate([cos, jnp.ones((L, HEAD), f32)]), jnp.concatenate([sin, jnp.zeros((L, HEAD), f32)]))


def _sel(mod, l, idx):
    return jnp.stack([mod[l, :, i] for i in idx], axis=1)


def _row(a, l):
    return a[l][None, :]


def _first_segment(T, ctx):
    def seg(x, mod, p):
        xs = jnp.concatenate([x, ctx], axis=0)
        (h,) = rowwise("modnorm", f_modnorm, [xs], sels=[_sel(mod, 0, (0, 1))], pars=[_row(p["norm1_g"], 0)],
                       outs=[(x.shape[1], bf16)], n_lat=T)
        return xs, h

    return seg


def _mixer_segment(l, T, cosf, sinf):
    row = _row

    def seg(xs, h, mod, lbs, p, w_in, w_out):
        D = xs.shape[1]
        aq, ak, av, hq, hff, hfb, hi, hgt, su, sv = mm(h, w_in, "w_in%d" % l, split=IN_SIZES)
        q, k, v = rowwise("qkprep%d" % l, f_qkprep, [aq, ak, av], consts=[cosf, sinf],
                          pars=[row(p["q_norm_g"], l), row(p["k_norm_g"], l)],
                          outs=[(aq.shape[1], bf16), (ak.shape[1], bf16), (av.shape[1], bf16)], n_lat=T)
        attn = attention(q, k, v, T, "attn%d" % l)
        o_f = hgrn(hq, hff, hi, lbs[0:1], T, False, "hgrn_f%d" % l)
        o_b = hgrn(hq, hfb, hi, lbs[1:2], T, True, "hgrn_b%d" % l)
        (hg,) = rowwise("hgout%d" % l, f_hgout, [o_f, o_b, hgt], pars=[row(p["hg_norm_g"], l)], outs=[(hgt.shape[1], bf16)], n_lat=T)
        (sg,) = rowwise("sgate%d" % l, f_sgate, [su, sv],
                        pars=[row(p["sg_norm_g"], l), p["sg_w"][l]] + [p["sg_b"][l, gi][:, None] for gi in range(SG_GROUPS)],
                        outs=[(su.shape[1], bf16)], n_lat=T)
        mix = jnp.concatenate([attn, hg, sg], axis=1)
        y = mm(mix, w_out, "w_out%d" % l)
        return rowwise("resid_a%d" % l, f_resid_modnorm, [xs, y], sels=[_sel(mod, l, (2, 3, 4))], pars=[row(p["norm2_g"], l)],
                       outs=[(D, f32), (D, bf16)], n_lat=T)

    return seg


def _ffn_segment(l, depth, T, tgt):
    row = _row

    def seg(xs, h2, mod, p, w_up, w_down):
        D = xs.shape[1]
        up = mm(h2, w_up, "w_up%d" % l, out_dtype=bf16)
        act = convact(up, p["conv_w"][l], row(p["conv_b"], l), T, "convact%d" % l)
        z = mm(act, w_down, "w_down%d" % l)
        if l + 1 < depth:
            return rowwise("resid_b%d" % l, f_resid_modnorm, [xs, z],
                           sels=[jnp.concatenate([_sel(mod, l, (5,)), _sel(mod, l + 1, (0, 1))], axis=1)],
                           pars=[row(p["norm1_g"], l + 1)], outs=[(D, f32), (D, bf16)], n_lat=T)
        return rowwise("resid_final", f_resid_final, [xs[:T], z[:T]], consts=[tgt], sels=[_sel(mod, l, (5,))[0:1]],
                       pars=[p["final_norm_g"][None, :]], outs=[(1, f32)])

    return seg


SEGMENT_PARAMS = ("norm1_g", "q_norm_g", "k_norm_g", "hg_norm_g", "sg_norm_g", "sg_w", "sg_b", "norm2_g", "conv_w", "conv_b",
                  "final_norm_g")


def _loss_and_grads(p, big, x, ctx, c_all, tgt, cosf, sinf):
    T = x.shape[0]
    depth = p["norm1_g"].shape[0]
    add = lambda a, b: jax.tree.map(jnp.add, a, b)
    small = {n: p[n] for n in SEGMENT_PARAMS}

    mod, vjp_mod = jax.vjp(lambda cc, wa, ba: ada_mod(c_all, cc, wa, ba), p["c_ctx"], p["w_ada"], p["b_ada"])
    lbs, vjp_lbs = jax.vjp(lambda hg: lower_bounds([hg[:, l] for l in range(depth)]), p["hg_lower_bounds"])
    (xs, h), vjp_first = jax.vjp(_first_segment(T, ctx), x, mod, small)
    vjps = []
    for l in range(depth):
        w_in, w_out = big.get(l, "w_in", h), big.get(l, "w_out", h)
        (xs, h2), vj = jax.vjp(_mixer_segment(l, T, cosf, sinf), xs, h, mod, lbs[l], small, w_in, w_out)
        vjps.append(vj)
        w_up, w_down = big.get(l, "w_up", h2), big.get(l, "w_down", h2)
        out, vj = jax.vjp(_ffn_segment(l, depth, T, tgt), xs, h2, mod, small, w_up, w_down)
        vjps.append(vj)
        if l + 1 < depth:
            xs, h = out
    (rowloss,) = out
    loss = 0.5 * jnp.sum(rowloss)

    ct = (jnp.full(rowloss.shape, 0.5, f32),)
    d_mod, d_small, d_lbs = jnp.zeros_like(mod), jax.tree.map(jnp.zeros_like, small), [None] * depth
    for l in range(depth - 1, -1, -1):
        dxs, dh2, dm, ds, d_up, d_down = vjps[2 * l + 1](ct)
        tokens = (big.start_scatter(l, "w_up", d_up), big.start_scatter(l, "w_down", d_down))
        dxs, dh2, _ = lax.optimization_barrier((dxs, dh2, tokens))
        d_mod, d_small = d_mod + dm, add(d_small, ds)
        dxs, dh, dm, d_lbs[l], ds, d_in, d_out = vjps[2 * l]((dxs, dh2))
        tokens = (big.start_scatter(l, "w_in", d_in), big.start_scatter(l, "w_out", d_out))
        dxs, dh, _ = lax.optimization_barrier((dxs, dh, tokens))
        d_mod, d_small = d_mod + dm, add(d_small, ds)
        ct = (dxs, dh)
    dx, dm, ds = vjp_first(ct)
    d_cc, d_wada, d_bada = vjp_mod(d_mod + dm)
    (d_hg,) = vjp_lbs(tuple(d_lbs))
    grads = dict(add(d_small, ds), c_ctx=d_cc, w_ada=d_wada, b_ada=d_bada, hg_lower_bounds=d_hg)
    return loss, grads, dx


def kernel(x, c, ctx, c_ctx, w_ada, b_ada, norm1_g, w_in, q_norm_g, k_norm_g, hg_lower_bounds, hg_norm_g, sg_norm_g, sg_w, sg_b, w_out, norm2_g, w_up, conv_w, conv_b, w_down, final_norm_g, loss_target, m_c_ctx, m_w_ada, m_b_ada, m_norm1_g, m_w_in, m_q_norm_g, m_k_norm_g, m_hg_lower_bounds, m_hg_norm_g, m_sg_norm_g, m_sg_w, m_sg_b, m_w_out, m_norm2_g, m_w_up, m_conv_w, m_conv_b, m_w_down, m_final_norm_g, v_c_ctx, v_w_ada, v_b_ada, v_norm1_g, v_w_in, v_q_norm_g, v_k_norm_g, v_hg_lower_bounds, v_hg_norm_g, v_sg_norm_g, v_sg_w, v_sg_b, v_w_out, v_norm2_g, v_w_up, v_conv_w, v_conv_b, v_w_down, v_final_norm_g):
    given = dict(c_ctx=c_ctx, w_ada=w_ada, b_ada=b_ada, norm1_g=norm1_g, w_in=w_in, q_norm_g=q_norm_g, k_norm_g=k_norm_g,
                 hg_lower_bounds=hg_lower_bounds, hg_norm_g=hg_norm_g, sg_norm_g=sg_norm_g, sg_w=sg_w, sg_b=sg_b, w_out=w_out,
                 norm2_g=norm2_g, w_up=w_up, conv_w=conv_w, conv_b=conv_b, w_down=w_down, final_norm_g=final_norm_g)
    moments_m = dict(c_ctx=m_c_ctx, w_ada=m_w_ada, b_ada=m_b_ada, norm1_g=m_norm1_g, w_in=m_w_in, q_norm_g=m_q_norm_g,
                     k_norm_g=m_k_norm_g, hg_lower_bounds=m_hg_lower_bounds, hg_norm_g=m_hg_norm_g, sg_norm_g=m_sg_norm_g,
                     sg_w=m_sg_w, sg_b=m_sg_b, w_out=m_w_out, norm2_g=m_norm2_g, w_up=m_w_up, conv_w=m_conv_w, conv_b=m_conv_b,
                     w_down=m_w_down, final_norm_g=m_final_norm_g)
    moments_v = dict(c_ctx=v_c_ctx, w_ada=v_w_ada, b_ada=v_b_ada, norm1_g=v_norm1_g, w_in=v_w_in, q_norm_g=v_q_norm_g,
                     k_norm_g=v_k_norm_g, hg_lower_bounds=v_hg_lower_bounds, hg_norm_g=v_hg_norm_g, sg_norm_g=v_sg_norm_g,
                     sg_w=v_sg_w, sg_b=v_sg_b, w_out=v_w_out, norm2_g=v_norm2_g, w_up=v_w_up, conv_w=v_conv_w, conv_b=v_conv_b,
                     w_down=v_w_down, final_norm_g=v_final_norm_g)
    T, D = x.shape[1], x.shape[2]
    L = ctx.shape[1]
    me = _my_index()
    axes = ("x", "y", "c")

    c_all = allgather_small(jnp.pad(c, ((0, 7), (0, 0))), "gather_c").reshape(N_DEV, 8, D)[:, 0]
    depth, hw = hg_lower_bounds.shape[1], hg_lower_bounds.shape[2]
    cw = conv_w.shape[2]
    small = jnp.concatenate([jnp.pad(hg_lower_bounds.reshape(2 * depth, hw), ((0, 0), (0, cw - hw))), conv_w.reshape(3 * depth, cw)], axis=0)
    rows_small = small.shape[0]
    small = allgather_small(jnp.pad(small, ((0, (-rows_small) % 8), (0, 0))), "gather_small").reshape(N_DEV, -1, cw)
    hg_full = small[:, :2 * depth, :hw].reshape(N_DEV, 2, depth, hw).transpose(1, 2, 0, 3).reshape(2, depth, N_DEV * hw)
    cw_full = small[:, 2 * depth:2 * depth + 3 * depth].reshape(N_DEV, depth, 3, cw).transpose(1, 2, 0, 3).reshape(depth, 3, N_DEV * cw)

    big = BigWeights(given, me)
    p = {n: a for n, a in given.items() if n not in dict(BIG)}
    p.update(hg_lower_bounds=hg_full, conv_w=cw_full, c_ctx=c_ctx[None, :])
    cosf, sinf = _rope_tables(T, L)
    loss, gp, gx = _loss_and_grads(p, big, x[0], ctx[0], c_all, loss_target[0], cosf, sinf)
    loss = lax.psum(loss, axes)

    partial = ['c_ctx', 'norm1_g', 'q_norm_g', 'k_norm_g', 'hg_lower_bounds', 'hg_norm_g', 'sg_norm_g', 'sg_w', 'sg_b',
               'norm2_g', 'conv_w', 'conv_b', 'final_norm_g']
    summed = dict(zip(partial, allreduce_small([gp[n] for n in partial], "reduce_small")))
    grads = dict(gp, **summed)
    grads['c_ctx'] = grads['c_ctx'][0]
    grads['hg_lower_bounds'] = lax.dynamic_slice_in_dim(grads['hg_lower_bounds'], me * hw, hw, axis=2)
    grads['conv_w'] = lax.dynamic_slice_in_dim(grads['conv_w'], me * cw, cw, axis=2)

    delta, new_m, new_v = {}, {}, {}
    last = gx
    for n in [w for w in WEIGHTS if w not in dict(BIG)] + ["w_down", "w_up", "w_out", "w_in"]:
        if n in dict(BIG):
            grads[n] = big.grad(n, last)
        delta[n], new_m[n], new_v[n] = adamw(given[n], grads[n], moments_m[n], moments_v[n], "adamw_" + n)
        last = delta[n]
    return (loss, gx[None], *[grads[n] for n in WEIGHTS], *[delta[n] for n in WEIGHTS],
            *[new_m[n] for n in WEIGHTS], *[new_v[n] for n in WEIGHTS])
```

```python
import functools

import jax
import jax.numpy as jnp
from jax import lax
from jax.experimental import pallas as pl
from jax.experimental.pallas import tpu as pltpu

f32 = jnp.float32
bf16 = jnp.bfloat16
HI = lax.Precision.HIGHEST
MESH = pl.DeviceIdType.MESH

EPS = 1e-6
F_MIN = 1e-30
GRID_W = 64
ROPE_THETA = 10000.0
HEAD = 128
ATTN_HEADS, ATTN_KV = 8, 2
ATTN_GROUP = ATTN_HEADS // ATTN_KV
HG_HEADS = 4
SG_GROUPS = 4
SG_CHUNK = 128
HG_CHUNK = 16
HG_GROUP = 16
IN_SIZES = (1024, 256, 256, 512, 512, 512, 512, 512, 512, 512)
N_DEV = 8
ROW_BLOCK = 256
MAX_TK = 2816
ADAM_LR, ADAM_B1, ADAM_B2, ADAM_EPS, ADAM_WD, ADAM_STEP = 0.001, 0.9, 0.999, 1e-08, 0.01, 10

WEIGHTS = ['c_ctx', 'w_ada', 'b_ada', 'norm1_g', 'w_in', 'q_norm_g', 'k_norm_g', 'hg_lower_bounds', 'hg_norm_g',
           'sg_norm_g', 'sg_w', 'sg_b', 'w_out', 'norm2_g', 'w_up', 'conv_w', 'conv_b', 'w_down', 'final_norm_g']


def _pick(dim, cands):
    for t in cands:
        if dim % t == 0:
            return t
    return dim


def _my_index():
    return 4 * lax.axis_index("x") + 2 * lax.axis_index("y") + lax.axis_index("c")


def _mm_call(a, b, mode, out_dtype, name):
    if mode == "nn":
        (M, K), N = a.shape, b.shape[1]
    elif mode == "nt":
        (M, K), N = a.shape, b.shape[0]
    else:
        (K, M), N = a.shape, b.shape[1]
    tm = _pick(M, (1088, 1024, 512, 256, 128))
    tn = _pick(N, (1024, 512, 256, 128))
    tk = K if K <= MAX_TK else _pick(K, (2816, 2560, 2176, 2048, 1408, 1088, 1024, 512, 256, 128))
    nk = K // tk
    dims = {"nn": (((1,), (0,)), ((), ())), "nt": (((1,), (1,)), ((), ())), "tn": (((0,), (0,)), ((), ()))}[mode]

    def body(a_ref, b_ref, o_ref, *acc):
        prod = lax.dot_general(a_ref[...].astype(bf16), b_ref[...].astype(bf16), dims, preferred_element_type=f32)
        if nk == 1:
            o_ref[...] = prod.astype(o_ref.dtype)
            return
        k = pl.program_id(2)

        @pl.when(k == 0)
        def _():
            acc[0][...] = prod

        @pl.when((k > 0) & (k < nk - 1))
        def _():
            acc[0][...] += prod

        @pl.when(k == nk - 1)
        def _():
            o_ref[...] = (acc[0][...] + prod).astype(o_ref.dtype)

    a_spec = pl.BlockSpec((tk, tm), lambda i, j, k: (k, i)) if mode == "tn" else pl.BlockSpec((tm, tk), lambda i, j, k: (i, k))
    b_spec = pl.BlockSpec((tn, tk), lambda i, j, k: (j, k)) if mode == "nt" else pl.BlockSpec((tk, tn), lambda i, j, k: (k, j))
    return pl.pallas_call(
        body, name=name, grid=(M // tm, N // tn, nk),
        in_specs=[a_spec, b_spec], out_specs=pl.BlockSpec((tm, tn), lambda i, j, k: (i, j)),
        out_shape=jax.ShapeDtypeStruct((M, N), out_dtype),
        scratch_shapes=[pltpu.VMEM((tm, tn), f32)] if nk > 1 else [],
        compiler_params=pltpu.CompilerParams(dimension_semantics=("parallel", "parallel", "arbitrary")),
    )(a, b)


def mm(a, w, name, out_dtype=f32, split=None):
    def parts(y):
        if split is None:
            return y
        offs = [sum(split[:i]) for i in range(len(split))]
        return tuple(y[:, o:o + s] for o, s in zip(offs, split))

    @jax.custom_vjp
    def op(a, w):
        return parts(_mm_call(a, w, "nn", out_dtype, name + "_fwd"))

    def fwd(a, w):
        return op(a, w), (a, w)

    def bwd(res, dy):
        a, w = res
        dy = dy.astype(bf16) if split is None else jnp.concatenate([d.astype(bf16) for d in dy], axis=1)
        return _mm_call(dy, w, "nt", a.dtype, name + "_bwd_a"), _mm_call(a, dy, "tn", w.dtype, name + "_bwd_w")

    op.defvjp(fwd, bwd)
    return op(a, w)


def _rowwise_specs(rows, consts, sels, pars, tb, nlb):
    specs = [pl.BlockSpec((tb, a.shape[1]), lambda i: (i, 0)) for a in (*rows, *consts)]
    specs += [pl.BlockSpec((None,) + a.shape[1:], lambda i: (jnp.where(i >= nlb, 1, 0), 0, 0)) for a in sels]
    specs += [pl.BlockSpec(a.shape, functools.partial(lambda i, n: (0,) * n, n=a.ndim)) for a in pars]
    return specs


def rowwise(name, f, rows, consts=(), sels=(), pars=(), outs=(), n_lat=None, tb=ROW_BLOCK):
    rows, consts, sels, pars = tuple(rows), tuple(consts), tuple(sels), tuple(pars)
    R = rows[0].shape[0]
    nb = R // tb
    nlb = nb if n_lat is None else n_lat // tb
    n_in = len(rows) + len(consts) + len(sels) + len(pars)
    n_out = len(outs)
    out_dtypes = [d for _, d in outs]
    out_specs = [pl.BlockSpec((tb, w), lambda i: (i, 0)) for w, _ in outs]
    out_shape = [jax.ShapeDtypeStruct((R, w), d) for w, d in outs]
    sem = pltpu.CompilerParams(dimension_semantics=("arbitrary",))

    def run_fwd(rows, consts, sels, pars):
        def body(*refs):
            res = f(*[r[...] for r in refs[:n_in]])
            for o_ref, r in zip(refs[n_in:], res):
                o_ref[...] = r.astype(o_ref.dtype)

        return tuple(pl.pallas_call(
            body, name=name + "_fwd", grid=(nb,), in_specs=_rowwise_specs(rows, consts, sels, pars, tb, nlb),
            out_specs=out_specs, out_shape=out_shape, compiler_params=sem,
        )(*rows, *consts, *sels, *pars))

    def run_bwd(rows, consts, sels, pars, cts):
        nr, nc, ns, npar = len(rows), len(consts), len(sels), len(pars)

        def body(*refs):
            i = pl.program_id(0)
            ins = [r[...] for r in refs[:n_in]]
            ct = tuple(r[...] for r in refs[n_in:n_in + n_out])
            o_refs = refs[n_in + n_out:]
            cvals = ins[nr:nr + nc]

            def g(*d):
                res = f(*d[:nr], *cvals, *d[nr:])
                return tuple(r.astype(t) for r, t in zip(res, out_dtypes))

            _, vjp = jax.vjp(g, *ins[:nr], *ins[nr + nc:])
            grads = vjp(ct)
            for k in range(nr):
                o_refs[k][...] = grads[k].astype(o_refs[k].dtype)
            for k in range(nr, nr + ns + npar):
                first = (i == 0) | (i == nlb) if k < nr + ns else (i == 0)
                gk = grads[k].astype(f32)

                @pl.when(first)
                def _(k=k, gk=gk):
                    o_refs[k][...] = gk

                @pl.when(jnp.logical_not(first))
                def _(k=k, gk=gk):
                    o_refs[k][...] += gk

        in_specs = _rowwise_specs(rows, consts, sels, pars, tb, nlb) + out_specs
        o_specs = [pl.BlockSpec((tb, a.shape[1]), lambda i: (i, 0)) for a in rows]
        o_specs += [pl.BlockSpec((None,) + a.shape[1:], lambda i: (jnp.where(i >= nlb, 1, 0), 0, 0)) for a in sels]
        o_specs += [pl.BlockSpec(a.shape, functools.partial(lambda i, n: (0,) * n, n=a.ndim)) for a in pars]
        o_shape = [jax.ShapeDtypeStruct(a.shape, a.dtype) for a in rows]
        o_shape += [jax.ShapeDtypeStruct(a.shape, f32) for a in (*sels, *pars)]
        res = pl.pallas_call(
            body, name=name + "_bwd", grid=(nb,), in_specs=in_specs, out_specs=o_specs, out_shape=o_shape,
            compiler_params=sem,
        )(*rows, *consts, *sels, *pars, *cts)
        return tuple(res[:nr]), tuple(res[nr:nr + ns]), tuple(res[nr + ns:])

    @jax.custom_vjp
    def op(rows, consts, sels, pars):
        return run_fwd(rows, consts, sels, pars)

    def fwd(rows, consts, sels, pars):
        return op(rows, consts, sels, pars), (rows, consts, sels, pars)

    def bwd(res, cts):
        rows, consts, sels, pars = res
        drows, dsels, dpars = run_bwd(rows, consts, sels, pars, tuple(cts))
        return drows, tuple(jnp.zeros_like(c) for c in consts), dsels, dpars

    op.defvjp(fwd, bwd)
    return op(rows, consts, sels, pars)


def _rms(x, g):
    return x * lax.rsqrt(jnp.mean(x * x, axis=-1, keepdims=True) + EPS) * g


def _silu(x):
    return x * jax.nn.sigmoid(x)


def f_modnorm(x, mods, g):
    return (_rms(x, g) * (1.0 + mods[1:2]) + mods[0:1],)


def f_resid_modnorm(x, y, mods, g):
    xn = x + mods[0:1] * y
    return xn, _rms(xn, g) * (1.0 + mods[2:3]) + mods[1:2]


def f_resid_final(x, y, tgt, mods, g):
    xn = x + mods[0:1] * y
    err = _rms(xn, g) - tgt
    return (jnp.mean(err * err, axis=-1, keepdims=True),)


def f_qkprep(aq, ak, av, cosf, sinf, qg, kg):
    r = lax.broadcasted_iota(jnp.int32, (HEAD, HEAD), 0)
    c = lax.broadcasted_iota(jnp.int32, (HEAD, HEAD), 1)
    swap = jnp.where((r ^ 1) == c, 1.0, 0.0).astype(f32)

    def head(xh, g):
        y = _rms(xh, g)
        ys = jnp.dot(y, swap, precision=HI, preferred_element_type=f32)
        return y * cosf + ys * sinf

    q = jnp.concatenate([head(aq[:, h * HEAD:(h + 1) * HEAD], qg) for h in range(ATTN_HEADS)], axis=1)
    k = jnp.concatenate([head(ak[:, h * HEAD:(h + 1) * HEAD], kg) for h in range(ATTN_KV)], axis=1)
    return q, k, av


def f_hgout(of, ob, gt, g):
    o = of + ob
    y = jnp.concatenate([_rms(o[:, h * HEAD:(h + 1) * HEAD], g) for h in range(HG_HEADS)], axis=1)
    return (y * _silu(gt),)


def f_sgate(u, v, g, w, b0, b1, b2, b3):
    u = jax.nn.gelu(u)
    v = jax.nn.gelu(v)
    bs = (b0, b1, b2, b3)
    cols = []
    for gi in range(SG_GROUPS):
        sl = slice(gi * HEAD, (gi + 1) * HEAD)
        vg = _rms(v[:, sl], g[:, sl])
        parts = []
        for n in range(v.shape[0] // SG_CHUNK):
            vc = vg[n * SG_CHUNK:(n + 1) * SG_CHUNK]
            parts.append(jnp.dot(w[gi].astype(bf16), vc.astype(bf16), preferred_element_type=f32) + bs[gi])
        cols.append(jnp.concatenate(parts, axis=0))
    return (u * jnp.concatenate(cols, axis=1),)


def attention(q, k, v, T, name, tq=ROW_BLOCK):
    NT = q.shape[0]
    nqb, nlb = NT // tq, T // tq
    scale = HEAD ** -0.5
    q_spec = pl.BlockSpec((tq, HEAD), lambda kv, g, i: (i, kv * ATTN_GROUP + g))
    kv_spec = pl.BlockSpec((NT, HEAD), lambda kv, g, i: (0, kv))
    lse_spec = pl.BlockSpec((None, tq, 1), lambda kv, g, i: (kv * ATTN_GROUP + g, i, 0))
    grid = (ATTN_KV, ATTN_GROUP, nqb)
    nt_dims = (((1,), (1,)), ((), ()))
    tn_dims = (((0,), (0,)), ((), ()))

    def on_keys(i, fn):
        @pl.when(i < nlb)
        def _():
            fn(pl.ds(0, NT))

        @pl.when(i >= nlb)
        def _():
            fn(pl.ds(T, NT - T))

    def run_fwd(q, k, v):
        def body(q_ref, k_ref, v_ref, o_ref, lse_ref):
            def run(rows):
                s = lax.dot_general(q_ref[...], k_ref[rows, :], nt_dims, preferred_element_type=f32)
                m = jnp.max(s, axis=-1, keepdims=True) * scale
                p = jnp.exp(s * scale - m)
                l = jnp.sum(p, axis=-1, keepdims=True)
                o = jnp.dot(p.astype(bf16), v_ref[rows, :], preferred_element_type=f32) / l
                o_ref[...] = o.astype(o_ref.dtype)
                lse_ref[...] = m + jnp.log(l)

            on_keys(pl.program_id(2), run)

        return pl.pallas_call(
            body, name=name + "_fwd", grid=grid, in_specs=[q_spec, kv_spec, kv_spec], out_specs=[q_spec, lse_spec],
            out_shape=[jax.ShapeDtypeStruct(q.shape, bf16), jax.ShapeDtypeStruct((ATTN_HEADS, NT, 1), f32)],
            compiler_params=pltpu.CompilerParams(dimension_semantics=("parallel", "parallel", "arbitrary")),
        )(q, k, v)

    def run_bwd(q, k, v, lse, do):
        def body(q_ref, k_ref, v_ref, lse_ref, do_ref, dq_ref, dk_ref, dv_ref):
            g, i = pl.program_id(1), pl.program_id(2)

            @pl.when((g == 0) & (i == 0))
            def _():
                dk_ref[...] = jnp.zeros_like(dk_ref)
                dv_ref[...] = jnp.zeros_like(dv_ref)

            def run(rows):
                qb, kb, vb, dob = q_ref[...], k_ref[rows, :], v_ref[rows, :], do_ref[...]
                s = lax.dot_general(qb, kb, nt_dims, preferred_element_type=f32)
                p = jnp.exp(s * scale - lse_ref[...])
                dp = lax.dot_general(dob, vb, nt_dims, preferred_element_type=f32)
                ds = (p * (dp - jnp.sum(p * dp, axis=-1, keepdims=True)) * scale).astype(bf16)
                dq_ref[...] = jnp.dot(ds, kb, preferred_element_type=f32).astype(dq_ref.dtype)
                dk_ref[rows, :] += lax.dot_general(ds, qb, tn_dims, preferred_element_type=f32)
                dv_ref[rows, :] += lax.dot_general(p.astype(bf16), dob, tn_dims, preferred_element_type=f32)

            on_keys(i, run)

        return pl.pallas_call(
            body, name=name + "_bwd", grid=grid, in_specs=[q_spec, kv_spec, kv_spec, lse_spec, q_spec],
            out_specs=[q_spec, kv_spec, kv_spec],
            out_shape=[jax.ShapeDtypeStruct(q.shape, bf16), jax.ShapeDtypeStruct(k.shape, f32), jax.ShapeDtypeStruct(v.shape, f32)],
            compiler_params=pltpu.CompilerParams(dimension_semantics=("parallel", "arbitrary", "arbitrary")),
        )(q, k, v, lse, do)

    @jax.custom_vjp
    def op(q, k, v):
        return run_fwd(q, k, v)[0]

    def fwd(q, k, v):
        o, lse = run_fwd(q, k, v)
        return o, (q, k, v, lse)

    def bwd(res, do):
        dq, dk, dv = run_bwd(*res, do)
        return dq, dk.astype(bf16), dv.astype(bf16)

    op.defvjp(fwd, bwd)
    return op(q, k, v)


def _bdot(a, b, ca, cb):
    fa, fb = 3 - ca, 3 - cb

    def dot(x, y, cx, cy):
        return lax.dot_general(x.astype(bf16), y.astype(bf16), (((cx,), (cy,)), ((0,), (0,))), preferred_element_type=f32)

    @jax.custom_vjp
    def op(a, b):
        return dot(a, b, ca, cb)

    def fwd(a, b):
        return op(a, b), (a, b)

    def bwd(res, ct):
        a, b = res
        da = dot(ct, b, 2, fb) if ca == 2 else dot(b, ct, fb, 2)
        db = dot(a, ct, fa, 1) if cb == 1 else dot(ct, a, 1, fa)
        return da, db

    op.defvjp(fwd, bwd)
    return op(a, b)


def _chunk_cumsum(x, rev):
    def impl(x, rev):
        n = x.shape[0]
        pos = lax.broadcasted_iota(jnp.int32, x.shape, 0) % HG_CHUNK
        s = 1
        while s < HG_CHUNK:
            if rev:
                x = x + jnp.where(pos < HG_CHUNK - s, pltpu.roll(x, n - s, 0), 0.0)
            else:
                x = x + jnp.where(pos >= s, pltpu.roll(x, s, 0), 0.0)
            s *= 2
        return x

    @jax.custom_vjp
    def op(x):
        return impl(x, rev)

    op.defvjp(lambda x: (op(x), None), lambda _, ct: (impl(ct, not rev),))
    return op(x)


def _hg_group(St, hq, hf, hi, lb, *, rev):
    G, C = HG_GROUP, HG_CHUNK
    R = G * C
    q = _silu(hq)
    f = lb + (1.0 - lb) * jax.nn.sigmoid(hf)
    logf = jnp.log(jnp.maximum(f, F_MIN))
    kk = (1.0 - lb) * jax.nn.sigmoid(-hf)
    b3 = _chunk_cumsum(logf, rev).reshape(G, C, HEAD)
    q3, k3, v3 = q.reshape(G, C, HEAD), kk.reshape(G, C, HEAD), hi.reshape(G, C, HEAD)
    btot = jnp.sum(logf.reshape(G, C, HEAD), axis=1)
    tt = lax.broadcasted_iota(jnp.int32, (G, C, C, HEAD), 1)
    ss = lax.broadcasted_iota(jnp.int32, (G, C, C, HEAD), 2)
    mask = (ss >= tt) if rev else (ss <= tt)
    diff = b3[:, :, None, :] - b3[:, None, :, :]
    dec = jnp.where(mask, jnp.exp(jnp.where(mask, diff, 0.0)), 0.0)
    scores = jnp.sum(q3[:, :, None, :] * k3[:, None, :, :] * dec, axis=-1)
    o_intra = _bdot(scores, v3, 2, 1)
    q_dec = q3 * jnp.exp(b3)
    k_dec = k3 * jnp.exp(btot[:, None, :] - b3)
    kvt = _bdot(v3, k_dec, 1, 1)
    dl = jnp.exp(btot)
    states = [None] * G
    for g in (range(G - 1, -1, -1) if rev else range(G)):
        states[g] = St
        St = St * dl[g:g + 1, :] + kvt[g]
    o_inter = _bdot(q_dec, jnp.stack(states), 2, 2)
    return St, (o_intra + o_inter).reshape(R, HEAD)


def hgrn(hq, hf, hi, lb, T, rev, name):
    NT, W = hq.shape
    R = HG_GROUP * HG_CHUNK
    n_lat, n_ctx = T // R, (NT - T) // R
    nG = n_lat + n_ctx

    def group_of(j):
        if rev:
            return jnp.where(j < n_ctx, nG - 1 - j, n_lat - 1 - (j - n_ctx))
        return jnp.where(j < n_ctx, n_lat + j, j - n_ctx)

    def rows_of(j):
        return pl.ds(pl.multiple_of(group_of(j) * R, R), R)

    col_spec = pl.BlockSpec((NT, HEAD), lambda h: (0, h))
    lb_spec = pl.BlockSpec((1, HEAD), lambda h: (0, h))
    st_spec = pl.BlockSpec((None, nG, HEAD, HEAD), lambda h: (h, 0, 0, 0))
    sem = pltpu.CompilerParams(dimension_semantics=("parallel",))

    def run_fwd(hq, hf, hi, lb):
        def body(hq_ref, hf_ref, hi_ref, lb_ref, o_ref, st_ref):
            def step(j, St):
                st_ref[j] = St
                rows = rows_of(j)
                St, o = _hg_group(St, hq_ref[rows, :], hf_ref[rows, :], hi_ref[rows, :], lb_ref[...], rev=rev)
                o_ref[rows, :] = o
                return St

            lax.fori_loop(0, nG, step, jnp.zeros((HEAD, HEAD), f32))

        return pl.pallas_call(
            body, name=name + "_fwd", grid=(W // HEAD,), in_specs=[col_spec, col_spec, col_spec, lb_spec],
            out_specs=[col_spec, st_spec],
            out_shape=[jax.ShapeDtypeStruct((NT, W), f32), jax.ShapeDtypeStruct((W // HEAD, nG, HEAD, HEAD), f32)],
            compiler_params=sem,
        )(hq, hf, hi, lb)

    def run_bwd(hq, hf, hi, lb, st, do):
        def body(hq_ref, hf_ref, hi_ref, lb_ref, st_ref, do_ref, dq_ref, df_ref, di_ref, dlb_ref):
            def step(jj, carry):
                dS, dlb = carry
                j = nG - 1 - jj
                rows = rows_of(j)
                _, vjp = jax.vjp(functools.partial(_hg_group, rev=rev), st_ref[j], hq_ref[rows, :], hf_ref[rows, :],
                                 hi_ref[rows, :], lb_ref[...])
                dS, dq, df, di, dl = vjp((dS, do_ref[rows, :]))
                dq_ref[rows, :] = dq
                df_ref[rows, :] = df
                di_ref[rows, :] = di
                return dS, dlb + dl

            _, dlb = lax.fori_loop(0, nG, step, (jnp.zeros((HEAD, HEAD), f32), jnp.zeros((1, HEAD), f32)))
            dlb_ref[...] = dlb

        return pl.pallas_call(
            body, name=name + "_bwd", grid=(W // HEAD,),
            in_specs=[col_spec, col_spec, col_spec, lb_spec, st_spec, col_spec],
            out_specs=[col_spec, col_spec, col_spec, lb_spec],
            out_shape=[jax.ShapeDtypeStruct((NT, W), f32)] * 3 + [jax.ShapeDtypeStruct((1, W), f32)],
            compiler_params=sem,
        )(hq, hf, hi, lb, st, do)

    @jax.custom_vjp
    def op(hq, hf, hi, lb):
        return run_fwd(hq, hf, hi, lb)[0]

    def fwd(hq, hf, hi, lb):
        o, st = run_fwd(hq, hf, hi, lb)
        return o, (hq, hf, hi, lb, st)

    def bwd(res, do):
        return tuple(run_bwd(*res, do))

    op.defvjp(fwd, bwd)
    return op(hq, hf, hi, lb)


def lower_bounds(params):
    n = len(params)

    def f(*a):
        m = functools.reduce(jnp.maximum, a)
        e = [jnp.exp(x - m) for x in a]
        s = functools.reduce(lambda u, v: u + v, e)
        p = [x / s for x in e]
        out, run = [], jnp.zeros_like(p[0])
        for l in range(n):
            out.append(run)
            run = run + p[l]
        return tuple(out[l] + p[l] - p[0] for l in range(n))

    shape = [jax.ShapeDtypeStruct(params[0].shape, f32)] * n

    @jax.custom_vjp
    def op(*a):
        def body(*refs):
            for o_ref, r in zip(refs[n:], f(*[x[...] for x in refs[:n]])):
                o_ref[...] = r
        return tuple(pl.pallas_call(body, name="lower_bounds_fwd", out_shape=shape)(*a))

    def fwd(*a):
        return op(*a), a

    def bwd(a, cts):
        def body(*refs):
            _, vjp = jax.vjp(f, *[x[...] for x in refs[:n]])
            for o_ref, r in zip(refs[2 * n:], vjp(tuple(x[...] for x in refs[n:2 * n]))):
                o_ref[...] = r
        return tuple(pl.pallas_call(body, name="lower_bounds_bwd", out_shape=shape)(*a, *cts))

    op.defvjp(fwd, bwd)
    return op(*params)


def _shift_rows(x, d, T):
    n = x.shape[0]
    t = lax.broadcasted_iota(jnp.int32, x.shape, 0)
    y = pltpu.roll(x, d % n, 0)
    edge = ((t == 0) | (t == T)) if d == 1 else ((t == T - 1) | (t == n - 1))
    return jnp.where(edge, 0.0, y)


def _conv(x, w, b, T):
    return b + w[0:1] * _shift_rows(x, 1, T) + w[1:2] * x + w[2:3] * _shift_rows(x, -1, T)


def convact(up, cw, cb, T, name, tc=128):
    NT, F2 = up.shape
    F = F2 // 2
    tc = _pick(F, (tc, 128))
    nf = F // tc
    g_spec = lambda r: pl.BlockSpec((r, tc), lambda j: (0, j))
    v_spec = lambda r: pl.BlockSpec((r, tc), lambda j: (0, j + nf))
    sem = pltpu.CompilerParams(dimension_semantics=("parallel",))

    def run_fwd(up, cw, cb):
        def body(xg_ref, xv_ref, wg_ref, wv_ref, bg_ref, bv_ref, o_ref):
            yg = _conv(xg_ref[...].astype(f32), wg_ref[...], bg_ref[...], T)
            yv = _conv(xv_ref[...].astype(f32), wv_ref[...], bv_ref[...], T)
            o_ref[...] = (_silu(yg) * yv).astype(o_ref.dtype)

        return pl.pallas_call(
            body, name=name + "_fwd", grid=(nf,),
            in_specs=[g_spec(NT), v_spec(NT), g_spec(3), v_spec(3), g_spec(1), v_spec(1)], out_specs=g_spec(NT),
            out_shape=jax.ShapeDtypeStruct((NT, F), bf16), compiler_params=sem,
        )(up, up, cw, cw, cb, cb)

    def run_bwd(up, cw, cb, dact):
        def body(xg_ref, xv_ref, wg_ref, wv_ref, bg_ref, bv_ref, da_ref, dxg_ref, dxv_ref, dwg_ref, dwv_ref, dbg_ref, dbv_ref):
            xg, xv, wg, wv = xg_ref[...].astype(f32), xv_ref[...].astype(f32), wg_ref[...], wv_ref[...]
            yg = _conv(xg, wg, bg_ref[...], T)
            yv = _conv(xv, wv, bv_ref[...], T)
            da = da_ref[...].astype(f32)
            sg = jax.nn.sigmoid(yg)
            dyv = da * yg * sg
            dyg = da * yv * sg * (1.0 + yg * (1.0 - sg))
            for x, w, dy, dx_ref, dw_ref, db_ref in ((xg, wg, dyg, dxg_ref, dwg_ref, dbg_ref), (xv, wv, dyv, dxv_ref, dwv_ref, dbv_ref)):
                dx_ref[...] = (w[0:1] * _shift_rows(dy, -1, T) + w[1:2] * dy + w[2:3] * _shift_rows(dy, 1, T)).astype(dx_ref.dtype)
                dw_ref[...] = jnp.concatenate([
                    jnp.sum(dy * _shift_rows(x, 1, T), axis=0, keepdims=True),
                    jnp.sum(dy * x, axis=0, keepdims=True),
                    jnp.sum(dy * _shift_rows(x, -1, T), axis=0, keepdims=True)], axis=0)
                db_ref[...] = jnp.sum(dy, axis=0, keepdims=True)

        return pl.pallas_call(
            body, name=name + "_bwd", grid=(nf,),
            in_specs=[g_spec(NT), v_spec(NT), g_spec(3), v_spec(3), g_spec(1), v_spec(1), g_spec(NT)],
            out_specs=[g_spec(NT), g_spec(NT), g_spec(3), g_spec(3), g_spec(1), g_spec(1)],
            out_shape=[jax.ShapeDtypeStruct((NT, F), up.dtype)] * 2 + [jax.ShapeDtypeStruct((3, F), f32)] * 2 + [jax.ShapeDtypeStruct((1, F), f32)] * 2,
            compiler_params=sem,
        )(up, up, cw, cw, cb, cb, dact)

    @jax.custom_vjp
    def op(up, cw, cb):
        return run_fwd(up, cw, cb)

    def fwd(up, cw, cb):
        return op(up, cw, cb), (up, cw, cb)

    def bwd(res, dact):
        dxg, dxv, dwg, dwv, dbg, dbv = run_bwd(*res, dact)
        return (jnp.concatenate([dxg, dxv], axis=1), jnp.concatenate([dwg, dwv], axis=1), jnp.concatenate([dbg, dbv], axis=1))

    op.defvjp(fwd, bwd)
    return op(up, cw, cb)


def _peers():
    x, y, c = lax.axis_index("x"), lax.axis_index("y"), lax.axis_index("c")
    return (x, y, c), [(x, y, 1 - c), (1 - x, y, c), (x, 1 - y, c), (1 - x, 1 - y, c),
                       (1 - x, y, 1 - c), (x, 1 - y, 1 - c), (1 - x, 1 - y, 1 - c)]


def _index(dev):
    return 4 * dev[0] + 2 * dev[1] + dev[2]


def allgather_small(x, name):
    m, n = x.shape

    def body(x_ref, out_ref, send_sems, recv_sems, local_sem):
        me, peers = _peers()

        def rows(dev):
            return out_ref.at[pl.ds(pl.multiple_of(_index(dev) * m, 8), m), :]

        mine = pltpu.make_async_copy(x_ref, rows(me), local_sem)
        mine.start()
        sends = [pltpu.make_async_remote_copy(src_ref=x_ref, dst_ref=rows(me), send_sem=send_sems.at[k], recv_sem=recv_sems.at[k],
                                              device_id=p, device_id_type=MESH) for k, p in enumerate(peers)]
        for cp in sends:
            cp.start()
        for k, p in enumerate(peers):
            pltpu.make_async_remote_copy(src_ref=x_ref, dst_ref=rows(p), send_sem=send_sems.at[k], recv_sem=recv_sems.at[k],
                                         device_id=p, device_id_type=MESH).wait_recv()
        for cp in sends:
            cp.wait_send()
        mine.wait()

    return pl.pallas_call(
        body, name=name, out_shape=jax.ShapeDtypeStruct((N_DEV * m, n), x.dtype),
        in_specs=[pl.BlockSpec(memory_space=pltpu.VMEM)], out_specs=pl.BlockSpec(memory_space=pltpu.VMEM),
        scratch_shapes=[pltpu.SemaphoreType.DMA((7,)), pltpu.SemaphoreType.DMA((7,)), pltpu.SemaphoreType.DMA],
    )(x)


HBM_SPEC = pl.BlockSpec(memory_space=pltpu.HBM)
SEM_SPEC = pl.BlockSpec(memory_space=pltpu.SEMAPHORE)
SPLIT_PARAMS = dict(compiler_params=pltpu.CompilerParams(has_side_effects=pltpu.SideEffectType.DATAFLOW_SIDE_EFFECTING))
N_PEERS = N_DEV - 1


def _part(ref, kind, j, width):
    if kind == "col":
        return ref.at[:, pl.ds(pl.multiple_of(j * width, 128), width)]
    return ref.at[pl.ds(pl.multiple_of(j * width, 8), width), :]


def _in_hbm(a):
    return pltpu.with_memory_space_constraint(a, pltpu.HBM)


def place_own(shards):
    n = len(shards)
    whole = [jax.ShapeDtypeStruct((N_DEV * s.shape[0], s.shape[1]), s.dtype) for s in shards]

    def body(*refs):
        srcs, outs, sems = refs[:n], refs[n:2 * n], refs[2 * n]
        me, _ = _peers()
        copies = [pltpu.make_async_copy(srcs[a], _part(outs[a], "row", _index(me), shards[a].shape[0]), sems.at[a]) for a in range(n)]
        for cp in copies:
            cp.start()
        for cp in copies:
            cp.wait()

    any_spec = pl.BlockSpec(memory_space=pl.ANY)
    return pl.pallas_call(body, name="place_own", out_shape=whole, in_specs=[any_spec] * n, out_specs=[any_spec] * n,
                          scratch_shapes=[pltpu.SemaphoreType.DMA((n,))])(*shards)


def gather_start(shards, lands, kinds):
    n = len(shards)
    widths = [s.shape[1] if k == "col" else s.shape[0] for s, k in zip(shards, kinds)]

    def body(*refs):
        srcs, lnds, send_sems, recv_sems, token = refs[:n], refs[n:2 * n], refs[2 * n], refs[2 * n + 1], refs[-1]
        me, peers = _peers()
        for a in range(n):
            for k, p in enumerate(peers):
                pltpu.make_async_remote_copy(
                    src_ref=srcs[a], dst_ref=_part(lnds[a], kinds[a], _index(me), widths[a]),
                    send_sem=send_sems.at[N_PEERS * a + k], recv_sem=recv_sems.at[N_PEERS * a + k],
                    device_id=p, device_id_type=MESH).start()
        token[...] = jnp.zeros_like(token)

    res = pl.pallas_call(
        body, name="gather_start",
        out_shape=(pltpu.SemaphoreType.DMA((N_PEERS * n,)), pltpu.SemaphoreType.DMA((N_PEERS * n,)),
                   *[pltpu.HBM(a.shape, a.dtype) for a in (*shards, *lands)], jax.ShapeDtypeStruct((8, 128), f32)),
        in_specs=[HBM_SPEC] * (2 * n), out_specs=(SEM_SPEC, SEM_SPEC, *[HBM_SPEC] * (2 * n), pl.BlockSpec(memory_space=pltpu.VMEM)),
        input_output_aliases={i: 2 + i for i in range(2 * n)}, **SPLIT_PARAMS,
    )(*[_in_hbm(a) for a in (*shards, *lands)])
    return res[0], res[1], res[2:2 + n], res[2 + n:2 + 2 * n], res[-1]


def gather_wait(a, shard, land, kind, send_sems, recv_sems, after, name):
    width = shard.shape[1] if kind == "col" else shard.shape[0]

    def body(src_ref, land_ref, send_ref, recv_ref, after_ref, src_out, land_out):
        _, peers = _peers()
        for k, p in enumerate(peers):
            cp = pltpu.make_async_remote_copy(
                src_ref=src_ref, dst_ref=_part(land_ref, kind, _index(p), width),
                send_sem=send_ref.at[N_PEERS * a + k], recv_sem=recv_ref.at[N_PEERS * a + k], device_id=p, device_id_type=MESH)
            cp.wait_send()
            cp.wait_recv()

    return pl.pallas_call(
        body, name=name, out_shape=(pltpu.HBM(shard.shape, shard.dtype), pltpu.HBM(land.shape, land.dtype)),
        in_specs=(HBM_SPEC, HBM_SPEC, SEM_SPEC, SEM_SPEC, pl.BlockSpec(memory_space=pl.ANY)), out_specs=(HBM_SPEC, HBM_SPEC),
        input_output_aliases={0: 0, 1: 1}, **SPLIT_PARAMS,
    )(shard, land, send_sems, recv_sems, after)[1]


def scatter_start(full, slots, kind, name):
    width = slots.shape[2] if kind == "col" else slots.shape[1]

    def body(full_ref, slots_ref, send_sems, recv_sems, full_out, slots_out, token):
        me, peers = _peers()
        for k, p in enumerate(peers):
            pltpu.make_async_remote_copy(
                src_ref=_part(full_ref, kind, _index(p), width), dst_ref=slots_ref.at[_index(me)],
                send_sem=send_sems.at[k], recv_sem=recv_sems.at[k], device_id=p, device_id_type=MESH).start()
        token[...] = jnp.zeros_like(token)

    return pl.pallas_call(
        body, name=name,
        out_shape=(pltpu.SemaphoreType.DMA((N_PEERS,)), pltpu.SemaphoreType.DMA((N_PEERS,)), pltpu.HBM(full.shape, full.dtype),
                   pltpu.HBM(slots.shape, slots.dtype), jax.ShapeDtypeStruct((8, 128), f32)),
        in_specs=(HBM_SPEC, HBM_SPEC), out_specs=(SEM_SPEC, SEM_SPEC, HBM_SPEC, HBM_SPEC, pl.BlockSpec(memory_space=pltpu.VMEM)),
        input_output_aliases={0: 2, 1: 3}, **SPLIT_PARAMS,
    )(_in_hbm(full), _in_hbm(slots))


def scatter_wait(full, slots, kind, send_sems, recv_sems, after, name):
    width = slots.shape[2] if kind == "col" else slots.shape[1]

    def body(full_ref, slots_ref, send_ref, recv_ref, after_ref, full_out, slots_out):
        me, peers = _peers()
        for k, p in enumerate(peers):
            cp = pltpu.make_async_remote_copy(
                src_ref=_part(full_ref, kind, _index(p), width), dst_ref=slots_ref.at[_index(p)],
                send_sem=send_ref.at[k], recv_sem=recv_ref.at[k], device_id=p, device_id_type=MESH)
            cp.wait_send()
            cp.wait_recv()

    return pl.pallas_call(
        body, name=name, out_shape=(pltpu.HBM(full.shape, full.dtype), pltpu.HBM(slots.shape, slots.dtype)),
        in_specs=(HBM_SPEC, HBM_SPEC, SEM_SPEC, SEM_SPEC, pl.BlockSpec(memory_space=pl.ANY)), out_specs=(HBM_SPEC, HBM_SPEC),
        input_output_aliases={0: 0, 1: 1}, **SPLIT_PARAMS,
    )(full, slots, send_sems, recv_sems, after)[1]


def sum_slots(x, name):
    _, R, C = x.shape
    tr = _pick(R, (256, 128, 64, 32, 16, 8))

    def body(x_ref, o_ref):
        acc = x_ref[0].astype(f32)
        for d in range(1, N_DEV):
            acc = acc + x_ref[d].astype(f32)
        o_ref[...] = acc

    return pl.pallas_call(
        body, name=name, grid=(R // tr,), in_specs=[pl.BlockSpec((N_DEV, tr, C), lambda i: (0, i, 0))],
        out_specs=pl.BlockSpec((tr, C), lambda i: (i, 0)), out_shape=jax.ShapeDtypeStruct((R, C), f32),
        compiler_params=pltpu.CompilerParams(dimension_semantics=("parallel",)),
    )(x)


BIG = (("w_in", "col"), ("w_out", "row"), ("w_up", "col"), ("w_down", "row"))


class BigWeights:
    def __init__(self, given, me):
        self.me = me
        self.kinds = dict(BIG)
        depth = given["w_in"].shape[0]
        self.keys = [(l, n) for l in range(depth) for n, _ in BIG]
        kinds = [self.kinds[n] for _, n in self.keys]
        shards = [given[n][l].astype(bf16) for l, n in self.keys]
        rows = [a for a, k in enumerate(kinds) if k == "row"]
        lands = dict(zip(rows, place_own([shards[a] for a in rows])))
        for a, k in enumerate(kinds):
            if k == "col":
                r, c = shards[a].shape
                lands[a] = lax.dynamic_update_slice(lax.empty((r, N_DEV * c), bf16), shards[a], (0, me * c))
        self.send, self.recv, self.shards, self.lands, _ = gather_start(shards, [lands[a] for a in range(len(kinds))], kinds)
        self.pending = {}

    def get(self, l, n, after):
        a = self.keys.index((l, n))
        return gather_wait(a, self.shards[a], self.lands[a], self.kinds[n], self.send, self.recv, after, "gather_wait_%s%d" % (n, l))

    def start_scatter(self, l, n, dw):
        kind = self.kinds[n]
        R, C = dw.shape
        r, c = (R, C // N_DEV) if kind == "col" else (R // N_DEV, C)
        own = lax.dynamic_slice(dw, (0, self.me * c) if kind == "col" else (self.me * r, 0), (r, c))
        slots = lax.dynamic_update_slice(lax.empty((N_DEV, r, c), bf16), own[None], (self.me, 0, 0))
        send, recv, full, slots, token = scatter_start(dw, slots, kind, "scatter_start_%s%d" % (n, l))
        self.pending[(l, n)] = (full, slots, send, recv)
        return token

    def contributions(self, l, n, after):
        full, slots, send, recv = self.pending[(l, n)]
        return scatter_wait(full, slots, self.kinds[n], send, recv, after, "scatter_wait_%s%d" % (n, l))


def allreduce_small(vals, name):
    flat = jnp.concatenate([v.reshape(-1) for v in vals])
    n = flat.shape[0]
    cols = 1024
    m = -(-n // (cols * 8)) * 8
    packed = jnp.pad(flat, (0, m * cols - n)).reshape(m, cols)
    total = sum_slots(allgather_small(packed, name).reshape(N_DEV, m, cols), name + "_sum").reshape(-1)
    out, off = [], 0
    for v in vals:
        out.append(total[off:off + v.size].reshape(v.shape))
        off += v.size
    return out


def ada_mod(c_all, c_ctx, w_ada, b_ada):
    L, D, S = w_ada.shape
    me = _my_index()

    def stacked(c_ctx):
        return jnp.concatenate([c_all, jnp.broadcast_to(c_ctx, (N_DEV, D))], axis=0)

    ts = _pick(S, (512, 384, 256, 128, 64))
    w_spec = pl.BlockSpec((None, D, ts), lambda l, j: (l, 0, j))
    c_spec = pl.BlockSpec((16, D), lambda l, j: (0, 0))
    p_spec = pl.BlockSpec((None, 16, ts), lambda l, j: (l, 0, j))

    def run_fwd(cin, w_ada):
        def body(c_ref, w_ref, o_ref):
            o_ref[...] = jnp.dot(_silu(c_ref[...]).astype(bf16), w_ref[...].astype(bf16), preferred_element_type=f32)

        return pl.pallas_call(
            body, name="ada_fwd", grid=(L, S // ts), in_specs=[c_spec, w_spec], out_specs=p_spec,
            out_shape=jax.ShapeDtypeStruct((L, 16, S), f32),
            compiler_params=pltpu.CompilerParams(dimension_semantics=("parallel", "parallel")),
        )(cin, w_ada)

    def run_bwd(cin, w_ada, dm):
        def body(c_ref, w_ref, dm_ref, gw_ref, dc_ref):
            first = (pl.program_id(0) == 0) & (pl.program_id(1) == 0)
            cv = c_ref[...]
            sg = jax.nn.sigmoid(cv)
            dmv = dm_ref[...].astype(bf16)
            gw_ref[...] = lax.dot_general((cv * sg).astype(bf16), dmv, (((0,), (0,)), ((), ())), preferred_element_type=f32)
            ds = lax.dot_general(dmv, w_ref[...].astype(bf16), (((1,), (1,)), ((), ())), preferred_element_type=f32)
            dc = ds * sg * (1.0 + cv * (1.0 - sg))

            @pl.when(first)
            def _():
                dc_ref[...] = dc

            @pl.when(jnp.logical_not(first))
            def _():
                dc_ref[...] += dc

        return pl.pallas_call(
            body, name="ada_bwd", grid=(L, S // ts), in_specs=[c_spec, w_spec, p_spec], out_specs=[w_spec, c_spec],
            out_shape=[jax.ShapeDtypeStruct((L, D, S), f32), jax.ShapeDtypeStruct((16, D), f32)],
            compiler_params=pltpu.CompilerParams(dimension_semantics=("arbitrary", "arbitrary")),
        )(cin, w_ada, dm)

    def bias_grad(dm_full):
        def body(x_ref, o_ref):
            o_ref[...] = jnp.sum(x_ref[...], axis=0, keepdims=True)

        return pl.pallas_call(
            body, name="ada_bias_grad", grid=(L,), in_specs=[pl.BlockSpec((None, 16, 6 * D), lambda l: (l, 0, 0))],
            out_specs=pl.BlockSpec((None, 1, 6 * D), lambda l: (l, 0, 0)), out_shape=jax.ShapeDtypeStruct((L, 1, 6 * D), f32),
        )(dm_full).reshape(L, 6 * D)

    @jax.custom_vjp
    def op(c_ctx, w_ada, b_ada):
        prod = run_fwd(stacked(c_ctx), w_ada)
        allp = allgather_small(prod.reshape(L * 16, S), "ada_gather").reshape(N_DEV, L, 16, S)
        allp = allp.transpose(1, 2, 0, 3).reshape(L, 16, N_DEV * S)
        mine = lax.dynamic_index_in_dim(allp, me, axis=1, keepdims=False) + b_ada
        ctx = allp[:, N_DEV] + b_ada
        return jnp.stack([mine, ctx], axis=1).reshape(L, 2, 6, D)

    def fwd(c_ctx, w_ada, b_ada):
        return op(c_ctx, w_ada, b_ada), (c_ctx, w_ada)

    def bwd(res, dmod):
        c_ctx, w_ada = res
        dm = dmod.reshape(L * 2, 6 * D)
        gathered = allgather_small(jnp.pad(dm, ((0, (-2 * L) % 8), (0, 0))), "ada_grad_gather")
        gathered = gathered.reshape(N_DEV, -1, 6 * D)[:, :2 * L].reshape(N_DEV, L, 2, 6 * D)
        dm_full = gathered.transpose(1, 2, 0, 3).reshape(L, 16, 6 * D)
        dm_mine = lax.dynamic_slice_in_dim(dm_full, me * S, S, axis=2)
        gw, dc = run_bwd(stacked(c_ctx), w_ada, dm_mine)
        d_cctx = jnp.sum(dc[N_DEV:], axis=0, keepdims=True)
        return d_cctx, gw, bias_grad(dm_full)

    op.defvjp(fwd, bwd)
    return op(c_ctx, w_ada, b_ada)


def _adamw_math(w, gv, m, v):
    c1 = 1.0 / (1.0 - ADAM_B1 ** ADAM_STEP)
    c2 = 1.0 / (1.0 - ADAM_B2 ** ADAM_STEP)
    mn = ADAM_B1 * m + (1.0 - ADAM_B1) * gv
    vn = ADAM_B2 * v + (1.0 - ADAM_B2) * gv * gv
    return -ADAM_LR * ((mn * c1) / (jnp.sqrt(vn * c2) + ADAM_EPS) + ADAM_WD * w), mn, vn


def adamw_layer(l, slots, w, m, v, outs, name):
    depth, r, c = w.shape
    tr = _pick(r, (128, 64, 32, 16, 8))
    if outs is None:
        outs = tuple(lax.empty((depth, r, c), f32) for _ in range(4))

    def body(s_ref, w_ref, m_ref, v_ref, *rest):
        g_ref, d_ref, mo_ref, vo_ref = rest[4:]
        gv = s_ref[0].astype(f32)
        for d in range(1, N_DEV):
            gv = gv + s_ref[d].astype(f32)
        g_ref[...] = gv
        d_ref[...], mo_ref[...], vo_ref[...] = _adamw_math(w_ref[...], gv, m_ref[...], v_ref[...])

    lay = pl.BlockSpec((None, tr, c), lambda i: (l, i, 0))
    return pl.pallas_call(
        body, name=name, grid=(r // tr,),
        in_specs=[pl.BlockSpec((N_DEV, tr, c), lambda i: (0, i, 0)), lay, lay, lay] + [pl.BlockSpec(memory_space=pl.ANY)] * 4,
        out_specs=[lay] * 4, out_shape=[jax.ShapeDtypeStruct((depth, r, c), f32)] * 4,
        input_output_aliases={4 + k: k for k in range(4)},
        compiler_params=pltpu.CompilerParams(dimension_semantics=("arbitrary",)),
    )(slots, w, m, v, *outs)


def adamw(w, g, m, v, name):
    shape = w.shape
    C = shape[-1]
    R = w.size // C
    tr = _pick(R, (256, 128, 64, 32, 16, 8)) if R * C * 4 > (1 << 20) else R

    def body(w_ref, g_ref, m_ref, v_ref, d_ref, mo_ref, vo_ref):
        d_ref[...], mo_ref[...], vo_ref[...] = _adamw_math(w_ref[...], g_ref[...], m_ref[...], v_ref[...])

    spec = pl.BlockSpec((tr, C), lambda i: (i, 0))
    res = pl.pallas_call(
        body, name=name, grid=(R // tr,), in_specs=[spec] * 4, out_specs=[spec] * 3,
        out_shape=[jax.ShapeDtypeStruct((R, C), f32)] * 3,
        compiler_params=pltpu.CompilerParams(dimension_semantics=("parallel",)),
    )(*[a.reshape(R, C) for a in (w, g, m, v)])
    return tuple(r.reshape(shape) for r in res)


def _rope_tables(T, L):
    rows = T // GRID_W
    row = jnp.repeat(jnp.arange(rows, dtype=f32), GRID_W)
    col = jnp.tile(jnp.arange(GRID_W, dtype=f32), rows)
    n_freq = HEAD // 4
    inv = ROPE_THETA ** (-jnp.arange(n_freq, dtype=f32) / n_freq)
    ang = jnp.concatenate([row[:, None] * inv, col[:, None] * inv], axis=-1)
    cos = jnp.repeat(jnp.cos(ang), 2, axis=-1)
    sin = jnp.repeat(jnp.sin(ang), 2, axis=-1) * jnp.tile(jnp.array([-1.0, 1.0], f32), HEAD // 2)
    return (jnp.concatenate([cos, jnp.ones((L, HEAD), f32)]), jnp.concatenate([sin, jnp.zeros((L, HEAD), f32)]))


def _sel(mod, l, idx):
    return jnp.stack([mod[l, :, i] for i in idx], axis=1)


def _row(a, l):
    return a[l][None, :]


def _first_segment(T, ctx):
    def seg(x, mod, p):
        xs = jnp.concatenate([x, ctx], axis=0)
        (h,) = rowwise("modnorm", f_modnorm, [xs], sels=[_sel(mod, 0, (0, 1))], pars=[_row(p["norm1_g"], 0)],
                       outs=[(x.shape[1], bf16)], n_lat=T)
        return xs, h

    return seg


def _mixer_segment(l, T, cosf, sinf):
    row = _row

    def seg(xs, h, mod, lbs, p, w_in, w_out):
        D = xs.shape[1]
        aq, ak, av, hq, hff, hfb, hi, hgt, su, sv = mm(h, w_in, "w_in%d" % l, split=IN_SIZES)
        q, k, v = rowwise("qkprep%d" % l, f_qkprep, [aq, ak, av], consts=[cosf, sinf],
                          pars=[row(p["q_norm_g"], l), row(p["k_norm_g"], l)],
                          outs=[(aq.shape[1], bf16), (ak.shape[1], bf16), (av.shape[1], bf16)], n_lat=T)
        attn = attention(q, k, v, T, "attn%d" % l)
        o_f = hgrn(hq, hff, hi, lbs[0:1], T, False, "hgrn_f%d" % l)
        o_b = hgrn(hq, hfb, hi, lbs[1:2], T, True, "hgrn_b%d" % l)
        (hg,) = rowwise("hgout%d" % l, f_hgout, [o_f, o_b, hgt], pars=[row(p["hg_norm_g"], l)], outs=[(hgt.shape[1], bf16)], n_lat=T)
        (sg,) = rowwise("sgate%d" % l, f_sgate, [su, sv],
                        pars=[row(p["sg_norm_g"], l), p["sg_w"][l]] + [p["sg_b"][l, gi][:, None] for gi in range(SG_GROUPS)],
                        outs=[(su.shape[1], bf16)], n_lat=T)
        mix = jnp.concatenate([attn, hg, sg], axis=1)
        y = mm(mix, w_out, "w_out%d" % l)
        return rowwise("resid_a%d" % l, f_resid_modnorm, [xs, y], sels=[_sel(mod, l, (2, 3, 4))], pars=[row(p["norm2_g"], l)],
                       outs=[(D, f32), (D, bf16)], n_lat=T)

    return seg


def _ffn_segment(l, depth, T, tgt):
    row = _row

    def seg(xs, h2, mod, p, w_up, w_down):
        D = xs.shape[1]
        up = mm(h2, w_up, "w_up%d" % l, out_dtype=bf16)
        act = convact(up, p["conv_w"][l], row(p["conv_b"], l), T, "convact%d" % l)
        z = mm(act, w_down, "w_down%d" % l)
        if l + 1 < depth:
            return rowwise("resid_b%d" % l, f_resid_modnorm, [xs, z],
                           sels=[jnp.concatenate([_sel(mod, l, (5,)), _sel(mod, l + 1, (0, 1))], axis=1)],
                           pars=[row(p["norm1_g"], l + 1)], outs=[(D, f32), (D, bf16)], n_lat=T)
        return rowwise("resid_final", f_resid_final, [xs[:T], z[:T]], consts=[tgt], sels=[_sel(mod, l, (5,))[0:1]],
                       pars=[p["final_norm_g"][None, :]], outs=[(1, f32)])

    return seg


SEGMENT_PARAMS = ("norm1_g", "q_norm_g", "k_norm_g", "hg_norm_g", "sg_norm_g", "sg_w", "sg_b", "norm2_g", "conv_w", "conv_b",
                  "final_norm_g")


def _loss_and_grads(p, big, x, ctx, c_all, tgt, cosf, sinf):
    T = x.shape[0]
    depth = p["norm1_g"].shape[0]
    add = lambda a, b: jax.tree.map(jnp.add, a, b)
    small = {n: p[n] for n in SEGMENT_PARAMS}

    mod, vjp_mod = jax.vjp(lambda cc, wa, ba: ada_mod(c_all, cc, wa, ba), p["c_ctx"], p["w_ada"], p["b_ada"])
    lbs, vjp_lbs = jax.vjp(lambda hg: lower_bounds([hg[:, l] for l in range(depth)]), p["hg_lower_bounds"])
    (xs, h), vjp_first = jax.vjp(_first_segment(T, ctx), x, mod, small)
    vjps = []
    for l in range(depth):
        w_in, w_out = big.get(l, "w_in", h), big.get(l, "w_out", h)
        (xs, h2), vj = jax.vjp(_mixer_segment(l, T, cosf, sinf), xs, h, mod, lbs[l], small, w_in, w_out)
        vjps.append(vj)
        w_up, w_down = big.get(l, "w_up", h2), big.get(l, "w_down", h2)
        out, vj = jax.vjp(_ffn_segment(l, depth, T, tgt), xs, h2, mod, small, w_up, w_down)
        vjps.append(vj)
        if l + 1 < depth:
            xs, h = out
    (rowloss,) = out
    loss = 0.5 * jnp.sum(rowloss)

    ct = (jnp.full(rowloss.shape, 0.5, f32),)
    d_mod, d_small, d_lbs = jnp.zeros_like(mod), jax.tree.map(jnp.zeros_like, small), [None] * depth
    for l in range(depth - 1, -1, -1):
        dxs, dh2, dm, ds, d_up, d_down = vjps[2 * l + 1](ct)
        tokens = (big.start_scatter(l, "w_up", d_up), big.start_scatter(l, "w_down", d_down))
        dxs, dh2, _ = lax.optimization_barrier((dxs, dh2, tokens))
        d_mod, d_small = d_mod + dm, add(d_small, ds)
        dxs, dh, dm, d_lbs[l], ds, d_in, d_out = vjps[2 * l]((dxs, dh2))
        tokens = (big.start_scatter(l, "w_in", d_in), big.start_scatter(l, "w_out", d_out))
        dxs, dh, _ = lax.optimization_barrier((dxs, dh, tokens))
        d_mod, d_small = d_mod + dm, add(d_small, ds)
        ct = (dxs, dh)
    dx, dm, ds = vjp_first(ct)
    d_cc, d_wada, d_bada = vjp_mod(d_mod + dm)
    (d_hg,) = vjp_lbs(tuple(d_lbs))
    grads = dict(add(d_small, ds), c_ctx=d_cc, w_ada=d_wada, b_ada=d_bada, hg_lower_bounds=d_hg)
    return loss, grads, dx


def kernel(x, c, ctx, c_ctx, w_ada, b_ada, norm1_g, w_in, q_norm_g, k_norm_g, hg_lower_bounds, hg_norm_g, sg_norm_g, sg_w, sg_b, w_out, norm2_g, w_up, conv_w, conv_b, w_down, final_norm_g, loss_target, m_c_ctx, m_w_ada, m_b_ada, m_norm1_g, m_w_in, m_q_norm_g, m_k_norm_g, m_hg_lower_bounds, m_hg_norm_g, m_sg_norm_g, m_sg_w, m_sg_b, m_w_out, m_norm2_g, m_w_up, m_conv_w, m_conv_b, m_w_down, m_final_norm_g, v_c_ctx, v_w_ada, v_b_ada, v_norm1_g, v_w_in, v_q_norm_g, v_k_norm_g, v_hg_lower_bounds, v_hg_norm_g, v_sg_norm_g, v_sg_w, v_sg_b, v_w_out, v_norm2_g, v_w_up, v_conv_w, v_conv_b, v_w_down, v_final_norm_g):
    given = dict(c_ctx=c_ctx, w_ada=w_ada, b_ada=b_ada, norm1_g=norm1_g, w_in=w_in, q_norm_g=q_norm_g, k_norm_g=k_norm_g,
                 hg_lower_bounds=hg_lower_bounds, hg_norm_g=hg_norm_g, sg_norm_g=sg_norm_g, sg_w=sg_w, sg_b=sg_b, w_out=w_out,
                 norm2_g=norm2_g, w_up=w_up, conv_w=conv_w, conv_b=conv_b, w_down=w_down, final_norm_g=final_norm_g)
    moments_m = dict(c_ctx=m_c_ctx, w_ada=m_w_ada, b_ada=m_b_ada, norm1_g=m_norm1_g, w_in=m_w_in, q_norm_g=m_q_norm_g,
                     k_norm_g=m_k_norm_g, hg_lower_bounds=m_hg_lower_bounds, hg_norm_g=m_hg_norm_g, sg_norm_g=m_sg_norm_g,
                     sg_w=m_sg_w, sg_b=m_sg_b, w_out=m_w_out, norm2_g=m_norm2_g, w_up=m_w_up, conv_w=m_conv_w, conv_b=m_conv_b,
                     w_down=m_w_down, final_norm_g=m_final_norm_g)
    moments_v = dict(c_ctx=v_c_ctx, w_ada=v_w_ada, b_ada=v_b_ada, norm1_g=v_norm1_g, w_in=v_w_in, q_norm_g=v_q_norm_g,
                     k_norm_g=v_k_norm_g, hg_lower_bounds=v_hg_lower_bounds, hg_norm_g=v_hg_norm_g, sg_norm_g=v_sg_norm_g,
                     sg_w=v_sg_w, sg_b=v_sg_b, w_out=v_w_out, norm2_g=v_norm2_g, w_up=v_w_up, conv_w=v_conv_w, conv_b=v_conv_b,
                     w_down=v_w_down, final_norm_g=v_final_norm_g)
    T, D = x.shape[1], x.shape[2]
    L = ctx.shape[1]
    me = _my_index()
    axes = ("x", "y", "c")

    c_all = allgather_small(jnp.pad(c, ((0, 7), (0, 0))), "gather_c").reshape(N_DEV, 8, D)[:, 0]
    depth, hw = hg_lower_bounds.shape[1], hg_lower_bounds.shape[2]
    cw = conv_w.shape[2]
    small = jnp.concatenate([jnp.pad(hg_lower_bounds.reshape(2 * depth, hw), ((0, 0), (0, cw - hw))), conv_w.reshape(3 * depth, cw)], axis=0)
    rows_small = small.shape[0]
    small = allgather_small(jnp.pad(small, ((0, (-rows_small) % 8), (0, 0))), "gather_small").reshape(N_DEV, -1, cw)
    hg_full = small[:, :2 * depth, :hw].reshape(N_DEV, 2, depth, hw).transpose(1, 2, 0, 3).reshape(2, depth, N_DEV * hw)
    cw_full = small[:, 2 * depth:2 * depth + 3 * depth].reshape(N_DEV, depth, 3, cw).transpose(1, 2, 0, 3).reshape(depth, 3, N_DEV * cw)

    big = BigWeights(given, me)
    p = {n: a for n, a in given.items() if n not in dict(BIG)}
    p.update(hg_lower_bounds=hg_full, conv_w=cw_full, c_ctx=c_ctx[None, :])
    cosf, sinf = _rope_tables(T, L)
    loss, gp, gx = _loss_and_grads(p, big, x[0], ctx[0], c_all, loss_target[0], cosf, sinf)
    loss = lax.psum(loss, axes)

    grads, delta, new_m, new_v = dict(gp), {}, {}, {}
    big_outs = {n: None for n, _ in BIG}
    last = gx

    def update_big_layer(l, last):
        for n in ("w_down", "w_up", "w_out", "w_in"):
            big_outs[n] = adamw_layer(l, big.contributions(l, n, last), given[n], moments_m[n], moments_v[n], big_outs[n],
                                      "adamw_%s%d" % (n, l))
            last = big_outs[n][1]
        return last

    for l in range(depth - 1, 0, -1):
        last = update_big_layer(l, last)

    partial = ['c_ctx', 'norm1_g', 'q_norm_g', 'k_norm_g', 'hg_lower_bounds', 'hg_norm_g', 'sg_norm_g', 'sg_w', 'sg_b',
               'norm2_g', 'conv_w', 'conv_b', 'final_norm_g']
    first, _ = lax.optimization_barrier((gp[partial[0]], last))
    grads.update(zip(partial, allreduce_small([first] + [gp[n] for n in partial[1:]], "reduce_small")))
    grads['c_ctx'] = grads['c_ctx'][0]
    grads['hg_lower_bounds'] = lax.dynamic_slice_in_dim(grads['hg_lower_bounds'], me * hw, hw, axis=2)
    grads['conv_w'] = lax.dynamic_slice_in_dim(grads['conv_w'], me * cw, cw, axis=2)
    for n in [w for w in WEIGHTS if w not in dict(BIG)]:
        delta[n], new_m[n], new_v[n] = adamw(given[n], grads[n], moments_m[n], moments_v[n], "adamw_" + n)
    last = update_big_layer(0, delta["w_ada"])
    for n, _ in BIG:
        grads[n], delta[n], new_m[n], new_v[n] = big_outs[n]
    return (loss, gx[None], *[grads[n] for n in WEIGHTS], *[delta[n] for n in WEIGHTS],
            *[new_m[n] for n in WEIGHTS], *[new_v[n] for n in WEIGHTS])
```

```python
import functools

import jax
import jax.numpy as jnp
from jax import lax
from jax.experimental import pallas as pl
from jax.experimental.pallas import tpu as pltpu

f32 = jnp.float32
bf16 = jnp.bfloat16
HI = lax.Precision.HIGHEST
MESH = pl.DeviceIdType.MESH

EPS = 1e-6
F_MIN = 1e-30
GRID_W = 64
ROPE_THETA = 10000.0
HEAD = 128
ATTN_HEADS, ATTN_KV = 8, 2
ATTN_GROUP = ATTN_HEADS // ATTN_KV
HG_HEADS = 4
SG_GROUPS = 4
SG_CHUNK = 128
HG_CHUNK = 16
HG_GROUP = 16
IN_SIZES = (1024, 256, 256, 512, 512, 512, 512, 512, 512, 512)
N_DEV = 8
ROW_BLOCK = 256
MAX_TK = 2816
ADAM_LR, ADAM_B1, ADAM_B2, ADAM_EPS, ADAM_WD, ADAM_STEP = 0.001, 0.9, 0.999, 1e-08, 0.01, 10

WEIGHTS = ['c_ctx', 'w_ada', 'b_ada', 'norm1_g', 'w_in', 'q_norm_g', 'k_norm_g', 'hg_lower_bounds', 'hg_norm_g',
           'sg_norm_g', 'sg_w', 'sg_b', 'w_out', 'norm2_g', 'w_up', 'conv_w', 'conv_b', 'w_down', 'final_norm_g']


def _pick(dim, cands):
    for t in cands:
        if dim % t == 0:
            return t
    return dim


def _my_index():
    return 4 * lax.axis_index("x") + 2 * lax.axis_index("y") + lax.axis_index("c")


def _mm_call(a, b, mode, out_dtype, name):
    if mode == "nn":
        (M, K), N = a.shape, b.shape[1]
    elif mode == "nt":
        (M, K), N = a.shape, b.shape[0]
    else:
        (K, M), N = a.shape, b.shape[1]
    tm = _pick(M, (1088, 1024, 512, 256, 128))
    tn = _pick(N, (1024, 512, 256, 128))
    tk = K if K <= MAX_TK else _pick(K, (2816, 2560, 2176, 2048, 1408, 1088, 1024, 512, 256, 128))
    nk = K // tk
    dims = {"nn": (((1,), (0,)), ((), ())), "nt": (((1,), (1,)), ((), ())), "tn": (((0,), (0,)), ((), ()))}[mode]

    def body(a_ref, b_ref, o_ref, *acc):
        prod = lax.dot_general(a_ref[...].astype(bf16), b_ref[...].astype(bf16), dims, preferred_element_type=f32)
        if nk == 1:
            o_ref[...] = prod.astype(o_ref.dtype)
            return
        k = pl.program_id(2)

        @pl.when(k == 0)
        def _():
            acc[0][...] = prod

        @pl.when((k > 0) & (k < nk - 1))
        def _():
            acc[0][...] += prod

        @pl.when(k == nk - 1)
        def _():
            o_ref[...] = (acc[0][...] + prod).astype(o_ref.dtype)

    a_spec = pl.BlockSpec((tk, tm), lambda i, j, k: (k, i)) if mode == "tn" else pl.BlockSpec((tm, tk), lambda i, j, k: (i, k))
    b_spec = pl.BlockSpec((tn, tk), lambda i, j, k: (j, k)) if mode == "nt" else pl.BlockSpec((tk, tn), lambda i, j, k: (k, j))
    return pl.pallas_call(
        body, name=name, grid=(M // tm, N // tn, nk),
        in_specs=[a_spec, b_spec], out_specs=pl.BlockSpec((tm, tn), lambda i, j, k: (i, j)),
        out_shape=jax.ShapeDtypeStruct((M, N), out_dtype),
        scratch_shapes=[pltpu.VMEM((tm, tn), f32)] if nk > 1 else [],
        compiler_params=pltpu.CompilerParams(dimension_semantics=("parallel", "parallel", "arbitrary")),
    )(a, b)


def mm(a, w, name, out_dtype=f32, split=None):
    def parts(y):
        if split is None:
            return y
        offs = [sum(split[:i]) for i in range(len(split))]
        return tuple(y[:, o:o + s] for o, s in zip(offs, split))

    @jax.custom_vjp
    def op(a, w):
        return parts(_mm_call(a, w, "nn", out_dtype, name + "_fwd"))

    def fwd(a, w):
        return op(a, w), (a, w)

    def bwd(res, dy):
        a, w = res
        dy = dy.astype(bf16) if split is None else jnp.concatenate([d.astype(bf16) for d in dy], axis=1)
        return _mm_call(dy, w, "nt", a.dtype, name + "_bwd_a"), _mm_call(a, dy, "tn", w.dtype, name + "_bwd_w")

    op.defvjp(fwd, bwd)
    return op(a, w)


def _rowwise_specs(rows, consts, sels, pars, tb, nlb):
    specs = [pl.BlockSpec((tb, a.shape[1]), lambda i: (i, 0)) for a in (*rows, *consts)]
    specs += [pl.BlockSpec((None,) + a.shape[1:], lambda i: (jnp.where(i >= nlb, 1, 0), 0, 0)) for a in sels]
    specs += [pl.BlockSpec(a.shape, functools.partial(lambda i, n: (0,) * n, n=a.ndim)) for a in pars]
    return specs


def rowwise(name, f, rows, consts=(), sels=(), pars=(), outs=(), n_lat=None, tb=ROW_BLOCK):
    rows, consts, sels, pars = tuple(rows), tuple(consts), tuple(sels), tuple(pars)
    R = rows[0].shape[0]
    nb = R // tb
    nlb = nb if n_lat is None else n_lat // tb
    n_in = len(rows) + len(consts) + len(sels) + len(pars)
    n_out = len(outs)
    out_dtypes = [d for _, d in outs]
    out_specs = [pl.BlockSpec((tb, w), lambda i: (i, 0)) for w, _ in outs]
    out_shape = [jax.ShapeDtypeStruct((R, w), d) for w, d in outs]
    sem = pltpu.CompilerParams(dimension_semantics=("arbitrary",))

    def run_fwd(rows, consts, sels, pars):
        def body(*refs):
            res = f(*[r[...] for r in refs[:n_in]])
            for o_ref, r in zip(refs[n_in:], res):
                o_ref[...] = r.astype(o_ref.dtype)

        return tuple(pl.pallas_call(
            body, name=name + "_fwd", grid=(nb,), in_specs=_rowwise_specs(rows, consts, sels, pars, tb, nlb),
            out_specs=out_specs, out_shape=out_shape, compiler_params=sem,
        )(*rows, *consts, *sels, *pars))

    def run_bwd(rows, consts, sels, pars, cts):
        nr, nc, ns, npar = len(rows), len(consts), len(sels), len(pars)

        def body(*refs):
            i = pl.program_id(0)
            ins = [r[...] for r in refs[:n_in]]
            ct = tuple(r[...] for r in refs[n_in:n_in + n_out])
            o_refs = refs[n_in + n_out:]
            cvals = ins[nr:nr + nc]

            def g(*d):
                res = f(*d[:nr], *cvals, *d[nr:])
                return tuple(r.astype(t) for r, t in zip(res, out_dtypes))

            _, vjp = jax.vjp(g, *ins[:nr], *ins[nr + nc:])
            grads = vjp(ct)
            for k in range(nr):
                o_refs[k][...] = grads[k].astype(o_refs[k].dtype)
            for k in range(nr, nr + ns + npar):
                first = (i == 0) | (i == nlb) if k < nr + ns else (i == 0)
                gk = grads[k].astype(f32)

                @pl.when(first)
                def _(k=k, gk=gk):
                    o_refs[k][...] = gk

                @pl.when(jnp.logical_not(first))
                def _(k=k, gk=gk):
                    o_refs[k][...] += gk

        in_specs = _rowwise_specs(rows, consts, sels, pars, tb, nlb) + out_specs
        o_specs = [pl.BlockSpec((tb, a.shape[1]), lambda i: (i, 0)) for a in rows]
        o_specs += [pl.BlockSpec((None,) + a.shape[1:], lambda i: (jnp.where(i >= nlb, 1, 0), 0, 0)) for a in sels]
        o_specs += [pl.BlockSpec(a.shape, functools.partial(lambda i, n: (0,) * n, n=a.ndim)) for a in pars]
        o_shape = [jax.ShapeDtypeStruct(a.shape, a.dtype) for a in rows]
        o_shape += [jax.ShapeDtypeStruct(a.shape, f32) for a in (*sels, *pars)]
        res = pl.pallas_call(
            body, name=name + "_bwd", grid=(nb,), in_specs=in_specs, out_specs=o_specs, out_shape=o_shape,
            compiler_params=sem,
        )(*rows, *consts, *sels, *pars, *cts)
        return tuple(res[:nr]), tuple(res[nr:nr + ns]), tuple(res[nr + ns:])

    @jax.custom_vjp
    def op(rows, consts, sels, pars):
        return run_fwd(rows, consts, sels, pars)

    def fwd(rows, consts, sels, pars):
        return op(rows, consts, sels, pars), (rows, consts, sels, pars)

    def bwd(res, cts):
        rows, consts, sels, pars = res
        drows, dsels, dpars = run_bwd(rows, consts, sels, pars, tuple(cts))
        return drows, tuple(jnp.zeros_like(c) for c in consts), dsels, dpars

    op.defvjp(fwd, bwd)
    return op(rows, consts, sels, pars)


def _rms(x, g):
    return x * lax.rsqrt(jnp.mean(x * x, axis=-1, keepdims=True) + EPS) * g


def _silu(x):
    return x * jax.nn.sigmoid(x)


def f_modnorm(x, mods, g):
    return (_rms(x, g) * (1.0 + mods[1:2]) + mods[0:1],)


def f_resid_modnorm(x, y, mods, g):
    xn = x + mods[0:1] * y
    return xn, _rms(xn, g) * (1.0 + mods[2:3]) + mods[1:2]


def f_resid_final(x, y, tgt, mods, g):
    xn = x + mods[0:1] * y
    err = _rms(xn, g) - tgt
    return (jnp.mean(err * err, axis=-1, keepdims=True),)


def f_qkprep(aq, ak, av, cosf, sinf, qg, kg):
    r = lax.broadcasted_iota(jnp.int32, (HEAD, HEAD), 0)
    c = lax.broadcasted_iota(jnp.int32, (HEAD, HEAD), 1)
    swap = jnp.where((r ^ 1) == c, 1.0, 0.0).astype(f32)

    def head(xh, g):
        y = _rms(xh, g)
        ys = jnp.dot(y, swap, precision=HI, preferred_element_type=f32)
        return y * cosf + ys * sinf

    q = jnp.concatenate([head(aq[:, h * HEAD:(h + 1) * HEAD], qg) for h in range(ATTN_HEADS)], axis=1)
    k = jnp.concatenate([head(ak[:, h * HEAD:(h + 1) * HEAD], kg) for h in range(ATTN_KV)], axis=1)
    return q, k, av


def f_hgout(of, ob, gt, g):
    o = of + ob
    y = jnp.concatenate([_rms(o[:, h * HEAD:(h + 1) * HEAD], g) for h in range(HG_HEADS)], axis=1)
    return (y * _silu(gt),)


def f_sgate(u, v, g, w, b0, b1, b2, b3):
    u = jax.nn.gelu(u)
    v = jax.nn.gelu(v)
    bs = (b0, b1, b2, b3)
    cols = []
    for gi in range(SG_GROUPS):
        sl = slice(gi * HEAD, (gi + 1) * HEAD)
        vg = _rms(v[:, sl], g[:, sl])
        parts = []
        for n in range(v.shape[0] // SG_CHUNK):
            vc = vg[n * SG_CHUNK:(n + 1) * SG_CHUNK]
            parts.append(jnp.dot(w[gi].astype(bf16), vc.astype(bf16), preferred_element_type=f32) + bs[gi])
        cols.append(jnp.concatenate(parts, axis=0))
    return (u * jnp.concatenate(cols, axis=1),)


def attention(q, k, v, T, name, tq=ROW_BLOCK):
    NT = q.shape[0]
    nqb, nlb = NT // tq, T // tq
    scale = HEAD ** -0.5
    q_spec = pl.BlockSpec((tq, HEAD), lambda kv, g, i: (i, kv * ATTN_GROUP + g))
    kv_spec = pl.BlockSpec((NT, HEAD), lambda kv, g, i: (0, kv))
    lse_spec = pl.BlockSpec((None, tq, 1), lambda kv, g, i: (kv * ATTN_GROUP + g, i, 0))
    grid = (ATTN_KV, ATTN_GROUP, nqb)
    nt_dims = (((1,), (1,)), ((), ()))
    tn_dims = (((0,), (0,)), ((), ()))

    def on_keys(i, fn):
        @pl.when(i < nlb)
        def _():
            fn(pl.ds(0, NT))

        @pl.when(i >= nlb)
        def _():
            fn(pl.ds(T, NT - T))

    def run_fwd(q, k, v):
        def body(q_ref, k_ref, v_ref, o_ref, lse_ref):
            def run(rows):
                s = lax.dot_general(q_ref[...], k_ref[rows, :], nt_dims, preferred_element_type=f32)
                m = jnp.max(s, axis=-1, keepdims=True) * scale
                p = jnp.exp(s * scale - m)
                l = jnp.sum(p, axis=-1, keepdims=True)
                o = jnp.dot(p.astype(bf16), v_ref[rows, :], preferred_element_type=f32) / l
                o_ref[...] = o.astype(o_ref.dtype)
                lse_ref[...] = m + jnp.log(l)

            on_keys(pl.program_id(2), run)

        return pl.pallas_call(
            body, name=name + "_fwd", grid=grid, in_specs=[q_spec, kv_spec, kv_spec], out_specs=[q_spec, lse_spec],
            out_shape=[jax.ShapeDtypeStruct(q.shape, bf16), jax.ShapeDtypeStruct((ATTN_HEADS, NT, 1), f32)],
            compiler_params=pltpu.CompilerParams(dimension_semantics=("parallel", "parallel", "arbitrary")),
        )(q, k, v)

    def run_bwd(q, k, v, lse, do):
        def body(q_ref, k_ref, v_ref, lse_ref, do_ref, dq_ref, dk_ref, dv_ref):
            g, i = pl.program_id(1), pl.program_id(2)

            @pl.when((g == 0) & (i == 0))
            def _():
                dk_ref[...] = jnp.zeros_like(dk_ref)
                dv_ref[...] = jnp.zeros_like(dv_ref)

            def run(rows):
                qb, kb, vb, dob = q_ref[...], k_ref[rows, :], v_ref[rows, :], do_ref[...]
                s = lax.dot_general(qb, kb, nt_dims, preferred_element_type=f32)
                p = jnp.exp(s * scale - lse_ref[...])
                dp = lax.dot_general(dob, vb, nt_dims, preferred_element_type=f32)
                ds = (p * (dp - jnp.sum(p * dp, axis=-1, keepdims=True)) * scale).astype(bf16)
                dq_ref[...] = jnp.dot(ds, kb, preferred_element_type=f32).astype(dq_ref.dtype)
                dk_ref[rows, :] += lax.dot_general(ds, qb, tn_dims, preferred_element_type=f32)
                dv_ref[rows, :] += lax.dot_general(p.astype(bf16), dob, tn_dims, preferred_element_type=f32)

            on_keys(i, run)

        return pl.pallas_call(
            body, name=name + "_bwd", grid=grid, in_specs=[q_spec, kv_spec, kv_spec, lse_spec, q_spec],
            out_specs=[q_spec, kv_spec, kv_spec],
            out_shape=[jax.ShapeDtypeStruct(q.shape, bf16), jax.ShapeDtypeStruct(k.shape, f32), jax.ShapeDtypeStruct(v.shape, f32)],
            compiler_params=pltpu.CompilerParams(dimension_semantics=("parallel", "arbitrary", "arbitrary")),
        )(q, k, v, lse, do)

    @jax.custom_vjp
    def op(q, k, v):
        return run_fwd(q, k, v)[0]

    def fwd(q, k, v):
        o, lse = run_fwd(q, k, v)
        return o, (q, k, v, lse)

    def bwd(res, do):
        dq, dk, dv = run_bwd(*res, do)
        return dq, dk.astype(bf16), dv.astype(bf16)

    op.defvjp(fwd, bwd)
    return op(q, k, v)


def _bdot(a, b, ca, cb):
    fa, fb = 3 - ca, 3 - cb

    def dot(x, y, cx, cy):
        return lax.dot_general(x.astype(bf16), y.astype(bf16), (((cx,), (cy,)), ((0,), (0,))), preferred_element_type=f32)

    @jax.custom_vjp
    def op(a, b):
        return dot(a, b, ca, cb)

    def fwd(a, b):
        return op(a, b), (a, b)

    def bwd(res, ct):
        a, b = res
        da = dot(ct, b, 2, fb) if ca == 2 else dot(b, ct, fb, 2)
        db = dot(a, ct, fa, 1) if cb == 1 else dot(ct, a, 1, fa)
        return da, db

    op.defvjp(fwd, bwd)
    return op(a, b)


def _chunk_cumsum(x, rev):
    def impl(x, rev):
        n = x.shape[0]
        pos = lax.broadcasted_iota(jnp.int32, x.shape, 0) % HG_CHUNK
        s = 1
        while s < HG_CHUNK:
            if rev:
                x = x + jnp.where(pos < HG_CHUNK - s, pltpu.roll(x, n - s, 0), 0.0)
            else:
                x = x + jnp.where(pos >= s, pltpu.roll(x, s, 0), 0.0)
            s *= 2
        return x

    @jax.custom_vjp
    def op(x):
        return impl(x, rev)

    op.defvjp(lambda x: (op(x), None), lambda _, ct: (impl(ct, not rev),))
    return op(x)


def _hg_group(St, hq, hf, hi, lb, *, rev):
    G, C = HG_GROUP, HG_CHUNK
    R = G * C
    q = _silu(hq)
    f = lb + (1.0 - lb) * jax.nn.sigmoid(hf)
    logf = jnp.log(jnp.maximum(f, F_MIN))
    kk = (1.0 - lb) * jax.nn.sigmoid(-hf)
    b3 = _chunk_cumsum(logf, rev).reshape(G, C, HEAD)
    q3, k3, v3 = q.reshape(G, C, HEAD), kk.reshape(G, C, HEAD), hi.reshape(G, C, HEAD)
    btot = jnp.sum(logf.reshape(G, C, HEAD), axis=1)
    tt = lax.broadcasted_iota(jnp.int32, (G, C, C, HEAD), 1)
    ss = lax.broadcasted_iota(jnp.int32, (G, C, C, HEAD), 2)
    mask = (ss >= tt) if rev else (ss <= tt)
    diff = b3[:, :, None, :] - b3[:, None, :, :]
    dec = jnp.where(mask, jnp.exp(jnp.where(mask, diff, 0.0)), 0.0)
    scores = jnp.sum(q3[:, :, None, :] * k3[:, None, :, :] * dec, axis=-1)
    o_intra = _bdot(scores, v3, 2, 1)
    q_dec = q3 * jnp.exp(b3)
    k_dec = k3 * jnp.exp(btot[:, None, :] - b3)
    kvt = _bdot(v3, k_dec, 1, 1)
    dl = jnp.exp(btot)
    states = [None] * G
    for g in (range(G - 1, -1, -1) if rev else range(G)):
        states[g] = St
        St = St * dl[g:g + 1, :] + kvt[g]
    o_inter = _bdot(q_dec, jnp.stack(states), 2, 2)
    return St, (o_intra + o_inter).reshape(R, HEAD)


def hgrn(hq, hf, hi, lb, T, rev, name):
    NT, W = hq.shape
    R = HG_GROUP * HG_CHUNK
    n_lat, n_ctx = T // R, (NT - T) // R
    nG = n_lat + n_ctx

    def group_of(j):
        if rev:
            return jnp.where(j < n_ctx, nG - 1 - j, n_lat - 1 - (j - n_ctx))
        return jnp.where(j < n_ctx, n_lat + j, j - n_ctx)

    def rows_of(j):
        return pl.ds(pl.multiple_of(group_of(j) * R, R), R)

    col_spec = pl.BlockSpec((NT, HEAD), lambda h: (0, h))
    lb_spec = pl.BlockSpec((1, HEAD), lambda h: (0, h))
    st_spec = pl.BlockSpec((None, nG, HEAD, HEAD), lambda h: (h, 0, 0, 0))
    sem = pltpu.CompilerParams(dimension_semantics=("parallel",))

    def run_fwd(hq, hf, hi, lb):
        def body(hq_ref, hf_ref, hi_ref, lb_ref, o_ref, st_ref):
            def step(j, St):
                st_ref[j] = St
                rows = rows_of(j)
                St, o = _hg_group(St, hq_ref[rows, :], hf_ref[rows, :], hi_ref[rows, :], lb_ref[...], rev=rev)
                o_ref[rows, :] = o
                return St

            lax.fori_loop(0, nG, step, jnp.zeros((HEAD, HEAD), f32))

        return pl.pallas_call(
            body, name=name + "_fwd", grid=(W // HEAD,), in_specs=[col_spec, col_spec, col_spec, lb_spec],
            out_specs=[col_spec, st_spec],
            out_shape=[jax.ShapeDtypeStruct((NT, W), f32), jax.ShapeDtypeStruct((W // HEAD, nG, HEAD, HEAD), f32)],
            compiler_params=sem,
        )(hq, hf, hi, lb)

    def run_bwd(hq, hf, hi, lb, st, do):
        def body(hq_ref, hf_ref, hi_ref, lb_ref, st_ref, do_ref, dq_ref, df_ref, di_ref, dlb_ref):
            def step(jj, carry):
                dS, dlb = carry
                j = nG - 1 - jj
                rows = rows_of(j)
                _, vjp = jax.vjp(functools.partial(_hg_group, rev=rev), st_ref[j], hq_ref[rows, :], hf_ref[rows, :],
                                 hi_ref[rows, :], lb_ref[...])
                dS, dq, df, di, dl = vjp((dS, do_ref[rows, :]))
                dq_ref[rows, :] = dq
                df_ref[rows, :] = df
                di_ref[rows, :] = di
                return dS, dlb + dl

            _, dlb = lax.fori_loop(0, nG, step, (jnp.zeros((HEAD, HEAD), f32), jnp.zeros((1, HEAD), f32)))
            dlb_ref[...] = dlb

        return pl.pallas_call(
            body, name=name + "_bwd", grid=(W // HEAD,),
            in_specs=[col_spec, col_spec, col_spec, lb_spec, st_spec, col_spec],
            out_specs=[col_spec, col_spec, col_spec, lb_spec],
            out_shape=[jax.ShapeDtypeStruct((NT, W), f32)] * 3 + [jax.ShapeDtypeStruct((1, W), f32)],
            compiler_params=sem,
        )(hq, hf, hi, lb, st, do)

    @jax.custom_vjp
    def op(hq, hf, hi, lb):
        return run_fwd(hq, hf, hi, lb)[0]

    def fwd(hq, hf, hi, lb):
        o, st = run_fwd(hq, hf, hi, lb)
        return o, (hq, hf, hi, lb, st)

    def bwd(res, do):
        return tuple(run_bwd(*res, do))

    op.defvjp(fwd, bwd)
    return op(hq, hf, hi, lb)


def lower_bounds(params):
    n = len(params)

    def f(*a):
        m = functools.reduce(jnp.maximum, a)
        e = [jnp.exp(x - m) for x in a]
        s = functools.reduce(lambda u, v: u + v, e)
        p = [x / s for x in e]
        out, run = [], jnp.zeros_like(p[0])
        for l in range(n):
            out.append(run)
            run = run + p[l]
        return tuple(out[l] + p[l] - p[0] for l in range(n))

    shape = [jax.ShapeDtypeStruct(params[0].shape, f32)] * n

    @jax.custom_vjp
    def op(*a):
        def body(*refs):
            for o_ref, r in zip(refs[n:], f(*[x[...] for x in refs[:n]])):
                o_ref[...] = r
        return tuple(pl.pallas_call(body, name="lower_bounds_fwd", out_shape=shape)(*a))

    def fwd(*a):
        return op(*a), a

    def bwd(a, cts):
        def body(*refs):
            _, vjp = jax.vjp(f, *[x[...] for x in refs[:n]])
            for o_ref, r in zip(refs[2 * n:], vjp(tuple(x[...] for x in refs[n:2 * n]))):
                o_ref[...] = r
        return tuple(pl.pallas_call(body, name="lower_bounds_bwd", out_shape=shape)(*a, *cts))

    op.defvjp(fwd, bwd)
    return op(*params)


def _shift_rows(x, d, T):
    n = x.shape[0]
    t = lax.broadcasted_iota(jnp.int32, x.shape, 0)
    y = pltpu.roll(x, d % n, 0)
    edge = ((t == 0) | (t == T)) if d == 1 else ((t == T - 1) | (t == n - 1))
    return jnp.where(edge, 0.0, y)


def _conv(x, w, b, T):
    return b + w[0:1] * _shift_rows(x, 1, T) + w[1:2] * x + w[2:3] * _shift_rows(x, -1, T)


def convact(up, cw, cb, T, name, tc=128):
    NT, F2 = up.shape
    F = F2 // 2
    tc = _pick(F, (tc, 128))
    nf = F // tc
    g_spec = lambda r: pl.BlockSpec((r, tc), lambda j: (0, j))
    v_spec = lambda r: pl.BlockSpec((r, tc), lambda j: (0, j + nf))
    sem = pltpu.CompilerParams(dimension_semantics=("parallel",))

    def run_fwd(up, cw, cb):
        def body(xg_ref, xv_ref, wg_ref, wv_ref, bg_ref, bv_ref, o_ref):
            yg = _conv(xg_ref[...].astype(f32), wg_ref[...], bg_ref[...], T)
            yv = _conv(xv_ref[...].astype(f32), wv_ref[...], bv_ref[...], T)
            o_ref[...] = (_silu(yg) * yv).astype(o_ref.dtype)

        return pl.pallas_call(
            body, name=name + "_fwd", grid=(nf,),
            in_specs=[g_spec(NT), v_spec(NT), g_spec(3), v_spec(3), g_spec(1), v_spec(1)], out_specs=g_spec(NT),
            out_shape=jax.ShapeDtypeStruct((NT, F), bf16), compiler_params=sem,
        )(up, up, cw, cw, cb, cb)

    def run_bwd(up, cw, cb, dact):
        def body(xg_ref, xv_ref, wg_ref, wv_ref, bg_ref, bv_ref, da_ref, dxg_ref, dxv_ref, dwg_ref, dwv_ref, dbg_ref, dbv_ref):
            xg, xv, wg, wv = xg_ref[...].astype(f32), xv_ref[...].astype(f32), wg_ref[...], wv_ref[...]
            yg = _conv(xg, wg, bg_ref[...], T)
            yv = _conv(xv, wv, bv_ref[...], T)
            da = da_ref[...].astype(f32)
            sg = jax.nn.sigmoid(yg)
            dyv = da * yg * sg
            dyg = da * yv * sg * (1.0 + yg * (1.0 - sg))
            for x, w, dy, dx_ref, dw_ref, db_ref in ((xg, wg, dyg, dxg_ref, dwg_ref, dbg_ref), (xv, wv, dyv, dxv_ref, dwv_ref, dbv_ref)):
                dx_ref[...] = (w[0:1] * _shift_rows(dy, -1, T) + w[1:2] * dy + w[2:3] * _shift_rows(dy, 1, T)).astype(dx_ref.dtype)
                dw_ref[...] = jnp.concatenate([
                    jnp.sum(dy * _shift_rows(x, 1, T), axis=0, keepdims=True),
                    jnp.sum(dy * x, axis=0, keepdims=True),
                    jnp.sum(dy * _shift_rows(x, -1, T), axis=0, keepdims=True)], axis=0)
                db_ref[...] = jnp.sum(dy, axis=0, keepdims=True)

        return pl.pallas_call(
            body, name=name + "_bwd", grid=(nf,),
            in_specs=[g_spec(NT), v_spec(NT), g_spec(3), v_spec(3), g_spec(1), v_spec(1), g_spec(NT)],
            out_specs=[g_spec(NT), g_spec(NT), g_spec(3), g_spec(3), g_spec(1), g_spec(1)],
            out_shape=[jax.ShapeDtypeStruct((NT, F), up.dtype)] * 2 + [jax.ShapeDtypeStruct((3, F), f32)] * 2 + [jax.ShapeDtypeStruct((1, F), f32)] * 2,
            compiler_params=sem,
        )(up, up, cw, cw, cb, cb, dact)

    @jax.custom_vjp
    def op(up, cw, cb):
        return run_fwd(up, cw, cb)

    def fwd(up, cw, cb):
        return op(up, cw, cb), (up, cw, cb)

    def bwd(res, dact):
        dxg, dxv, dwg, dwv, dbg, dbv = run_bwd(*res, dact)
        return (jnp.concatenate([dxg, dxv], axis=1), jnp.concatenate([dwg, dwv], axis=1), jnp.concatenate([dbg, dbv], axis=1))

    op.defvjp(fwd, bwd)
    return op(up, cw, cb)


def _peers():
    x, y, c = lax.axis_index("x"), lax.axis_index("y"), lax.axis_index("c")
    return (x, y, c), [(x, y, 1 - c), (1 - x, y, c), (x, 1 - y, c), (1 - x, 1 - y, c),
                       (1 - x, y, 1 - c), (x, 1 - y, 1 - c), (1 - x, 1 - y, 1 - c)]


def _index(dev):
    return 4 * dev[0] + 2 * dev[1] + dev[2]


def allgather_small(x, name):
    m, n = x.shape

    def body(x_ref, out_ref, send_sems, recv_sems, local_sem):
        me, peers = _peers()

        def rows(dev):
            return out_ref.at[pl.ds(pl.multiple_of(_index(dev) * m, 8), m), :]

        mine = pltpu.make_async_copy(x_ref, rows(me), local_sem)
        mine.start()
        sends = [pltpu.make_async_remote_copy(src_ref=x_ref, dst_ref=rows(me), send_sem=send_sems.at[k], recv_sem=recv_sems.at[k],
                                              device_id=p, device_id_type=MESH) for k, p in enumerate(peers)]
        for cp in sends:
            cp.start()
        for k, p in enumerate(peers):
            pltpu.make_async_remote_copy(src_ref=x_ref, dst_ref=rows(p), send_sem=send_sems.at[k], recv_sem=recv_sems.at[k],
                                         device_id=p, device_id_type=MESH).wait_recv()
        for cp in sends:
            cp.wait_send()
        mine.wait()

    return pl.pallas_call(
        body, name=name, out_shape=jax.ShapeDtypeStruct((N_DEV * m, n), x.dtype),
        in_specs=[pl.BlockSpec(memory_space=pltpu.VMEM)], out_specs=pl.BlockSpec(memory_space=pltpu.VMEM),
        scratch_shapes=[pltpu.SemaphoreType.DMA((7,)), pltpu.SemaphoreType.DMA((7,)), pltpu.SemaphoreType.DMA],
    )(x)


HBM_SPEC = pl.BlockSpec(memory_space=pltpu.HBM)
SEM_SPEC = pl.BlockSpec(memory_space=pltpu.SEMAPHORE)
SPLIT_PARAMS = dict(compiler_params=pltpu.CompilerParams(has_side_effects=pltpu.SideEffectType.DATAFLOW_SIDE_EFFECTING))
N_PEERS = N_DEV - 1


def _part(ref, kind, j, width):
    if kind == "col":
        return ref.at[:, pl.ds(pl.multiple_of(j * width, 128), width)]
    return ref.at[pl.ds(pl.multiple_of(j * width, 8), width), :]


def _in_hbm(a):
    return pltpu.with_memory_space_constraint(a, pltpu.HBM)


def gather_start(shards, lands, kinds):
    n = len(shards)
    widths = [s.shape[1] if k == "col" else s.shape[0] for s, k in zip(shards, kinds)]

    def body(*refs):
        srcs, lnds, send_sems, recv_sems, token = refs[:n], refs[n:2 * n], refs[2 * n], refs[2 * n + 1], refs[-1]
        me, peers = _peers()
        for a in range(n):
            for k, p in enumerate(peers):
                pltpu.make_async_remote_copy(
                    src_ref=srcs[a], dst_ref=_part(lnds[a], kinds[a], _index(me), widths[a]),
                    send_sem=send_sems.at[N_PEERS * a + k], recv_sem=recv_sems.at[N_PEERS * a + k],
                    device_id=p, device_id_type=MESH).start()
        token[...] = jnp.zeros_like(token)

    res = pl.pallas_call(
        body, name="gather_start",
        out_shape=(pltpu.SemaphoreType.DMA((N_PEERS * n,)), pltpu.SemaphoreType.DMA((N_PEERS * n,)),
                   *[pltpu.HBM(a.shape, a.dtype) for a in (*shards, *lands)], jax.ShapeDtypeStruct((8, 128), f32)),
        in_specs=[HBM_SPEC] * (2 * n), out_specs=(SEM_SPEC, SEM_SPEC, *[HBM_SPEC] * (2 * n), pl.BlockSpec(memory_space=pltpu.VMEM)),
        input_output_aliases={i: 2 + i for i in range(2 * n)}, **SPLIT_PARAMS,
    )(*[_in_hbm(a) for a in (*shards, *lands)])
    return res[0], res[1], res[2:2 + n], res[2 + n:2 + 2 * n], res[-1]


def gather_wait(a, shard, land, kind, send_sems, recv_sems, after, name):
    width = shard.shape[1] if kind == "col" else shard.shape[0]

    def body(src_ref, land_ref, send_ref, recv_ref, after_ref, src_out, land_out):
        _, peers = _peers()
        for k, p in enumerate(peers):
            cp = pltpu.make_async_remote_copy(
                src_ref=src_ref, dst_ref=_part(land_ref, kind, _index(p), width),
                send_sem=send_ref.at[N_PEERS * a + k], recv_sem=recv_ref.at[N_PEERS * a + k], device_id=p, device_id_type=MESH)
            cp.wait_send()
            cp.wait_recv()

    return pl.pallas_call(
        body, name=name, out_shape=(pltpu.HBM(shard.shape, shard.dtype), pltpu.HBM(land.shape, land.dtype)),
        in_specs=(HBM_SPEC, HBM_SPEC, SEM_SPEC, SEM_SPEC, pl.BlockSpec(memory_space=pl.ANY)), out_specs=(HBM_SPEC, HBM_SPEC),
        input_output_aliases={0: 0, 1: 1}, **SPLIT_PARAMS,
    )(shard, land, send_sems, recv_sems, after)[1]


def scatter_start(full, slots, kind, name):
    width = slots.shape[2] if kind == "col" else slots.shape[1]

    def body(full_ref, slots_ref, send_sems, recv_sems, full_out, slots_out, token):
        me, peers = _peers()
        for k, p in enumerate(peers):
            pltpu.make_async_remote_copy(
                src_ref=_part(full_ref, kind, _index(p), width), dst_ref=slots_ref.at[_index(me)],
                send_sem=send_sems.at[k], recv_sem=recv_sems.at[k], device_id=p, device_id_type=MESH).start()
        token[...] = jnp.zeros_like(token)

    return pl.pallas_call(
        body, name=name,
        out_shape=(pltpu.SemaphoreType.DMA((N_PEERS,)), pltpu.SemaphoreType.DMA((N_PEERS,)), pltpu.HBM(full.shape, full.dtype),
                   pltpu.HBM(slots.shape, slots.dtype), jax.ShapeDtypeStruct((8, 128), f32)),
        in_specs=(HBM_SPEC, HBM_SPEC), out_specs=(SEM_SPEC, SEM_SPEC, HBM_SPEC, HBM_SPEC, pl.BlockSpec(memory_space=pltpu.VMEM)),
        input_output_aliases={0: 2, 1: 3}, **SPLIT_PARAMS,
    )(_in_hbm(full), _in_hbm(slots))


def scatter_wait(full, slots, kind, send_sems, recv_sems, after, name):
    width = slots.shape[2] if kind == "col" else slots.shape[1]

    def body(full_ref, slots_ref, send_ref, recv_ref, after_ref, full_out, slots_out):
        me, peers = _peers()
        for k, p in enumerate(peers):
            cp = pltpu.make_async_remote_copy(
                src_ref=_part(full_ref, kind, _index(p), width), dst_ref=slots_ref.at[_index(p)],
                send_sem=send_ref.at[k], recv_sem=recv_ref.at[k], device_id=p, device_id_type=MESH)
            cp.wait_send()
            cp.wait_recv()

    return pl.pallas_call(
        body, name=name, out_shape=(pltpu.HBM(full.shape, full.dtype), pltpu.HBM(slots.shape, slots.dtype)),
        in_specs=(HBM_SPEC, HBM_SPEC, SEM_SPEC, SEM_SPEC, pl.BlockSpec(memory_space=pl.ANY)), out_specs=(HBM_SPEC, HBM_SPEC),
        input_output_aliases={0: 0, 1: 1}, **SPLIT_PARAMS,
    )(full, slots, send_sems, recv_sems, after)[1]


def sum_slots(x, name):
    _, R, C = x.shape
    tr = _pick(R, (256, 128, 64, 32, 16, 8))

    def body(x_ref, o_ref):
        acc = x_ref[0].astype(f32)
        for d in range(1, N_DEV):
            acc = acc + x_ref[d].astype(f32)
        o_ref[...] = acc

    return pl.pallas_call(
        body, name=name, grid=(R // tr,), in_specs=[pl.BlockSpec((N_DEV, tr, C), lambda i: (0, i, 0))],
        out_specs=pl.BlockSpec((tr, C), lambda i: (i, 0)), out_shape=jax.ShapeDtypeStruct((R, C), f32),
        compiler_params=pltpu.CompilerParams(dimension_semantics=("parallel",)),
    )(x)


BIG = (("w_in", "col"), ("w_out", "row"), ("w_up", "col"), ("w_down", "row"))


class BigWeights:
    def __init__(self, given, me, after):
        self.me = me
        self.kinds = dict(BIG)
        depth = given["w_in"].shape[0]
        self.keys = [(l, n) for l in range(depth) for n, _ in BIG]
        kinds = [self.kinds[n] for _, n in self.keys]
        shards = [given[n][l].astype(bf16) for l, n in self.keys]
        shards[0], _ = lax.optimization_barrier((shards[0], after))
        lands = []
        for s, k in zip(shards, kinds):
            r, c = s.shape
            if k == "col":
                lands.append(lax.dynamic_update_slice(lax.empty((r, N_DEV * c), bf16), s, (0, me * c)))
            else:
                lands.append(lax.dynamic_update_slice(lax.empty((N_DEV * r, c), bf16), s, (me * r, 0)))
        self.send, self.recv, self.shards, self.lands, _ = gather_start(shards, lands, kinds)
        self.pending = {}

    def get(self, l, n, after):
        a = self.keys.index((l, n))
        return gather_wait(a, self.shards[a], self.lands[a], self.kinds[n], self.send, self.recv, after, "gather_wait_%s%d" % (n, l))

    def start_scatter(self, l, n, dw):
        kind = self.kinds[n]
        R, C = dw.shape
        r, c = (R, C // N_DEV) if kind == "col" else (R // N_DEV, C)
        own = lax.dynamic_slice(dw, (0, self.me * c) if kind == "col" else (self.me * r, 0), (r, c))
        slots = lax.dynamic_update_slice(lax.empty((N_DEV, r, c), bf16), own[None], (self.me, 0, 0))
        send, recv, full, slots, token = scatter_start(dw, slots, kind, "scatter_start_%s%d" % (n, l))
        self.pending[(l, n)] = (full, slots, send, recv)
        return token

    def contributions(self, l, n, after):
        full, slots, send, recv = self.pending[(l, n)]
        return scatter_wait(full, slots, self.kinds[n], send, recv, after, "scatter_wait_%s%d" % (n, l))


def allreduce_small(vals, name):
    flat = jnp.concatenate([v.reshape(-1) for v in vals])
    n = flat.shape[0]
    cols = 1024
    m = -(-n // (cols * 8)) * 8
    packed = jnp.pad(flat, (0, m * cols - n)).reshape(m, cols)
    total = sum_slots(allgather_small(packed, name).reshape(N_DEV, m, cols), name + "_sum").reshape(-1)
    out, off = [], 0
    for v in vals:
        out.append(total[off:off + v.size].reshape(v.shape))
        off += v.size
    return out


def ada_mod(c_all, c_ctx, w_ada, b_ada):
    L, D, S = w_ada.shape
    me = _my_index()

    def stacked(c_ctx):
        return jnp.concatenate([c_all, jnp.broadcast_to(c_ctx, (N_DEV, D))], axis=0)

    ts = _pick(S, (512, 384, 256, 128, 64))
    w_spec = pl.BlockSpec((None, D, ts), lambda l, j: (l, 0, j))
    c_spec = pl.BlockSpec((16, D), lambda l, j: (0, 0))
    p_spec = pl.BlockSpec((None, 16, ts), lambda l, j: (l, 0, j))

    def run_fwd(cin, w_ada):
        def body(c_ref, w_ref, o_ref):
            o_ref[...] = jnp.dot(_silu(c_ref[...]).astype(bf16), w_ref[...].astype(bf16), preferred_element_type=f32)

        return pl.pallas_call(
            body, name="ada_fwd", grid=(L, S // ts), in_specs=[c_spec, w_spec], out_specs=p_spec,
            out_shape=jax.ShapeDtypeStruct((L, 16, S), f32),
            compiler_params=pltpu.CompilerParams(dimension_semantics=("parallel", "parallel")),
        )(cin, w_ada)

    def run_bwd(cin, w_ada, dm):
        def body(c_ref, w_ref, dm_ref, gw_ref, dc_ref):
            first = (pl.program_id(0) == 0) & (pl.program_id(1) == 0)
            cv = c_ref[...]
            sg = jax.nn.sigmoid(cv)
            dmv = dm_ref[...].astype(bf16)
            gw_ref[...] = lax.dot_general((cv * sg).astype(bf16), dmv, (((0,), (0,)), ((), ())), preferred_element_type=f32)
            ds = lax.dot_general(dmv, w_ref[...].astype(bf16), (((1,), (1,)), ((), ())), preferred_element_type=f32)
            dc = ds * sg * (1.0 + cv * (1.0 - sg))

            @pl.when(first)
            def _():
                dc_ref[...] = dc

            @pl.when(jnp.logical_not(first))
            def _():
                dc_ref[...] += dc

        return pl.pallas_call(
            body, name="ada_bwd", grid=(L, S // ts), in_specs=[c_spec, w_spec, p_spec], out_specs=[w_spec, c_spec],
            out_shape=[jax.ShapeDtypeStruct((L, D, S), f32), jax.ShapeDtypeStruct((16, D), f32)],
            compiler_params=pltpu.CompilerParams(dimension_semantics=("arbitrary", "arbitrary")),
        )(cin, w_ada, dm)

    def bias_grad(dm_full):
        def body(x_ref, o_ref):
            o_ref[...] = jnp.sum(x_ref[...], axis=0, keepdims=True)

        return pl.pallas_call(
            body, name="ada_bias_grad", grid=(L,), in_specs=[pl.BlockSpec((None, 16, 6 * D), lambda l: (l, 0, 0))],
            out_specs=pl.BlockSpec((None, 1, 6 * D), lambda l: (l, 0, 0)), out_shape=jax.ShapeDtypeStruct((L, 1, 6 * D), f32),
        )(dm_full).reshape(L, 6 * D)

    @jax.custom_vjp
    def op(c_ctx, w_ada, b_ada):
        prod = run_fwd(stacked(c_ctx), w_ada)
        allp = allgather_small(prod.reshape(L * 16, S), "ada_gather").reshape(N_DEV, L, 16, S)
        allp = allp.transpose(1, 2, 0, 3).reshape(L, 16, N_DEV * S)
        mine = lax.dynamic_index_in_dim(allp, me, axis=1, keepdims=False) + b_ada
        ctx = allp[:, N_DEV] + b_ada
        return jnp.stack([mine, ctx], axis=1).reshape(L, 2, 6, D)

    def fwd(c_ctx, w_ada, b_ada):
        return op(c_ctx, w_ada, b_ada), (c_ctx, w_ada)

    def bwd(res, dmod):
        c_ctx, w_ada = res
        dm = dmod.reshape(L * 2, 6 * D)
        gathered = allgather_small(jnp.pad(dm, ((0, (-2 * L) % 8), (0, 0))), "ada_grad_gather")
        gathered = gathered.reshape(N_DEV, -1, 6 * D)[:, :2 * L].reshape(N_DEV, L, 2, 6 * D)
        dm_full = gathered.transpose(1, 2, 0, 3).reshape(L, 16, 6 * D)
        dm_mine = lax.dynamic_slice_in_dim(dm_full, me * S, S, axis=2)
        gw, dc = run_bwd(stacked(c_ctx), w_ada, dm_mine)
        d_cctx = jnp.sum(dc[N_DEV:], axis=0, keepdims=True)
        return d_cctx, gw, bias_grad(dm_full)

    op.defvjp(fwd, bwd)
    return op(c_ctx, w_ada, b_ada)


def _adamw_math(w, gv, m, v):
    c1 = 1.0 / (1.0 - ADAM_B1 ** ADAM_STEP)
    c2 = 1.0 / (1.0 - ADAM_B2 ** ADAM_STEP)
    mn = ADAM_B1 * m + (1.0 - ADAM_B1) * gv
    vn = ADAM_B2 * v + (1.0 - ADAM_B2) * gv * gv
    return -ADAM_LR * ((mn * c1) / (jnp.sqrt(vn * c2) + ADAM_EPS) + ADAM_WD * w), mn, vn


def adamw_layer(l, slots, w, m, v, outs, name):
    depth, r, c = w.shape
    tr = _pick(r, (128, 64, 32, 16, 8))
    if outs is None:
        outs = tuple(lax.empty((depth, r, c), f32) for _ in range(4))

    def body(s_ref, w_ref, m_ref, v_ref, *rest):
        g_ref, d_ref, mo_ref, vo_ref = rest[4:]
        gv = s_ref[0].astype(f32)
        for d in range(1, N_DEV):
            gv = gv + s_ref[d].astype(f32)
        g_ref[...] = gv
        d_ref[...], mo_ref[...], vo_ref[...] = _adamw_math(w_ref[...], gv, m_ref[...], v_ref[...])

    lay = pl.BlockSpec((None, tr, c), lambda i: (l, i, 0))
    return pl.pallas_call(
        body, name=name, grid=(r // tr,),
        in_specs=[pl.BlockSpec((N_DEV, tr, c), lambda i: (0, i, 0)), lay, lay, lay] + [pl.BlockSpec(memory_space=pl.ANY)] * 4,
        out_specs=[lay] * 4, out_shape=[jax.ShapeDtypeStruct((depth, r, c), f32)] * 4,
        input_output_aliases={4 + k: k for k in range(4)},
        compiler_params=pltpu.CompilerParams(dimension_semantics=("arbitrary",)),
    )(slots, w, m, v, *outs)


def adamw(w, g, m, v, name):
    shape = w.shape
    C = shape[-1]
    R = w.size // C
    tr = _pick(R, (256, 128, 64, 32, 16, 8)) if R * C * 4 > (1 << 20) else R

    def body(w_ref, g_ref, m_ref, v_ref, d_ref, mo_ref, vo_ref):
        d_ref[...], mo_ref[...], vo_ref[...] = _adamw_math(w_ref[...], g_ref[...], m_ref[...], v_ref[...])

    spec = pl.BlockSpec((tr, C), lambda i: (i, 0))
    res = pl.pallas_call(
        body, name=name, grid=(R // tr,), in_specs=[spec] * 4, out_specs=[spec] * 3,
        out_shape=[jax.ShapeDtypeStruct((R, C), f32)] * 3,
        compiler_params=pltpu.CompilerParams(dimension_semantics=("parallel",)),
    )(*[a.reshape(R, C) for a in (w, g, m, v)])
    return tuple(r.reshape(shape) for r in res)


def _rope_tables(T, L):
    rows = T // GRID_W
    row = jnp.repeat(jnp.arange(rows, dtype=f32), GRID_W)
    col = jnp.tile(jnp.arange(GRID_W, dtype=f32), rows)
    n_freq = HEAD // 4
    inv = ROPE_THETA ** (-jnp.arange(n_freq, dtype=f32) / n_freq)
    ang = jnp.concatenate([row[:, None] * inv, col[:, None] * inv], axis=-1)
    cos = jnp.repeat(jnp.cos(ang), 2, axis=-1)
    sin = jnp.repeat(jnp.sin(ang), 2, axis=-1) * jnp.tile(jnp.array([-1.0, 1.0], f32), HEAD // 2)
    return (jnp.concatenate([cos, jnp.ones((L, HEAD), f32)]), jnp.concatenate([sin, jnp.zeros((L, HEAD), f32)]))


def _sel(mod, l, idx):
    return jnp.stack([mod[l, :, i] for i in idx], axis=1)


def _row(a, l):
    return a[l][None, :]


def _first_segment(T, ctx):
    def seg(x, mod, p):
        xs = jnp.concatenate([x, ctx], axis=0)
        (h,) = rowwise("modnorm", f_modnorm, [xs], sels=[_sel(mod, 0, (0, 1))], pars=[_row(p["norm1_g"], 0)],
                       outs=[(x.shape[1], bf16)], n_lat=T)
        return xs, h

    return seg


def _mixer_segment(l, T, cosf, sinf):
    row = _row

    def seg(xs, h, mod, lbs, p, w_in, w_out):
        D = xs.shape[1]
        aq, ak, av, hq, hff, hfb, hi, hgt, su, sv = mm(h, w_in, "w_in%d" % l, split=IN_SIZES)
        q, k, v = rowwise("qkprep%d" % l, f_qkprep, [aq, ak, av], consts=[cosf, sinf],
                          pars=[row(p["q_norm_g"], l), row(p["k_norm_g"], l)],
                          outs=[(aq.shape[1], bf16), (ak.shape[1], bf16), (av.shape[1], bf16)], n_lat=T)
        attn = attention(q, k, v, T, "attn%d" % l)
        o_f = hgrn(hq, hff, hi, lbs[0:1], T, False, "hgrn_f%d" % l)
        o_b = hgrn(hq, hfb, hi, lbs[1:2], T, True, "hgrn_b%d" % l)
        (hg,) = rowwise("hgout%d" % l, f_hgout, [o_f, o_b, hgt], pars=[row(p["hg_norm_g"], l)], outs=[(hgt.shape[1], bf16)], n_lat=T)
        (sg,) = rowwise("sgate%d" % l, f_sgate, [su, sv],
                        pars=[row(p["sg_norm_g"], l), p["sg_w"][l]] + [p["sg_b"][l, gi][:, None] for gi in range(SG_GROUPS)],
                        outs=[(su.shape[1], bf16)], n_lat=T)
        mix = jnp.concatenate([attn, hg, sg], axis=1)
        y = mm(mix, w_out, "w_out%d" % l)
        return rowwise("resid_a%d" % l, f_resid_modnorm, [xs, y], sels=[_sel(mod, l, (2, 3, 4))], pars=[row(p["norm2_g"], l)],
                       outs=[(D, f32), (D, bf16)], n_lat=T)

    return seg


def _ffn_segment(l, depth, T, tgt):
    row = _row

    def seg(xs, h2, mod, p, w_up, w_down):
        D = xs.shape[1]
        up = mm(h2, w_up, "w_up%d" % l, out_dtype=bf16)
        act = convact(up, p["conv_w"][l], row(p["conv_b"], l), T, "convact%d" % l)
        z = mm(act, w_down, "w_down%d" % l)
        if l + 1 < depth:
            return rowwise("resid_b%d" % l, f_resid_modnorm, [xs, z],
                           sels=[jnp.concatenate([_sel(mod, l, (5,)), _sel(mod, l + 1, (0, 1))], axis=1)],
                           pars=[row(p["norm1_g"], l + 1)], outs=[(D, f32), (D, bf16)], n_lat=T)
        return rowwise("resid_final", f_resid_final, [xs[:T], z[:T]], consts=[tgt], sels=[_sel(mod, l, (5,))[0:1]],
                       pars=[p["final_norm_g"][None, :]], outs=[(1, f32)])

    return seg


SEGMENT_PARAMS = ("norm1_g", "q_norm_g", "k_norm_g", "hg_norm_g", "sg_norm_g", "sg_w", "sg_b", "norm2_g", "conv_w", "conv_b",
                  "final_norm_g")


def _prologue(p, c_all):
    depth = p["norm1_g"].shape[0]
    mod, vjp_mod = jax.vjp(lambda cc, wa, ba: ada_mod(c_all, cc, wa, ba), p["c_ctx"], p["w_ada"], p["b_ada"])
    lbs, vjp_lbs = jax.vjp(lambda hg: lower_bounds([hg[:, l] for l in range(depth)]), p["hg_lower_bounds"])
    return mod, vjp_mod, lbs, vjp_lbs


def _loss_and_grads(p, prologue, big, x, ctx, tgt, cosf, sinf):
    T = x.shape[0]
    depth = p["norm1_g"].shape[0]
    add = lambda a, b: jax.tree.map(jnp.add, a, b)
    small = {n: p[n] for n in SEGMENT_PARAMS}
    mod, vjp_mod, lbs, vjp_lbs = prologue
    (xs, h), vjp_first = jax.vjp(_first_segment(T, ctx), x, mod, small)
    vjps = []
    for l in range(depth):
        w_in, w_out = big.get(l, "w_in", h), big.get(l, "w_out", h)
        (xs, h2), vj = jax.vjp(_mixer_segment(l, T, cosf, sinf), xs, h, mod, lbs[l], small, w_in, w_out)
        vjps.append(vj)
        w_up, w_down = big.get(l, "w_up", h2), big.get(l, "w_down", h2)
        out, vj = jax.vjp(_ffn_segment(l, depth, T, tgt), xs, h2, mod, small, w_up, w_down)
        vjps.append(vj)
        if l + 1 < depth:
            xs, h = out
    (rowloss,) = out
    loss = 0.5 * jnp.sum(rowloss)

    ct = (jnp.full(rowloss.shape, 0.5, f32),)
    d_mod, d_small, d_lbs = jnp.zeros_like(mod), jax.tree.map(jnp.zeros_like, small), [None] * depth
    for l in range(depth - 1, -1, -1):
        dxs, dh2, dm, ds, d_up, d_down = vjps[2 * l + 1](ct)
        tokens = (big.start_scatter(l, "w_up", d_up), big.start_scatter(l, "w_down", d_down))
        dxs, dh2, _ = lax.optimization_barrier((dxs, dh2, tokens))
        d_mod, d_small = d_mod + dm, add(d_small, ds)
        dxs, dh, dm, d_lbs[l], ds, d_in, d_out = vjps[2 * l]((dxs, dh2))
        tokens = (big.start_scatter(l, "w_in", d_in), big.start_scatter(l, "w_out", d_out))
        dxs, dh, _ = lax.optimization_barrier((dxs, dh, tokens))
        d_mod, d_small = d_mod + dm, add(d_small, ds)
        ct = (dxs, dh)
    dx, dm, ds = vjp_first(ct)
    d_cc, d_wada, d_bada = vjp_mod(d_mod + dm)
    (d_hg,) = vjp_lbs(tuple(d_lbs))
    grads = dict(add(d_small, ds), c_ctx=d_cc, w_ada=d_wada, b_ada=d_bada, hg_lower_bounds=d_hg)
    return loss, grads, dx


def kernel(x, c, ctx, c_ctx, w_ada, b_ada, norm1_g, w_in, q_norm_g, k_norm_g, hg_lower_bounds, hg_norm_g, sg_norm_g, sg_w, sg_b, w_out, norm2_g, w_up, conv_w, conv_b, w_down, final_norm_g, loss_target, m_c_ctx, m_w_ada, m_b_ada, m_norm1_g, m_w_in, m_q_norm_g, m_k_norm_g, m_hg_lower_bounds, m_hg_norm_g, m_sg_norm_g, m_sg_w, m_sg_b, m_w_out, m_norm2_g, m_w_up, m_conv_w, m_conv_b, m_w_down, m_final_norm_g, v_c_ctx, v_w_ada, v_b_ada, v_norm1_g, v_w_in, v_q_norm_g, v_k_norm_g, v_hg_lower_bounds, v_hg_norm_g, v_sg_norm_g, v_sg_w, v_sg_b, v_w_out, v_norm2_g, v_w_up, v_conv_w, v_conv_b, v_w_down, v_final_norm_g):
    given = dict(c_ctx=c_ctx, w_ada=w_ada, b_ada=b_ada, norm1_g=norm1_g, w_in=w_in, q_norm_g=q_norm_g, k_norm_g=k_norm_g,
                 hg_lower_bounds=hg_lower_bounds, hg_norm_g=hg_norm_g, sg_norm_g=sg_norm_g, sg_w=sg_w, sg_b=sg_b, w_out=w_out,
                 norm2_g=norm2_g, w_up=w_up, conv_w=conv_w, conv_b=conv_b, w_down=w_down, final_norm_g=final_norm_g)
    moments_m = dict(c_ctx=m_c_ctx, w_ada=m_w_ada, b_ada=m_b_ada, norm1_g=m_norm1_g, w_in=m_w_in, q_norm_g=m_q_norm_g,
                     k_norm_g=m_k_norm_g, hg_lower_bounds=m_hg_lower_bounds, hg_norm_g=m_hg_norm_g, sg_norm_g=m_sg_norm_g,
                     sg_w=m_sg_w, sg_b=m_sg_b, w_out=m_w_out, norm2_g=m_norm2_g, w_up=m_w_up, conv_w=m_conv_w, conv_b=m_conv_b,
                     w_down=m_w_down, final_norm_g=m_final_norm_g)
    moments_v = dict(c_ctx=v_c_ctx, w_ada=v_w_ada, b_ada=v_b_ada, norm1_g=v_norm1_g, w_in=v_w_in, q_norm_g=v_q_norm_g,
                     k_norm_g=v_k_norm_g, hg_lower_bounds=v_hg_lower_bounds, hg_norm_g=v_hg_norm_g, sg_norm_g=v_sg_norm_g,
                     sg_w=v_sg_w, sg_b=v_sg_b, w_out=v_w_out, norm2_g=v_norm2_g, w_up=v_w_up, conv_w=v_conv_w, conv_b=v_conv_b,
                     w_down=v_w_down, final_norm_g=v_final_norm_g)
    T, D = x.shape[1], x.shape[2]
    L = ctx.shape[1]
    me = _my_index()
    axes = ("x", "y", "c")

    c_all = allgather_small(jnp.pad(c, ((0, 7), (0, 0))), "gather_c").reshape(N_DEV, 8, D)[:, 0]
    depth, hw = hg_lower_bounds.shape[1], hg_lower_bounds.shape[2]
    cw = conv_w.shape[2]
    small = jnp.concatenate([jnp.pad(hg_lower_bounds.reshape(2 * depth, hw), ((0, 0), (0, cw - hw))), conv_w.reshape(3 * depth, cw)], axis=0)
    rows_small = small.shape[0]
    small = allgather_small(jnp.pad(small, ((0, (-rows_small) % 8), (0, 0))), "gather_small").reshape(N_DEV, -1, cw)
    hg_full = small[:, :2 * depth, :hw].reshape(N_DEV, 2, depth, hw).transpose(1, 2, 0, 3).reshape(2, depth, N_DEV * hw)
    cw_full = small[:, 2 * depth:2 * depth + 3 * depth].reshape(N_DEV, depth, 3, cw).transpose(1, 2, 0, 3).reshape(depth, 3, N_DEV * cw)

    p = {n: a for n, a in given.items() if n not in dict(BIG)}
    p.update(hg_lower_bounds=hg_full, conv_w=cw_full, c_ctx=c_ctx[None, :])
    prologue = _prologue(p, c_all)
    big = BigWeights(given, me, after=(prologue[0], cw_full))
    cosf, sinf = _rope_tables(T, L)
    loss, gp, gx = _loss_and_grads(p, prologue, big, x[0], ctx[0], loss_target[0], cosf, sinf)
    loss = lax.psum(loss, axes)

    grads, delta, new_m, new_v = dict(gp), {}, {}, {}
    partial = ['c_ctx', 'norm1_g', 'q_norm_g', 'k_norm_g', 'hg_lower_bounds', 'hg_norm_g', 'sg_norm_g', 'sg_w', 'sg_b',
               'norm2_g', 'conv_w', 'conv_b', 'final_norm_g']
    last = gx
    for n in [w for w in WEIGHTS if w not in dict(BIG) and w not in partial]:
        delta[n], new_m[n], new_v[n] = adamw(given[n], grads[n], moments_m[n], moments_v[n], "adamw_" + n)
        last = delta[n]
    big_outs = {n: None for n, _ in BIG}
    for l in range(depth - 1, -1, -1):
        for n in ("w_down", "w_up", "w_out", "w_in"):
            big_outs[n] = adamw_layer(l, big.contributions(l, n, last), given[n], moments_m[n], moments_v[n], big_outs[n],
                                      "adamw_%s%d" % (n, l))
            last = big_outs[n][1]
    for n, _ in BIG:
        grads[n], delta[n], new_m[n], new_v[n] = big_outs[n]

    first, _ = lax.optimization_barrier((gp[partial[0]], last))
    grads.update(zip(partial, allreduce_small([first] + [gp[n] for n in partial[1:]], "reduce_small")))
    grads['c_ctx'] = grads['c_ctx'][0]
    grads['hg_lower_bounds'] = lax.dynamic_slice_in_dim(grads['hg_lower_bounds'], me * hw, hw, axis=2)
    grads['conv_w'] = lax.dynamic_slice_in_dim(grads['conv_w'], me * cw, cw, axis=2)
    for n in partial:
        delta[n], new_m[n], new_v[n] = adamw(given[n], grads[n], moments_m[n], moments_v[n], "adamw_" + n)
    return (loss, gx[None], *[grads[n] for n in WEIGHTS], *[delta[n] for n in WEIGHTS],
            *[new_m[n] for n in WEIGHTS], *[new_v[n] for n in WEIGHTS])
```

```python
import functools

import jax
import jax.numpy as jnp
from jax import lax
from jax.experimental import pallas as pl
from jax.experimental.pallas import tpu as pltpu

f32 = jnp.float32
bf16 = jnp.bfloat16
HI = lax.Precision.HIGHEST
MESH = pl.DeviceIdType.MESH

EPS = 1e-6
F_MIN = 1e-30
GRID_W = 64
ROPE_THETA = 10000.0
HEAD = 128
ATTN_HEADS, ATTN_KV = 8, 2
ATTN_GROUP = ATTN_HEADS // ATTN_KV
HG_HEADS = 4
SG_GROUPS = 4
SG_CHUNK = 128
HG_CHUNK = 16
HG_GROUP = 16
IN_SIZES = (1024, 256, 256, 512, 512, 512, 512, 512, 512, 512)
N_DEV = 8
ROW_BLOCK = 256
MAX_TK = 2816
ADAM_LR, ADAM_B1, ADAM_B2, ADAM_EPS, ADAM_WD, ADAM_STEP = 0.001, 0.9, 0.999, 1e-08, 0.01, 10

WEIGHTS = ['c_ctx', 'w_ada', 'b_ada', 'norm1_g', 'w_in', 'q_norm_g', 'k_norm_g', 'hg_lower_bounds', 'hg_norm_g',
           'sg_norm_g', 'sg_w', 'sg_b', 'w_out', 'norm2_g', 'w_up', 'conv_w', 'conv_b', 'w_down', 'final_norm_g']


def _pick(dim, cands):
    for t in cands:
        if dim % t == 0:
            return t
    return dim


def _my_index():
    return 4 * lax.axis_index("x") + 2 * lax.axis_index("y") + lax.axis_index("c")


def _mm_call(a, b, mode, out_dtype, name):
    if mode == "nn":
        (M, K), N = a.shape, b.shape[1]
    elif mode == "nt":
        (M, K), N = a.shape, b.shape[0]
    else:
        (K, M), N = a.shape, b.shape[1]
    tm = _pick(M, (1088, 1024, 512, 256, 128))
    tn = _pick(N, (1024, 512, 256, 128))
    tk = K if K <= MAX_TK else _pick(K, (2816, 2560, 2176, 2048, 1408, 1088, 1024, 512, 256, 128))
    nk = K // tk
    dims = {"nn": (((1,), (0,)), ((), ())), "nt": (((1,), (1,)), ((), ())), "tn": (((0,), (0,)), ((), ()))}[mode]

    def body(a_ref, b_ref, o_ref, *acc):
        prod = lax.dot_general(a_ref[...].astype(bf16), b_ref[...].astype(bf16), dims, preferred_element_type=f32)
        if nk == 1:
            o_ref[...] = prod.astype(o_ref.dtype)
            return
        k = pl.program_id(2)

        @pl.when(k == 0)
        def _():
            acc[0][...] = prod

        @pl.when((k > 0) & (k < nk - 1))
        def _():
            acc[0][...] += prod

        @pl.when(k == nk - 1)
        def _():
            o_ref[...] = (acc[0][...] + prod).astype(o_ref.dtype)

    a_spec = pl.BlockSpec((tk, tm), lambda i, j, k: (k, i)) if mode == "tn" else pl.BlockSpec((tm, tk), lambda i, j, k: (i, k))
    b_spec = pl.BlockSpec((tn, tk), lambda i, j, k: (j, k)) if mode == "nt" else pl.BlockSpec((tk, tn), lambda i, j, k: (k, j))
    return pl.pallas_call(
        body, name=name, grid=(M // tm, N // tn, nk),
        in_specs=[a_spec, b_spec], out_specs=pl.BlockSpec((tm, tn), lambda i, j, k: (i, j)),
        out_shape=jax.ShapeDtypeStruct((M, N), out_dtype),
        scratch_shapes=[pltpu.VMEM((tm, tn), f32)] if nk > 1 else [],
        compiler_params=pltpu.CompilerParams(dimension_semantics=("parallel", "parallel", "arbitrary")),
    )(a, b)


def mm(a, w, name, out_dtype=f32, split=None):
    def parts(y):
        if split is None:
            return y
        offs = [sum(split[:i]) for i in range(len(split))]
        return tuple(y[:, o:o + s] for o, s in zip(offs, split))

    @jax.custom_vjp
    def op(a, w):
        return parts(_mm_call(a, w, "nn", out_dtype, name + "_fwd"))

    def fwd(a, w):
        return op(a, w), (a, w)

    def bwd(res, dy):
        a, w = res
        dy = dy.astype(bf16) if split is None else jnp.concatenate([d.astype(bf16) for d in dy], axis=1)
        return _mm_call(dy, w, "nt", a.dtype, name + "_bwd_a"), _mm_call(a, dy, "tn", w.dtype, name + "_bwd_w")

    op.defvjp(fwd, bwd)
    return op(a, w)


def _rowwise_specs(rows, consts, sels, pars, tb, nlb):
    specs = [pl.BlockSpec((tb, a.shape[1]), lambda i: (i, 0)) for a in (*rows, *consts)]
    specs += [pl.BlockSpec((None,) + a.shape[1:], lambda i: (jnp.where(i >= nlb, 1, 0), 0, 0)) for a in sels]
    specs += [pl.BlockSpec(a.shape, functools.partial(lambda i, n: (0,) * n, n=a.ndim)) for a in pars]
    return specs


def rowwise(name, f, rows, consts=(), sels=(), pars=(), outs=(), n_lat=None, tb=ROW_BLOCK):
    rows, consts, sels, pars = tuple(rows), tuple(consts), tuple(sels), tuple(pars)
    R = rows[0].shape[0]
    nb = R // tb
    nlb = nb if n_lat is None else n_lat // tb
    n_in = len(rows) + len(consts) + len(sels) + len(pars)
    n_out = len(outs)
    out_dtypes = [d for _, d in outs]
    out_specs = [pl.BlockSpec((tb, w), lambda i: (i, 0)) for w, _ in outs]
    out_shape = [jax.ShapeDtypeStruct((R, w), d) for w, d in outs]
    sem = pltpu.CompilerParams(dimension_semantics=("arbitrary",))

    def run_fwd(rows, consts, sels, pars):
        def body(*refs):
            res = f(*[r[...] for r in refs[:n_in]])
            for o_ref, r in zip(refs[n_in:], res):
                o_ref[...] = r.astype(o_ref.dtype)

        return tuple(pl.pallas_call(
            body, name=name + "_fwd", grid=(nb,), in_specs=_rowwise_specs(rows, consts, sels, pars, tb, nlb),
            out_specs=out_specs, out_shape=out_shape, compiler_params=sem,
        )(*rows, *consts, *sels, *pars))

    def run_bwd(rows, consts, sels, pars, cts):
        nr, nc, ns, npar = len(rows), len(consts), len(sels), len(pars)

        def body(*refs):
            i = pl.program_id(0)
            ins = [r[...] for r in refs[:n_in]]
            ct = tuple(r[...] for r in refs[n_in:n_in + n_out])
            o_refs = refs[n_in + n_out:]
            cvals = ins[nr:nr + nc]

            def g(*d):
                res = f(*d[:nr], *cvals, *d[nr:])
                return tuple(r.astype(t) for r, t in zip(res, out_dtypes))

            _, vjp = jax.vjp(g, *ins[:nr], *ins[nr + nc:])
            grads = vjp(ct)
            for k in range(nr):
                o_refs[k][...] = grads[k].astype(o_refs[k].dtype)
            for k in range(nr, nr + ns + npar):
                first = (i == 0) | (i == nlb) if k < nr + ns else (i == 0)
                gk = grads[k].astype(f32)

                @pl.when(first)
                def _(k=k, gk=gk):
                    o_refs[k][...] = gk

                @pl.when(jnp.logical_not(first))
                def _(k=k, gk=gk):
                    o_refs[k][...] += gk

        in_specs = _rowwise_specs(rows, consts, sels, pars, tb, nlb) + out_specs
        o_specs = [pl.BlockSpec((tb, a.shape[1]), lambda i: (i, 0)) for a in rows]
        o_specs += [pl.BlockSpec((None,) + a.shape[1:], lambda i: (jnp.where(i >= nlb, 1, 0), 0, 0)) for a in sels]
        o_specs += [pl.BlockSpec(a.shape, functools.partial(lambda i, n: (0,) * n, n=a.ndim)) for a in pars]
        o_shape = [jax.ShapeDtypeStruct(a.shape, a.dtype) for a in rows]
        o_shape += [jax.ShapeDtypeStruct(a.shape, f32) for a in (*sels, *pars)]
        res = pl.pallas_call(
            body, name=name + "_bwd", grid=(nb,), in_specs=in_specs, out_specs=o_specs, out_shape=o_shape,
            compiler_params=sem,
        )(*rows, *consts, *sels, *pars, *cts)
        return tuple(res[:nr]), tuple(res[nr:nr + ns]), tuple(res[nr + ns:])

    @jax.custom_vjp
    def op(rows, consts, sels, pars):
        return run_fwd(rows, consts, sels, pars)

    def fwd(rows, consts, sels, pars):
        return op(rows, consts, sels, pars), (rows, consts, sels, pars)

    def bwd(res, cts):
        rows, consts, sels, pars = res
        drows, dsels, dpars = run_bwd(rows, consts, sels, pars, tuple(cts))
        return drows, tuple(jnp.zeros_like(c) for c in consts), dsels, dpars

    op.defvjp(fwd, bwd)
    return op(rows, consts, sels, pars)


def _rms(x, g):
    return x * lax.rsqrt(jnp.mean(x * x, axis=-1, keepdims=True) + EPS) * g


def _silu(x):
    return x * jax.nn.sigmoid(x)


def f_modnorm(x, mods, g):
    return (_rms(x, g) * (1.0 + mods[1:2]) + mods[0:1],)


def f_resid_modnorm(x, y, mods, g):
    xn = x + mods[0:1] * y
    return xn, _rms(xn, g) * (1.0 + mods[2:3]) + mods[1:2]


def f_resid_final(x, y, tgt, mods, g):
    xn = x + mods[0:1] * y
    err = _rms(xn, g) - tgt
    return (jnp.mean(err * err, axis=-1, keepdims=True),)


def f_qkprep(aq, ak, av, cosf, sinf, qg, kg):
    r = lax.broadcasted_iota(jnp.int32, (HEAD, HEAD), 0)
    c = lax.broadcasted_iota(jnp.int32, (HEAD, HEAD), 1)
    swap = jnp.where((r ^ 1) == c, 1.0, 0.0).astype(f32)

    def head(xh, g):
        y = _rms(xh, g)
        ys = jnp.dot(y, swap, precision=HI, preferred_element_type=f32)
        return y * cosf + ys * sinf

    q = jnp.concatenate([head(aq[:, h * HEAD:(h + 1) * HEAD], qg) for h in range(ATTN_HEADS)], axis=1)
    k = jnp.concatenate([head(ak[:, h * HEAD:(h + 1) * HEAD], kg) for h in range(ATTN_KV)], axis=1)
    return q, k, av


def f_hgout(of, ob, gt, g):
    o = of + ob
    y = jnp.concatenate([_rms(o[:, h * HEAD:(h + 1) * HEAD], g) for h in range(HG_HEADS)], axis=1)
    return (y * _silu(gt),)


def f_sgate(u, v, g, w, b0, b1, b2, b3):
    u = jax.nn.gelu(u)
    v = jax.nn.gelu(v)
    bs = (b0, b1, b2, b3)
    cols = []
    for gi in range(SG_GROUPS):
        sl = slice(gi * HEAD, (gi + 1) * HEAD)
        vg = _rms(v[:, sl], g[:, sl])
        parts = []
        for n in range(v.shape[0] // SG_CHUNK):
            vc = vg[n * SG_CHUNK:(n + 1) * SG_CHUNK]
            parts.append(jnp.dot(w[gi].astype(bf16), vc.astype(bf16), preferred_element_type=f32) + bs[gi])
        cols.append(jnp.concatenate(parts, axis=0))
    return (u * jnp.concatenate(cols, axis=1),)


def attention(q, k, v, T, name, tq=ROW_BLOCK):
    NT = q.shape[0]
    nqb, nlb = NT // tq, T // tq
    scale = HEAD ** -0.5
    q_spec = pl.BlockSpec((tq, HEAD), lambda kv, g, i: (i, kv * ATTN_GROUP + g))
    kv_spec = pl.BlockSpec((NT, HEAD), lambda kv, g, i: (0, kv))
    lse_spec = pl.BlockSpec((None, tq, 1), lambda kv, g, i: (kv * ATTN_GROUP + g, i, 0))
    grid = (ATTN_KV, ATTN_GROUP, nqb)
    nt_dims = (((1,), (1,)), ((), ()))
    tn_dims = (((0,), (0,)), ((), ()))

    def on_keys(i, fn):
        @pl.when(i < nlb)
        def _():
            fn(pl.ds(0, NT))

        @pl.when(i >= nlb)
        def _():
            fn(pl.ds(T, NT - T))

    def run_fwd(q, k, v):
        def body(q_ref, k_ref, v_ref, o_ref, lse_ref):
            def run(rows):
                s = lax.dot_general(q_ref[...], k_ref[rows, :], nt_dims, preferred_element_type=f32)
                m = jnp.max(s, axis=-1, keepdims=True) * scale
                p = jnp.exp(s * scale - m)
                l = jnp.sum(p, axis=-1, keepdims=True)
                o = jnp.dot(p.astype(bf16), v_ref[rows, :], preferred_element_type=f32) / l
                o_ref[...] = o.astype(o_ref.dtype)
                lse_ref[...] = m + jnp.log(l)

            on_keys(pl.program_id(2), run)

        return pl.pallas_call(
            body, name=name + "_fwd", grid=grid, in_specs=[q_spec, kv_spec, kv_spec], out_specs=[q_spec, lse_spec],
            out_shape=[jax.ShapeDtypeStruct(q.shape, bf16), jax.ShapeDtypeStruct((ATTN_HEADS, NT, 1), f32)],
            compiler_params=pltpu.CompilerParams(dimension_semantics=("parallel", "parallel", "arbitrary")),
        )(q, k, v)

    def run_bwd(q, k, v, lse, do):
        def body(q_ref, k_ref, v_ref, lse_ref, do_ref, dq_ref, dk_ref, dv_ref):
            g, i = pl.program_id(1), pl.program_id(2)

            @pl.when((g == 0) & (i == 0))
            def _():
                dk_ref[...] = jnp.zeros_like(dk_ref)
                dv_ref[...] = jnp.zeros_like(dv_ref)

            def run(rows):
                qb, kb, vb, dob = q_ref[...], k_ref[rows, :], v_ref[rows, :], do_ref[...]
                s = lax.dot_general(qb, kb, nt_dims, preferred_element_type=f32)
                p = jnp.exp(s * scale - lse_ref[...])
                dp = lax.dot_general(dob, vb, nt_dims, preferred_element_type=f32)
                ds = (p * (dp - jnp.sum(p * dp, axis=-1, keepdims=True)) * scale).astype(bf16)
                dq_ref[...] = jnp.dot(ds, kb, preferred_element_type=f32).astype(dq_ref.dtype)
                dk_ref[rows, :] += lax.dot_general(ds, qb, tn_dims, preferred_element_type=f32)
                dv_ref[rows, :] += lax.dot_general(p.astype(bf16), dob, tn_dims, preferred_element_type=f32)

            on_keys(i, run)

        return pl.pallas_call(
            body, name=name + "_bwd", grid=grid, in_specs=[q_spec, kv_spec, kv_spec, lse_spec, q_spec],
            out_specs=[q_spec, kv_spec, kv_spec],
            out_shape=[jax.ShapeDtypeStruct(q.shape, bf16), jax.ShapeDtypeStruct(k.shape, f32), jax.ShapeDtypeStruct(v.shape, f32)],
            compiler_params=pltpu.CompilerParams(dimension_semantics=("parallel", "arbitrary", "arbitrary")),
        )(q, k, v, lse, do)

    @jax.custom_vjp
    def op(q, k, v):
        return run_fwd(q, k, v)[0]

    def fwd(q, k, v):
        o, lse = run_fwd(q, k, v)
        return o, (q, k, v, lse)

    def bwd(res, do):
        dq, dk, dv = run_bwd(*res, do)
        return dq, dk.astype(bf16), dv.astype(bf16)

    op.defvjp(fwd, bwd)
    return op(q, k, v)


def _bdot(a, b, ca, cb):
    fa, fb = 3 - ca, 3 - cb

    def dot(x, y, cx, cy):
        return lax.dot_general(x.astype(bf16), y.astype(bf16), (((cx,), (cy,)), ((0,), (0,))), preferred_element_type=f32)

    @jax.custom_vjp
    def op(a, b):
        return dot(a, b, ca, cb)

    def fwd(a, b):
        return op(a, b), (a, b)

    def bwd(res, ct):
        a, b = res
        da = dot(ct, b, 2, fb) if ca == 2 else dot(b, ct, fb, 2)
        db = dot(a, ct, fa, 1) if cb == 1 else dot(ct, a, 1, fa)
        return da, db

    op.defvjp(fwd, bwd)
    return op(a, b)


def _chunk_cumsum(x, rev):
    def impl(x, rev):
        n = x.shape[0]
        pos = lax.broadcasted_iota(jnp.int32, x.shape, 0) % HG_CHUNK
        s = 1
        while s < HG_CHUNK:
            if rev:
                x = x + jnp.where(pos < HG_CHUNK - s, pltpu.roll(x, n - s, 0), 0.0)
            else:
                x = x + jnp.where(pos >= s, pltpu.roll(x, s, 0), 0.0)
            s *= 2
        return x

    @jax.custom_vjp
    def op(x):
        return impl(x, rev)

    op.defvjp(lambda x: (op(x), None), lambda _, ct: (impl(ct, not rev),))
    return op(x)


def _hg_group(St, hq, hf, hi, lb, *, rev):
    G, C = HG_GROUP, HG_CHUNK
    R = G * C
    q = _silu(hq)
    f = lb + (1.0 - lb) * jax.nn.sigmoid(hf)
    logf = jnp.log(jnp.maximum(f, F_MIN))
    kk = (1.0 - lb) * jax.nn.sigmoid(-hf)
    b3 = _chunk_cumsum(logf, rev).reshape(G, C, HEAD)
    q3, k3, v3 = q.reshape(G, C, HEAD), kk.reshape(G, C, HEAD), hi.reshape(G, C, HEAD)
    btot = jnp.sum(logf.reshape(G, C, HEAD), axis=1)
    tt = lax.broadcasted_iota(jnp.int32, (G, C, C, HEAD), 1)
    ss = lax.broadcasted_iota(jnp.int32, (G, C, C, HEAD), 2)
    mask = (ss >= tt) if rev else (ss <= tt)
    diff = b3[:, :, None, :] - b3[:, None, :, :]
    dec = jnp.where(mask, jnp.exp(jnp.where(mask, diff, 0.0)), 0.0)
    scores = jnp.sum(q3[:, :, None, :] * k3[:, None, :, :] * dec, axis=-1)
    o_intra = _bdot(scores, v3, 2, 1)
    q_dec = q3 * jnp.exp(b3)
    k_dec = k3 * jnp.exp(btot[:, None, :] - b3)
    kvt = _bdot(v3, k_dec, 1, 1)
    dl = jnp.exp(btot)
    states = [None] * G
    for g in (range(G - 1, -1, -1) if rev else range(G)):
        states[g] = St
        St = St * dl[g:g + 1, :] + kvt[g]
    o_inter = _bdot(q_dec, jnp.stack(states), 2, 2)
    return St, (o_intra + o_inter).reshape(R, HEAD)


def hgrn(hq, hf, hi, lb, T, rev, name):
    NT, W = hq.shape
    R = HG_GROUP * HG_CHUNK
    n_lat, n_ctx = T // R, (NT - T) // R
    nG = n_lat + n_ctx

    def group_of(j):
        if rev:
            return jnp.where(j < n_ctx, nG - 1 - j, n_lat - 1 - (j - n_ctx))
        return jnp.where(j < n_ctx, n_lat + j, j - n_ctx)

    def rows_of(j):
        return pl.ds(pl.multiple_of(group_of(j) * R, R), R)

    col_spec = pl.BlockSpec((NT, HEAD), lambda h: (0, h))
    lb_spec = pl.BlockSpec((1, HEAD), lambda h: (0, h))
    st_spec = pl.BlockSpec((None, nG, HEAD, HEAD), lambda h: (h, 0, 0, 0))
    sem = pltpu.CompilerParams(dimension_semantics=("parallel",))

    def run_fwd(hq, hf, hi, lb):
        def body(hq_ref, hf_ref, hi_ref, lb_ref, o_ref, st_ref):
            def step(j, St):
                st_ref[j] = St
                rows = rows_of(j)
                St, o = _hg_group(St, hq_ref[rows, :], hf_ref[rows, :], hi_ref[rows, :], lb_ref[...], rev=rev)
                o_ref[rows, :] = o
                return St

            lax.fori_loop(0, nG, step, jnp.zeros((HEAD, HEAD), f32))

        return pl.pallas_call(
            body, name=name + "_fwd", grid=(W // HEAD,), in_specs=[col_spec, col_spec, col_spec, lb_spec],
            out_specs=[col_spec, st_spec],
            out_shape=[jax.ShapeDtypeStruct((NT, W), f32), jax.ShapeDtypeStruct((W // HEAD, nG, HEAD, HEAD), f32)],
            compiler_params=sem,
        )(hq, hf, hi, lb)

    def run_bwd(hq, hf, hi, lb, st, do):
        def body(hq_ref, hf_ref, hi_ref, lb_ref, st_ref, do_ref, dq_ref, df_ref, di_ref, dlb_ref):
            def step(jj, carry):
                dS, dlb = carry
                j = nG - 1 - jj
                rows = rows_of(j)
                _, vjp = jax.vjp(functools.partial(_hg_group, rev=rev), st_ref[j], hq_ref[rows, :], hf_ref[rows, :],
                                 hi_ref[rows, :], lb_ref[...])
                dS, dq, df, di, dl = vjp((dS, do_ref[rows, :]))
                dq_ref[rows, :] = dq
                df_ref[rows, :] = df
                di_ref[rows, :] = di
                return dS, dlb + dl

            _, dlb = lax.fori_loop(0, nG, step, (jnp.zeros((HEAD, HEAD), f32), jnp.zeros((1, HEAD), f32)))
            dlb_ref[...] = dlb

        return pl.pallas_call(
            body, name=name + "_bwd", grid=(W // HEAD,),
            in_specs=[col_spec, col_spec, col_spec, lb_spec, st_spec, col_spec],
            out_specs=[col_spec, col_spec, col_spec, lb_spec],
            out_shape=[jax.ShapeDtypeStruct((NT, W), f32)] * 3 + [jax.ShapeDtypeStruct((1, W), f32)],
            compiler_params=sem,
        )(hq, hf, hi, lb, st, do)

    @jax.custom_vjp
    def op(hq, hf, hi, lb):
        return run_fwd(hq, hf, hi, lb)[0]

    def fwd(hq, hf, hi, lb):
        o, st = run_fwd(hq, hf, hi, lb)
        return o, (hq, hf, hi, lb, st)

    def bwd(res, do):
        return tuple(run_bwd(*res, do))

    op.defvjp(fwd, bwd)
    return op(hq, hf, hi, lb)


def lower_bounds(params):
    n = len(params)

    def f(*a):
        m = functools.reduce(jnp.maximum, a)
        e = [jnp.exp(x - m) for x in a]
        s = functools.reduce(lambda u, v: u + v, e)
        p = [x / s for x in e]
        out, run = [], jnp.zeros_like(p[0])
        for l in range(n):
            out.append(run)
            run = run + p[l]
        return tuple(out[l] + p[l] - p[0] for l in range(n))

    shape = [jax.ShapeDtypeStruct(params[0].shape, f32)] * n

    @jax.custom_vjp
    def op(*a):
        def body(*refs):
            for o_ref, r in zip(refs[n:], f(*[x[...] for x in refs[:n]])):
                o_ref[...] = r
        return tuple(pl.pallas_call(body, name="lower_bounds_fwd", out_shape=shape)(*a))

    def fwd(*a):
        return op(*a), a

    def bwd(a, cts):
        def body(*refs):
            _, vjp = jax.vjp(f, *[x[...] for x in refs[:n]])
            for o_ref, r in zip(refs[2 * n:], vjp(tuple(x[...] for x in refs[n:2 * n]))):
                o_ref[...] = r
        return tuple(pl.pallas_call(body, name="lower_bounds_bwd", out_shape=shape)(*a, *cts))

    op.defvjp(fwd, bwd)
    return op(*params)


def _shift_rows(x, d, T):
    n = x.shape[0]
    t = lax.broadcasted_iota(jnp.int32, x.shape, 0)
    y = pltpu.roll(x, d % n, 0)
    edge = ((t == 0) | (t == T)) if d == 1 else ((t == T - 1) | (t == n - 1))
    return jnp.where(edge, 0.0, y)


def _conv(x, w, b, T):
    return b + w[0:1] * _shift_rows(x, 1, T) + w[1:2] * x + w[2:3] * _shift_rows(x, -1, T)


def convact(up, cw, cb, T, name, tc=128):
    NT, F2 = up.shape
    F = F2 // 2
    tc = _pick(F, (tc, 128))
    nf = F // tc
    g_spec = lambda r: pl.BlockSpec((r, tc), lambda j: (0, j))
    v_spec = lambda r: pl.BlockSpec((r, tc), lambda j: (0, j + nf))
    sem = pltpu.CompilerParams(dimension_semantics=("parallel",))

    def run_fwd(up, cw, cb):
        def body(xg_ref, xv_ref, wg_ref, wv_ref, bg_ref, bv_ref, o_ref):
            yg = _conv(xg_ref[...].astype(f32), wg_ref[...], bg_ref[...], T)
            yv = _conv(xv_ref[...].astype(f32), wv_ref[...], bv_ref[...], T)
            o_ref[...] = (_silu(yg) * yv).astype(o_ref.dtype)

        return pl.pallas_call(
            body, name=name + "_fwd", grid=(nf,),
            in_specs=[g_spec(NT), v_spec(NT), g_spec(3), v_spec(3), g_spec(1), v_spec(1)], out_specs=g_spec(NT),
            out_shape=jax.ShapeDtypeStruct((NT, F), bf16), compiler_params=sem,
        )(up, up, cw, cw, cb, cb)

    def run_bwd(up, cw, cb, dact):
        def body(xg_ref, xv_ref, wg_ref, wv_ref, bg_ref, bv_ref, da_ref, dxg_ref, dxv_ref, dwg_ref, dwv_ref, dbg_ref, dbv_ref):
            xg, xv, wg, wv = xg_ref[...].astype(f32), xv_ref[...].astype(f32), wg_ref[...], wv_ref[...]
            yg = _conv(xg, wg, bg_ref[...], T)
            yv = _conv(xv, wv, bv_ref[...], T)
            da = da_ref[...].astype(f32)
            sg = jax.nn.sigmoid(yg)
            dyv = da * yg * sg
            dyg = da * yv * sg * (1.0 + yg * (1.0 - sg))
            for x, w, dy, dx_ref, dw_ref, db_ref in ((xg, wg, dyg, dxg_ref, dwg_ref, dbg_ref), (xv, wv, dyv, dxv_ref, dwv_ref, dbv_ref)):
                dx_ref[...] = (w[0:1] * _shift_rows(dy, -1, T) + w[1:2] * dy + w[2:3] * _shift_rows(dy, 1, T)).astype(dx_ref.dtype)
                dw_ref[...] = jnp.concatenate([
                    jnp.sum(dy * _shift_rows(x, 1, T), axis=0, keepdims=True),
                    jnp.sum(dy * x, axis=0, keepdims=True),
                    jnp.sum(dy * _shift_rows(x, -1, T), axis=0, keepdims=True)], axis=0)
                db_ref[...] = jnp.sum(dy, axis=0, keepdims=True)

        return pl.pallas_call(
            body, name=name + "_bwd", grid=(nf,),
            in_specs=[g_spec(NT), v_spec(NT), g_spec(3), v_spec(3), g_spec(1), v_spec(1), g_spec(NT)],
            out_specs=[g_spec(NT), g_spec(NT), g_spec(3), g_spec(3), g_spec(1), g_spec(1)],
            out_shape=[jax.ShapeDtypeStruct((NT, F), up.dtype)] * 2 + [jax.ShapeDtypeStruct((3, F), f32)] * 2 + [jax.ShapeDtypeStruct((1, F), f32)] * 2,
            compiler_params=sem,
        )(up, up, cw, cw, cb, cb, dact)

    @jax.custom_vjp
    def op(up, cw, cb):
        return run_fwd(up, cw, cb)

    def fwd(up, cw, cb):
        return op(up, cw, cb), (up, cw, cb)

    def bwd(res, dact):
        dxg, dxv, dwg, dwv, dbg, dbv = run_bwd(*res, dact)
        return (jnp.concatenate([dxg, dxv], axis=1), jnp.concatenate([dwg, dwv], axis=1), jnp.concatenate([dbg, dbv], axis=1))

    op.defvjp(fwd, bwd)
    return op(up, cw, cb)


def _peers():
    x, y, c = lax.axis_index("x"), lax.axis_index("y"), lax.axis_index("c")
    return (x, y, c), [(x, y, 1 - c), (1 - x, y, c), (x, 1 - y, c), (1 - x, 1 - y, c),
                       (1 - x, y, 1 - c), (x, 1 - y, 1 - c), (1 - x, 1 - y, 1 - c)]


def _index(dev):
    return 4 * dev[0] + 2 * dev[1] + dev[2]


def allgather_small(x, name):
    m, n = x.shape

    def body(x_ref, out_ref, send_sems, recv_sems, local_sem):
        me, peers = _peers()

        def rows(dev):
            return out_ref.at[pl.ds(pl.multiple_of(_index(dev) * m, 8), m), :]

        mine = pltpu.make_async_copy(x_ref, rows(me), local_sem)
        mine.start()
        sends = [pltpu.make_async_remote_copy(src_ref=x_ref, dst_ref=rows(me), send_sem=send_sems.at[k], recv_sem=recv_sems.at[k],
                                              device_id=p, device_id_type=MESH) for k, p in enumerate(peers)]
        for cp in sends:
            cp.start()
        for k, p in enumerate(peers):
            pltpu.make_async_remote_copy(src_ref=x_ref, dst_ref=rows(p), send_sem=send_sems.at[k], recv_sem=recv_sems.at[k],
                                         device_id=p, device_id_type=MESH).wait_recv()
        for cp in sends:
            cp.wait_send()
        mine.wait()

    return pl.pallas_call(
        body, name=name, out_shape=jax.ShapeDtypeStruct((N_DEV * m, n), x.dtype),
        in_specs=[pl.BlockSpec(memory_space=pltpu.VMEM)], out_specs=pl.BlockSpec(memory_space=pltpu.VMEM),
        scratch_shapes=[pltpu.SemaphoreType.DMA((7,)), pltpu.SemaphoreType.DMA((7,)), pltpu.SemaphoreType.DMA],
    )(x)


HBM_SPEC = pl.BlockSpec(memory_space=pltpu.HBM)
SEM_SPEC = pl.BlockSpec(memory_space=pltpu.SEMAPHORE)
SPLIT_PARAMS = dict(compiler_params=pltpu.CompilerParams(has_side_effects=pltpu.SideEffectType.DATAFLOW_SIDE_EFFECTING))
N_PEERS = N_DEV - 1


def _part(ref, kind, j, width):
    if kind == "col":
        return ref.at[:, pl.ds(pl.multiple_of(j * width, 128), width)]
    return ref.at[pl.ds(pl.multiple_of(j * width, 8), width), :]


def _in_hbm(a):
    return pltpu.with_memory_space_constraint(a, pltpu.HBM)


def gather_start(shards, lands, kinds, after):
    n = len(shards)
    widths = [s.shape[1] if k == "col" else s.shape[0] for s, k in zip(shards, kinds)]
    n_in = 2 * n + len(after)

    def body(*refs):
        srcs, lnds, send_sems, recv_sems, token = refs[:n], refs[n:2 * n], refs[n_in], refs[n_in + 1], refs[-1]
        me, peers = _peers()
        for a in range(n):
            for k, p in enumerate(peers):
                pltpu.make_async_remote_copy(
                    src_ref=srcs[a], dst_ref=_part(lnds[a], kinds[a], _index(me), widths[a]),
                    send_sem=send_sems.at[N_PEERS * a + k], recv_sem=recv_sems.at[N_PEERS * a + k],
                    device_id=p, device_id_type=MESH).start()
        token[...] = jnp.zeros_like(token)

    res = pl.pallas_call(
        body, name="gather_start",
        out_shape=(pltpu.SemaphoreType.DMA((N_PEERS * n,)), pltpu.SemaphoreType.DMA((N_PEERS * n,)),
                   *[pltpu.HBM(a.shape, a.dtype) for a in (*shards, *lands)], jax.ShapeDtypeStruct((8, 128), f32)),
        in_specs=[HBM_SPEC] * (2 * n) + [pl.BlockSpec(memory_space=pl.ANY)] * len(after),
        out_specs=(SEM_SPEC, SEM_SPEC, *[HBM_SPEC] * (2 * n), pl.BlockSpec(memory_space=pltpu.VMEM)),
        input_output_aliases={i: 2 + i for i in range(2 * n)}, **SPLIT_PARAMS,
    )(*[_in_hbm(a) for a in (*shards, *lands)], *after)
    return res[0], res[1], res[2:2 + n], res[2 + n:2 + 2 * n], res[-1]


def gather_wait(a, shard, land, kind, send_sems, recv_sems, after, name):
    width = shard.shape[1] if kind == "col" else shard.shape[0]

    def body(src_ref, land_ref, send_ref, recv_ref, after_ref, src_out, land_out):
        _, peers = _peers()
        for k, p in enumerate(peers):
            cp = pltpu.make_async_remote_copy(
                src_ref=src_ref, dst_ref=_part(land_ref, kind, _index(p), width),
                send_sem=send_ref.at[N_PEERS * a + k], recv_sem=recv_ref.at[N_PEERS * a + k], device_id=p, device_id_type=MESH)
            cp.wait_send()
            cp.wait_recv()

    return pl.pallas_call(
        body, name=name, out_shape=(pltpu.HBM(shard.shape, shard.dtype), pltpu.HBM(land.shape, land.dtype)),
        in_specs=(HBM_SPEC, HBM_SPEC, SEM_SPEC, SEM_SPEC, pl.BlockSpec(memory_space=pl.ANY)), out_specs=(HBM_SPEC, HBM_SPEC),
        input_output_aliases={0: 0, 1: 1}, **SPLIT_PARAMS,
    )(shard, land, send_sems, recv_sems, after)[1]


def scatter_start(full, slots, kind, name):
    width = slots.shape[2] if kind == "col" else slots.shape[1]

    def body(full_ref, slots_ref, send_sems, recv_sems, full_out, slots_out, token):
        me, peers = _peers()
        for k, p in enumerate(peers):
            pltpu.make_async_remote_copy(
                src_ref=_part(full_ref, kind, _index(p), width), dst_ref=slots_ref.at[_index(me)],
                send_sem=send_sems.at[k], recv_sem=recv_sems.at[k], device_id=p, device_id_type=MESH).start()
        token[...] = jnp.zeros_like(token)

    return pl.pallas_call(
        body, name=name,
        out_shape=(pltpu.SemaphoreType.DMA((N_PEERS,)), pltpu.SemaphoreType.DMA((N_PEERS,)), pltpu.HBM(full.shape, full.dtype),
                   pltpu.HBM(slots.shape, slots.dtype), jax.ShapeDtypeStruct((8, 128), f32)),
        in_specs=(HBM_SPEC, HBM_SPEC), out_specs=(SEM_SPEC, SEM_SPEC, HBM_SPEC, HBM_SPEC, pl.BlockSpec(memory_space=pltpu.VMEM)),
        input_output_aliases={0: 2, 1: 3}, **SPLIT_PARAMS,
    )(_in_hbm(full), _in_hbm(slots))


def scatter_wait(full, slots, kind, send_sems, recv_sems, after, name):
    width = slots.shape[2] if kind == "col" else slots.shape[1]

    def body(full_ref, slots_ref, send_ref, recv_ref, after_ref, full_out, slots_out):
        me, peers = _peers()
        for k, p in enumerate(peers):
            cp = pltpu.make_async_remote_copy(
                src_ref=_part(full_ref, kind, _index(p), width), dst_ref=slots_ref.at[_index(p)],
                send_sem=send_ref.at[k], recv_sem=recv_ref.at[k], device_id=p, device_id_type=MESH)
            cp.wait_send()
            cp.wait_recv()

    return pl.pallas_call(
        body, name=name, out_shape=(pltpu.HBM(full.shape, full.dtype), pltpu.HBM(slots.shape, slots.dtype)),
        in_specs=(HBM_SPEC, HBM_SPEC, SEM_SPEC, SEM_SPEC, pl.BlockSpec(memory_space=pl.ANY)), out_specs=(HBM_SPEC, HBM_SPEC),
        input_output_aliases={0: 0, 1: 1}, **SPLIT_PARAMS,
    )(full, slots, send_sems, recv_sems, after)[1]


def sum_slots(x, name):
    _, R, C = x.shape
    tr = _pick(R, (256, 128, 64, 32, 16, 8))

    def body(x_ref, o_ref):
        acc = x_ref[0].astype(f32)
        for d in range(1, N_DEV):
            acc = acc + x_ref[d].astype(f32)
        o_ref[...] = acc

    return pl.pallas_call(
        body, name=name, grid=(R // tr,), in_specs=[pl.BlockSpec((N_DEV, tr, C), lambda i: (0, i, 0))],
        out_specs=pl.BlockSpec((tr, C), lambda i: (i, 0)), out_shape=jax.ShapeDtypeStruct((R, C), f32),
        compiler_params=pltpu.CompilerParams(dimension_semantics=("parallel",)),
    )(x)


BIG = (("w_in", "col"), ("w_out", "row"), ("w_up", "col"), ("w_down", "row"))


class BigWeights:
    def __init__(self, given, me, after):
        self.me = me
        self.kinds = dict(BIG)
        depth = given["w_in"].shape[0]
        self.keys = [(l, n) for l in range(depth) for n, _ in BIG]
        kinds = [self.kinds[n] for _, n in self.keys]
        shards = [given[n][l].astype(bf16) for l, n in self.keys]
        lands = []
        for s, k in zip(shards, kinds):
            r, c = s.shape
            if k == "col":
                lands.append(lax.dynamic_update_slice(lax.empty((r, N_DEV * c), bf16), s, (0, me * c)))
            else:
                lands.append(lax.dynamic_update_slice(lax.empty((N_DEV * r, c), bf16), s, (me * r, 0)))
        self.send, self.recv, self.shards, self.lands, _ = gather_start(shards, lands, kinds, after)
        self.pending = {}

    def get(self, l, n, after):
        a = self.keys.index((l, n))
        return gather_wait(a, self.shards[a], self.lands[a], self.kinds[n], self.send, self.recv, after, "gather_wait_%s%d" % (n, l))

    def start_scatter(self, l, n, dw):
        kind = self.kinds[n]
        R, C = dw.shape
        r, c = (R, C // N_DEV) if kind == "col" else (R // N_DEV, C)
        own = lax.dynamic_slice(dw, (0, self.me * c) if kind == "col" else (self.me * r, 0), (r, c))
        slots = lax.dynamic_update_slice(lax.empty((N_DEV, r, c), bf16), own[None], (self.me, 0, 0))
        send, recv, full, slots, token = scatter_start(dw, slots, kind, "scatter_start_%s%d" % (n, l))
        self.pending[(l, n)] = (full, slots, send, recv)
        return token

    def contributions(self, l, n, after):
        full, slots, send, recv = self.pending[(l, n)]
        return scatter_wait(full, slots, self.kinds[n], send, recv, after, "scatter_wait_%s%d" % (n, l))


def allreduce_small(vals, name):
    flat = jnp.concatenate([v.reshape(-1) for v in vals])
    n = flat.shape[0]
    cols = 1024
    m = -(-n // (cols * 8)) * 8
    packed = jnp.pad(flat, (0, m * cols - n)).reshape(m, cols)
    total = sum_slots(allgather_small(packed, name).reshape(N_DEV, m, cols), name + "_sum").reshape(-1)
    out, off = [], 0
    for v in vals:
        out.append(total[off:off + v.size].reshape(v.shape))
        off += v.size
    return out


def ada_mod(c_all, c_ctx, w_ada, b_ada):
    L, D, S = w_ada.shape
    me = _my_index()

    def stacked(c_ctx):
        return jnp.concatenate([c_all, jnp.broadcast_to(c_ctx, (N_DEV, D))], axis=0)

    ts = _pick(S, (512, 384, 256, 128, 64))
    w_spec = pl.BlockSpec((None, D, ts), lambda l, j: (l, 0, j))
    c_spec = pl.BlockSpec((16, D), lambda l, j: (0, 0))
    p_spec = pl.BlockSpec((None, 16, ts), lambda l, j: (l, 0, j))

    def run_fwd(cin, w_ada):
        def body(c_ref, w_ref, o_ref):
            o_ref[...] = jnp.dot(_silu(c_ref[...]).astype(bf16), w_ref[...].astype(bf16), preferred_element_type=f32)

        return pl.pallas_call(
            body, name="ada_fwd", grid=(L, S // ts), in_specs=[c_spec, w_spec], out_specs=p_spec,
            out_shape=jax.ShapeDtypeStruct((L, 16, S), f32),
            compiler_params=pltpu.CompilerParams(dimension_semantics=("parallel", "parallel")),
        )(cin, w_ada)

    def run_bwd(cin, w_ada, dm):
        def body(c_ref, w_ref, dm_ref, gw_ref, dc_ref):
            first = (pl.program_id(0) == 0) & (pl.program_id(1) == 0)
            cv = c_ref[...]
            sg = jax.nn.sigmoid(cv)
            dmv = dm_ref[...].astype(bf16)
            gw_ref[...] = lax.dot_general((cv * sg).astype(bf16), dmv, (((0,), (0,)), ((), ())), preferred_element_type=f32)
            ds = lax.dot_general(dmv, w_ref[...].astype(bf16), (((1,), (1,)), ((), ())), preferred_element_type=f32)
            dc = ds * sg * (1.0 + cv * (1.0 - sg))

            @pl.when(first)
            def _():
                dc_ref[...] = dc

            @pl.when(jnp.logical_not(first))
            def _():
                dc_ref[...] += dc

        return pl.pallas_call(
            body, name="ada_bwd", grid=(L, S // ts), in_specs=[c_spec, w_spec, p_spec], out_specs=[w_spec, c_spec],
            out_shape=[jax.ShapeDtypeStruct((L, D, S), f32), jax.ShapeDtypeStruct((16, D), f32)],
            compiler_params=pltpu.CompilerParams(dimension_semantics=("arbitrary", "arbitrary")),
        )(cin, w_ada, dm)

    def bias_grad(dm_full):
        def body(x_ref, o_ref):
            o_ref[...] = jnp.sum(x_ref[...], axis=0, keepdims=True)

        return pl.pallas_call(
            body, name="ada_bias_grad", grid=(L,), in_specs=[pl.BlockSpec((None, 16, 6 * D), lambda l: (l, 0, 0))],
            out_specs=pl.BlockSpec((None, 1, 6 * D), lambda l: (l, 0, 0)), out_shape=jax.ShapeDtypeStruct((L, 1, 6 * D), f32),
        )(dm_full).reshape(L, 6 * D)

    @jax.custom_vjp
    def op(c_ctx, w_ada, b_ada):
        prod = run_fwd(stacked(c_ctx), w_ada)
        allp = allgather_small(prod.reshape(L * 16, S), "ada_gather").reshape(N_DEV, L, 16, S)
        allp = allp.transpose(1, 2, 0, 3).reshape(L, 16, N_DEV * S)
        mine = lax.dynamic_index_in_dim(allp, me, axis=1, keepdims=False) + b_ada
        ctx = allp[:, N_DEV] + b_ada
        return jnp.stack([mine, ctx], axis=1).reshape(L, 2, 6, D)

    def fwd(c_ctx, w_ada, b_ada):
        return op(c_ctx, w_ada, b_ada), (c_ctx, w_ada)

    def bwd(res, dmod):
        c_ctx, w_ada = res
        dm = dmod.reshape(L * 2, 6 * D)
        gathered = allgather_small(jnp.pad(dm, ((0, (-2 * L) % 8), (0, 0))), "ada_grad_gather")
        gathered = gathered.reshape(N_DEV, -1, 6 * D)[:, :2 * L].reshape(N_DEV, L, 2, 6 * D)
        dm_full = gathered.transpose(1, 2, 0, 3).reshape(L, 16, 6 * D)
        dm_mine = lax.dynamic_slice_in_dim(dm_full, me * S, S, axis=2)
        gw, dc = run_bwd(stacked(c_ctx), w_ada, dm_mine)
        d_cctx = jnp.sum(dc[N_DEV:], axis=0, keepdims=True)
        return d_cctx, gw, bias_grad(dm_full)

    op.defvjp(fwd, bwd)
    return op(c_ctx, w_ada, b_ada)


def _adamw_math(w, gv, m, v):
    c1 = 1.0 / (1.0 - ADAM_B1 ** ADAM_STEP)
    c2 = 1.0 / (1.0 - ADAM_B2 ** ADAM_STEP)
    mn = ADAM_B1 * m + (1.0 - ADAM_B1) * gv
    vn = ADAM_B2 * v + (1.0 - ADAM_B2) * gv * gv
    return -ADAM_LR * ((mn * c1) / (jnp.sqrt(vn * c2) + ADAM_EPS) + ADAM_WD * w), mn, vn


def adamw_layer(l, slots, w, m, v, outs, name):
    depth, r, c = w.shape
    tr = _pick(r, (128, 64, 32, 16, 8))
    if outs is None:
        outs = tuple(lax.empty((depth, r, c), f32) for _ in range(4))

    def body(s_ref, w_ref, m_ref, v_ref, *rest):
        g_ref, d_ref, mo_ref, vo_ref = rest[4:]
        gv = s_ref[0].astype(f32)
        for d in range(1, N_DEV):
            gv = gv + s_ref[d].astype(f32)
        g_ref[...] = gv
        d_ref[...], mo_ref[...], vo_ref[...] = _adamw_math(w_ref[...], gv, m_ref[...], v_ref[...])

    lay = pl.BlockSpec((None, tr, c), lambda i: (l, i, 0))
    return pl.pallas_call(
        body, name=name, grid=(r // tr,),
        in_specs=[pl.BlockSpec((N_DEV, tr, c), lambda i: (0, i, 0)), lay, lay, lay] + [pl.BlockSpec(memory_space=pl.ANY)] * 4,
        out_specs=[lay] * 4, out_shape=[jax.ShapeDtypeStruct((depth, r, c), f32)] * 4,
        input_output_aliases={4 + k: k for k in range(4)},
        compiler_params=pltpu.CompilerParams(dimension_semantics=("arbitrary",)),
    )(slots, w, m, v, *outs)


def adamw(w, g, m, v, name):
    shape = w.shape
    C = shape[-1]
    R = w.size // C
    tr = _pick(R, (256, 128, 64, 32, 16, 8)) if R * C * 4 > (1 << 20) else R

    def body(w_ref, g_ref, m_ref, v_ref, d_ref, mo_ref, vo_ref):
        d_ref[...], mo_ref[...], vo_ref[...] = _adamw_math(w_ref[...], g_ref[...], m_ref[...], v_ref[...])

    spec = pl.BlockSpec((tr, C), lambda i: (i, 0))
    res = pl.pallas_call(
        body, name=name, grid=(R // tr,), in_specs=[spec] * 4, out_specs=[spec] * 3,
        out_shape=[jax.ShapeDtypeStruct((R, C), f32)] * 3,
        compiler_params=pltpu.CompilerParams(dimension_semantics=("parallel",)),
    )(*[a.reshape(R, C) for a in (w, g, m, v)])
    return tuple(r.reshape(shape) for r in res)


def _rope_tables(T, L):
    rows = T // GRID_W
    row = jnp.repeat(jnp.arange(rows, dtype=f32), GRID_W)
    col = jnp.tile(jnp.arange(GRID_W, dtype=f32), rows)
    n_freq = HEAD // 4
    inv = ROPE_THETA ** (-jnp.arange(n_freq, dtype=f32) / n_freq)
    ang = jnp.concatenate([row[:, None] * inv, col[:, None] * inv], axis=-1)
    cos = jnp.repeat(jnp.cos(ang), 2, axis=-1)
    sin = jnp.repeat(jnp.sin(ang), 2, axis=-1) * jnp.tile(jnp.array([-1.0, 1.0], f32), HEAD // 2)
    return (jnp.concatenate([cos, jnp.ones((L, HEAD), f32)]), jnp.concatenate([sin, jnp.zeros((L, HEAD), f32)]))


def _sel(mod, l, idx):
    return jnp.stack([mod[l, :, i] for i in idx], axis=1)


def _row(a, l):
    return a[l][None, :]


def _first_segment(T, ctx):
    def seg(x, mod, p):
        xs = jnp.concatenate([x, ctx], axis=0)
        (h,) = rowwise("modnorm", f_modnorm, [xs], sels=[_sel(mod, 0, (0, 1))], pars=[_row(p["norm1_g"], 0)],
                       outs=[(x.shape[1], bf16)], n_lat=T)
        return xs, h

    return seg


def _mixer_segment(l, T, cosf, sinf):
    row = _row

    def seg(xs, h, mod, lbs, p, w_in, w_out):
        D = xs.shape[1]
        aq, ak, av, hq, hff, hfb, hi, hgt, su, sv = mm(h, w_in, "w_in%d" % l, split=IN_SIZES)
        q, k, v = rowwise("qkprep%d" % l, f_qkprep, [aq, ak, av], consts=[cosf, sinf],
                          pars=[row(p["q_norm_g"], l), row(p["k_norm_g"], l)],
                          outs=[(aq.shape[1], bf16), (ak.shape[1], bf16), (av.shape[1], bf16)], n_lat=T)
        attn = attention(q, k, v, T, "attn%d" % l)
        o_f = hgrn(hq, hff, hi, lbs[0:1], T, False, "hgrn_f%d" % l)
        o_b = hgrn(hq, hfb, hi, lbs[1:2], T, True, "hgrn_b%d" % l)
        (hg,) = rowwise("hgout%d" % l, f_hgout, [o_f, o_b, hgt], pars=[row(p["hg_norm_g"], l)], outs=[(hgt.shape[1], bf16)], n_lat=T)
        (sg,) = rowwise("sgate%d" % l, f_sgate, [su, sv],
                        pars=[row(p["sg_norm_g"], l), p["sg_w"][l]] + [p["sg_b"][l, gi][:, None] for gi in range(SG_GROUPS)],
                        outs=[(su.shape[1], bf16)], n_lat=T)
        mix = jnp.concatenate([attn, hg, sg], axis=1)
        y = mm(mix, w_out, "w_out%d" % l)
        return rowwise("resid_a%d" % l, f_resid_modnorm, [xs, y], sels=[_sel(mod, l, (2, 3, 4))], pars=[row(p["norm2_g"], l)],
                       outs=[(D, f32), (D, bf16)], n_lat=T)

    return seg


def _ffn_segment(l, depth, T, tgt):
    row = _row

    def seg(xs, h2, mod, p, w_up, w_down):
        D = xs.shape[1]
        up = mm(h2, w_up, "w_up%d" % l, out_dtype=bf16)
        act = convact(up, p["conv_w"][l], row(p["conv_b"], l), T, "convact%d" % l)
        z = mm(act, w_down, "w_down%d" % l)
        if l + 1 < depth:
            return rowwise("resid_b%d" % l, f_resid_modnorm, [xs, z],
                           sels=[jnp.concatenate([_sel(mod, l, (5,)), _sel(mod, l + 1, (0, 1))], axis=1)],
                           pars=[row(p["norm1_g"], l + 1)], outs=[(D, f32), (D, bf16)], n_lat=T)
        return rowwise("resid_final", f_resid_final, [xs[:T], z[:T]], consts=[tgt], sels=[_sel(mod, l, (5,))[0:1]],
                       pars=[p["final_norm_g"][None, :]], outs=[(1, f32)])

    return seg


SEGMENT_PARAMS = ("norm1_g", "q_norm_g", "k_norm_g", "hg_norm_g", "sg_norm_g", "sg_w", "sg_b", "norm2_g", "conv_w", "conv_b",
                  "final_norm_g")


def _prologue(p, c_all):
    depth = p["norm1_g"].shape[0]
    mod, vjp_mod = jax.vjp(lambda cc, wa, ba: ada_mod(c_all, cc, wa, ba), p["c_ctx"], p["w_ada"], p["b_ada"])
    lbs, vjp_lbs = jax.vjp(lambda hg: lower_bounds([hg[:, l] for l in range(depth)]), p["hg_lower_bounds"])
    return mod, vjp_mod, lbs, vjp_lbs


def _loss_and_grads(p, prologue, big, x, ctx, tgt, cosf, sinf):
    T = x.shape[0]
    depth = p["norm1_g"].shape[0]
    add = lambda a, b: jax.tree.map(jnp.add, a, b)
    small = {n: p[n] for n in SEGMENT_PARAMS}
    mod, vjp_mod, lbs, vjp_lbs = prologue
    (xs, h), vjp_first = jax.vjp(_first_segment(T, ctx), x, mod, small)
    vjps = []
    for l in range(depth):
        w_in, w_out = big.get(l, "w_in", h), big.get(l, "w_out", h)
        (xs, h2), vj = jax.vjp(_mixer_segment(l, T, cosf, sinf), xs, h, mod, lbs[l], small, w_in, w_out)
        vjps.append(vj)
        w_up, w_down = big.get(l, "w_up", h2), big.get(l, "w_down", h2)
        out, vj = jax.vjp(_ffn_segment(l, depth, T, tgt), xs, h2, mod, small, w_up, w_down)
        vjps.append(vj)
        if l + 1 < depth:
            xs, h = out
    (rowloss,) = out
    loss = 0.5 * jnp.sum(rowloss)

    ct = (jnp.full(rowloss.shape, 0.5, f32),)
    d_mod, d_small, d_lbs = jnp.zeros_like(mod), jax.tree.map(jnp.zeros_like, small), [None] * depth
    for l in range(depth - 1, -1, -1):
        dxs, dh2, dm, ds, d_up, d_down = vjps[2 * l + 1](ct)
        tokens = (big.start_scatter(l, "w_up", d_up), big.start_scatter(l, "w_down", d_down))
        dxs, dh2, _ = lax.optimization_barrier((dxs, dh2, tokens))
        d_mod, d_small = d_mod + dm, add(d_small, ds)
        dxs, dh, dm, d_lbs[l], ds, d_in, d_out = vjps[2 * l]((dxs, dh2))
        tokens = (big.start_scatter(l, "w_in", d_in), big.start_scatter(l, "w_out", d_out))
        dxs, dh, _ = lax.optimization_barrier((dxs, dh, tokens))
        d_mod, d_small = d_mod + dm, add(d_small, ds)
        ct = (dxs, dh)
    dx, dm, ds = vjp_first(ct)
    d_cc, d_wada, d_bada = vjp_mod(d_mod + dm)
    (d_hg,) = vjp_lbs(tuple(d_lbs))
    grads = dict(add(d_small, ds), c_ctx=d_cc, w_ada=d_wada, b_ada=d_bada, hg_lower_bounds=d_hg)
    return loss, grads, dx


def kernel(x, c, ctx, c_ctx, w_ada, b_ada, norm1_g, w_in, q_norm_g, k_norm_g, hg_lower_bounds, hg_norm_g, sg_norm_g, sg_w, sg_b, w_out, norm2_g, w_up, conv_w, conv_b, w_down, final_norm_g, loss_target, m_c_ctx, m_w_ada, m_b_ada, m_norm1_g, m_w_in, m_q_norm_g, m_k_norm_g, m_hg_lower_bounds, m_hg_norm_g, m_sg_norm_g, m_sg_w, m_sg_b, m_w_out, m_norm2_g, m_w_up, m_conv_w, m_conv_b, m_w_down, m_final_norm_g, v_c_ctx, v_w_ada, v_b_ada, v_norm1_g, v_w_in, v_q_norm_g, v_k_norm_g, v_hg_lower_bounds, v_hg_norm_g, v_sg_norm_g, v_sg_w, v_sg_b, v_w_out, v_norm2_g, v_w_up, v_conv_w, v_conv_b, v_w_down, v_final_norm_g):
    given = dict(c_ctx=c_ctx, w_ada=w_ada, b_ada=b_ada, norm1_g=norm1_g, w_in=w_in, q_norm_g=q_norm_g, k_norm_g=k_norm_g,
                 hg_lower_bounds=hg_lower_bounds, hg_norm_g=hg_norm_g, sg_norm_g=sg_norm_g, sg_w=sg_w, sg_b=sg_b, w_out=w_out,
                 norm2_g=norm2_g, w_up=w_up, conv_w=conv_w, conv_b=conv_b, w_down=w_down, final_norm_g=final_norm_g)
    moments_m = dict(c_ctx=m_c_ctx, w_ada=m_w_ada, b_ada=m_b_ada, norm1_g=m_norm1_g, w_in=m_w_in, q_norm_g=m_q_norm_g,
                     k_norm_g=m_k_norm_g, hg_lower_bounds=m_hg_lower_bounds, hg_norm_g=m_hg_norm_g, sg_norm_g=m_sg_norm_g,
                     sg_w=m_sg_w, sg_b=m_sg_b, w_out=m_w_out, norm2_g=m_norm2_g, w_up=m_w_up, conv_w=m_conv_w, conv_b=m_conv_b,
                     w_down=m_w_down, final_norm_g=m_final_norm_g)
    moments_v = dict(c_ctx=v_c_ctx, w_ada=v_w_ada, b_ada=v_b_ada, norm1_g=v_norm1_g, w_in=v_w_in, q_norm_g=v_q_norm_g,
                     k_norm_g=v_k_norm_g, hg_lower_bounds=v_hg_lower_bounds, hg_norm_g=v_hg_norm_g, sg_norm_g=v_sg_norm_g,
                     sg_w=v_sg_w, sg_b=v_sg_b, w_out=v_w_out, norm2_g=v_norm2_g, w_up=v_w_up, conv_w=v_conv_w, conv_b=v_conv_b,
                     w_down=v_w_down, final_norm_g=v_final_norm_g)
    T, D = x.shape[1], x.shape[2]
    L = ctx.shape[1]
    me = _my_index()
    axes = ("x", "y", "c")

    c_all = allgather_small(jnp.pad(c, ((0, 7), (0, 0))), "gather_c").reshape(N_DEV, 8, D)[:, 0]
    depth, hw = hg_lower_bounds.shape[1], hg_lower_bounds.shape[2]
    cw = conv_w.shape[2]
    small = jnp.concatenate([jnp.pad(hg_lower_bounds.reshape(2 * depth, hw), ((0, 0), (0, cw - hw))), conv_w.reshape(3 * depth, cw)], axis=0)
    rows_small = small.shape[0]
    small = allgather_small(jnp.pad(small, ((0, (-rows_small) % 8), (0, 0))), "gather_small").reshape(N_DEV, -1, cw)
    hg_full = small[:, :2 * depth, :hw].reshape(N_DEV, 2, depth, hw).transpose(1, 2, 0, 3).reshape(2, depth, N_DEV * hw)
    cw_full = small[:, 2 * depth:2 * depth + 3 * depth].reshape(N_DEV, depth, 3, cw).transpose(1, 2, 0, 3).reshape(depth, 3, N_DEV * cw)

    p = {n: a for n, a in given.items() if n not in dict(BIG)}
    p.update(hg_lower_bounds=hg_full, conv_w=cw_full, c_ctx=c_ctx[None, :])
    prologue = _prologue(p, c_all)
    big = BigWeights(given, me, after=(prologue[0], cw_full))
    cosf, sinf = _rope_tables(T, L)
    loss, gp, gx = _loss_and_grads(p, prologue, big, x[0], ctx[0], loss_target[0], cosf, sinf)
    loss = lax.psum(loss, axes)

    grads, delta, new_m, new_v = dict(gp), {}, {}, {}
    partial = ['c_ctx', 'norm1_g', 'q_norm_g', 'k_norm_g', 'hg_lower_bounds', 'hg_norm_g', 'sg_norm_g', 'sg_w', 'sg_b',
               'norm2_g', 'conv_w', 'conv_b', 'final_norm_g']
    grads.update(zip(partial, allreduce_small([gp[n] for n in partial], "reduce_small")))
    grads['c_ctx'] = grads['c_ctx'][0]
    grads['hg_lower_bounds'] = lax.dynamic_slice_in_dim(grads['hg_lower_bounds'], me * hw, hw, axis=2)
    grads['conv_w'] = lax.dynamic_slice_in_dim(grads['conv_w'], me * cw, cw, axis=2)
    last = gx
    for n in [w for w in WEIGHTS if w not in dict(BIG)]:
        delta[n], new_m[n], new_v[n] = adamw(given[n], grads[n], moments_m[n], moments_v[n], "adamw_" + n)
        last = delta[n]

    big_outs = {n: None for n, _ in BIG}
    for l in range(depth - 1, -1, -1):
        for n in ("w_down", "w_up", "w_out", "w_in"):
            big_outs[n] = adamw_layer(l, big.contributions(l, n, last), given[n], moments_m[n], moments_v[n], big_outs[n],
                                      "adamw_%s%d" % (n, l))
            last = big_outs[n][1]
    for n, _ in BIG:
        grads[n], delta[n], new_m[n], new_v[n] = big_outs[n]
    return (loss, gx[None], *[grads[n] for n in WEIGHTS], *[delta[n] for n in WEIGHTS],
            *[new_m[n] for n in WEIGHTS], *[new_v[n] for n in WEIGHTS])
```

```python
import functools

import jax
import jax.numpy as jnp
from jax import lax
from jax.experimental import pallas as pl
from jax.experimental.pallas import tpu as pltpu

f32 = jnp.float32
bf16 = jnp.bfloat16
HI = lax.Precision.HIGHEST
MESH = pl.DeviceIdType.MESH

EPS = 1e-6
F_MIN = 1e-30
GRID_W = 64
ROPE_THETA = 10000.0
HEAD = 128
ATTN_HEADS, ATTN_KV = 8, 2
ATTN_GROUP = ATTN_HEADS // ATTN_KV
HG_HEADS = 4
SG_GROUPS = 4
SG_CHUNK = 128
HG_CHUNK = 16
HG_GROUP = 16
IN_SIZES = (1024, 256, 256, 512, 512, 512, 512, 512, 512, 512)
N_DEV = 8
ROW_BLOCK = 256
MAX_TK = 2816
ADAM_LR, ADAM_B1, ADAM_B2, ADAM_EPS, ADAM_WD, ADAM_STEP = 0.001, 0.9, 0.999, 1e-08, 0.01, 10

WEIGHTS = ['c_ctx', 'w_ada', 'b_ada', 'norm1_g', 'w_in', 'q_norm_g', 'k_norm_g', 'hg_lower_bounds', 'hg_norm_g',
           'sg_norm_g', 'sg_w', 'sg_b', 'w_out', 'norm2_g', 'w_up', 'conv_w', 'conv_b', 'w_down', 'final_norm_g']


def _pick(dim, cands):
    for t in cands:
        if dim % t == 0:
            return t
    return dim


def _my_index():
    return 4 * lax.axis_index("x") + 2 * lax.axis_index("y") + lax.axis_index("c")


def _mm_call(a, b, mode, out_dtype, name):
    if mode == "nn":
        (M, K), N = a.shape, b.shape[1]
    elif mode == "nt":
        (M, K), N = a.shape, b.shape[0]
    else:
        (K, M), N = a.shape, b.shape[1]
    tm = _pick(M, (1088, 1024, 512, 256, 128))
    tn = _pick(N, (1024, 512, 256, 128))
    tk = K if K <= MAX_TK else _pick(K, (2816, 2560, 2176, 2048, 1408, 1088, 1024, 512, 256, 128))
    nk = K // tk
    dims = {"nn": (((1,), (0,)), ((), ())), "nt": (((1,), (1,)), ((), ())), "tn": (((0,), (0,)), ((), ()))}[mode]

    def body(a_ref, b_ref, o_ref, *acc):
        prod = lax.dot_general(a_ref[...].astype(bf16), b_ref[...].astype(bf16), dims, preferred_element_type=f32)
        if nk == 1:
            o_ref[...] = prod.astype(o_ref.dtype)
            return
        k = pl.program_id(2)

        @pl.when(k == 0)
        def _():
            acc[0][...] = prod

        @pl.when((k > 0) & (k < nk - 1))
        def _():
            acc[0][...] += prod

        @pl.when(k == nk - 1)
        def _():
            o_ref[...] = (acc[0][...] + prod).astype(o_ref.dtype)

    a_spec = pl.BlockSpec((tk, tm), lambda i, j, k: (k, i)) if mode == "tn" else pl.BlockSpec((tm, tk), lambda i, j, k: (i, k))
    b_spec = pl.BlockSpec((tn, tk), lambda i, j, k: (j, k)) if mode == "nt" else pl.BlockSpec((tk, tn), lambda i, j, k: (k, j))
    return pl.pallas_call(
        body, name=name, grid=(M // tm, N // tn, nk),
        in_specs=[a_spec, b_spec], out_specs=pl.BlockSpec((tm, tn), lambda i, j, k: (i, j)),
        out_shape=jax.ShapeDtypeStruct((M, N), out_dtype),
        scratch_shapes=[pltpu.VMEM((tm, tn), f32)] if nk > 1 else [],
        compiler_params=pltpu.CompilerParams(dimension_semantics=("parallel", "parallel", "arbitrary")),
    )(a, b)


def mm(a, w, name, out_dtype=f32, split=None):
    def parts(y):
        if split is None:
            return y
        offs = [sum(split[:i]) for i in range(len(split))]
        return tuple(y[:, o:o + s] for o, s in zip(offs, split))

    @jax.custom_vjp
    def op(a, w):
        return parts(_mm_call(a, w, "nn", out_dtype, name + "_fwd"))

    def fwd(a, w):
        return op(a, w), (a, w)

    def bwd(res, dy):
        a, w = res
        dy = dy.astype(bf16) if split is None else jnp.concatenate([d.astype(bf16) for d in dy], axis=1)
        return _mm_call(dy, w, "nt", a.dtype, name + "_bwd_a"), _mm_call(a, dy, "tn", w.dtype, name + "_bwd_w")

    op.defvjp(fwd, bwd)
    return op(a, w)


def _rowwise_specs(rows, consts, sels, pars, tb, nlb):
    specs = [pl.BlockSpec((tb, a.shape[1]), lambda i: (i, 0)) for a in (*rows, *consts)]
    specs += [pl.BlockSpec((None,) + a.shape[1:], lambda i: (jnp.where(i >= nlb, 1, 0), 0, 0)) for a in sels]
    specs += [pl.BlockSpec(a.shape, functools.partial(lambda i, n: (0,) * n, n=a.ndim)) for a in pars]
    return specs


def rowwise(name, f, rows, consts=(), sels=(), pars=(), outs=(), n_lat=None, tb=ROW_BLOCK):
    rows, consts, sels, pars = tuple(rows), tuple(consts), tuple(sels), tuple(pars)
    R = rows[0].shape[0]
    nb = R // tb
    nlb = nb if n_lat is None else n_lat // tb
    n_in = len(rows) + len(consts) + len(sels) + len(pars)
    n_out = len(outs)
    out_dtypes = [d for _, d in outs]
    out_specs = [pl.BlockSpec((tb, w), lambda i: (i, 0)) for w, _ in outs]
    out_shape = [jax.ShapeDtypeStruct((R, w), d) for w, d in outs]
    sem = pltpu.CompilerParams(dimension_semantics=("arbitrary",))

    def run_fwd(rows, consts, sels, pars):
        def body(*refs):
            res = f(*[r[...] for r in refs[:n_in]])
            for o_ref, r in zip(refs[n_in:], res):
                o_ref[...] = r.astype(o_ref.dtype)

        return tuple(pl.pallas_call(
            body, name=name + "_fwd", grid=(nb,), in_specs=_rowwise_specs(rows, consts, sels, pars, tb, nlb),
            out_specs=out_specs, out_shape=out_shape, compiler_params=sem,
        )(*rows, *consts, *sels, *pars))

    def run_bwd(rows, consts, sels, pars, cts):
        nr, nc, ns, npar = len(rows), len(consts), len(sels), len(pars)

        def body(*refs):
            i = pl.program_id(0)
            ins = [r[...] for r in refs[:n_in]]
            ct = tuple(r[...] for r in refs[n_in:n_in + n_out])
            o_refs = refs[n_in + n_out:]
            cvals = ins[nr:nr + nc]

            def g(*d):
                res = f(*d[:nr], *cvals, *d[nr:])
                return tuple(r.astype(t) for r, t in zip(res, out_dtypes))

            _, vjp = jax.vjp(g, *ins[:nr], *ins[nr + nc:])
            grads = vjp(ct)
            for k in range(nr):
                o_refs[k][...] = grads[k].astype(o_refs[k].dtype)
            for k in range(nr, nr + ns + npar):
                first = (i == 0) | (i == nlb) if k < nr + ns else (i == 0)
                gk = grads[k].astype(f32)

                @pl.when(first)
                def _(k=k, gk=gk):
                    o_refs[k][...] = gk

                @pl.when(jnp.logical_not(first))
                def _(k=k, gk=gk):
                    o_refs[k][...] += gk

        in_specs = _rowwise_specs(rows, consts, sels, pars, tb, nlb) + out_specs
        o_specs = [pl.BlockSpec((tb, a.shape[1]), lambda i: (i, 0)) for a in rows]
        o_specs += [pl.BlockSpec((None,) + a.shape[1:], lambda i: (jnp.where(i >= nlb, 1, 0), 0, 0)) for a in sels]
        o_specs += [pl.BlockSpec(a.shape, functools.partial(lambda i, n: (0,) * n, n=a.ndim)) for a in pars]
        o_shape = [jax.ShapeDtypeStruct(a.shape, a.dtype) for a in rows]
        o_shape += [jax.ShapeDtypeStruct(a.shape, f32) for a in (*sels, *pars)]
        res = pl.pallas_call(
            body, name=name + "_bwd", grid=(nb,), in_specs=in_specs, out_specs=o_specs, out_shape=o_shape,
            compiler_params=sem,
        )(*rows, *consts, *sels, *pars, *cts)
        return tuple(res[:nr]), tuple(res[nr:nr + ns]), tuple(res[nr + ns:])

    @jax.custom_vjp
    def op(rows, consts, sels, pars):
        return run_fwd(rows, consts, sels, pars)

    def fwd(rows, consts, sels, pars):
        return op(rows, consts, sels, pars), (rows, consts, sels, pars)

    def bwd(res, cts):
        rows, consts, sels, pars = res
        drows, dsels, dpars = run_bwd(rows, consts, sels, pars, tuple(cts))
        return drows, tuple(jnp.zeros_like(c) for c in consts), dsels, dpars

    op.defvjp(fwd, bwd)
    return op(rows, consts, sels, pars)


def _rms(x, g):
    return x * lax.rsqrt(jnp.mean(x * x, axis=-1, keepdims=True) + EPS) * g


def _silu(x):
    return x * jax.nn.sigmoid(x)


def f_modnorm(x, mods, g):
    return (_rms(x, g) * (1.0 + mods[1:2]) + mods[0:1],)


def f_resid_modnorm(x, y, mods, g):
    xn = x + mods[0:1] * y
    return xn, _rms(xn, g) * (1.0 + mods[2:3]) + mods[1:2]


def f_resid_final(x, y, tgt, mods, g):
    xn = x + mods[0:1] * y
    err = _rms(xn, g) - tgt
    return (jnp.mean(err * err, axis=-1, keepdims=True),)


def f_qkprep(aq, ak, av, cosf, sinf, qg, kg):
    r = lax.broadcasted_iota(jnp.int32, (HEAD, HEAD), 0)
    c = lax.broadcasted_iota(jnp.int32, (HEAD, HEAD), 1)
    swap = jnp.where((r ^ 1) == c, 1.0, 0.0).astype(f32)

    def head(xh, g):
        y = _rms(xh, g)
        ys = jnp.dot(y, swap, precision=HI, preferred_element_type=f32)
        return y * cosf + ys * sinf

    q = jnp.concatenate([head(aq[:, h * HEAD:(h + 1) * HEAD], qg) for h in range(ATTN_HEADS)], axis=1)
    k = jnp.concatenate([head(ak[:, h * HEAD:(h + 1) * HEAD], kg) for h in range(ATTN_KV)], axis=1)
    return q, k, av


def f_hgout(of, ob, gt, g):
    o = of + ob
    y = jnp.concatenate([_rms(o[:, h * HEAD:(h + 1) * HEAD], g) for h in range(HG_HEADS)], axis=1)
    return (y * _silu(gt),)


def f_sgate(u, v, g, w, b0, b1, b2, b3):
    u = jax.nn.gelu(u)
    v = jax.nn.gelu(v)
    bs = (b0, b1, b2, b3)
    cols = []
    for gi in range(SG_GROUPS):
        sl = slice(gi * HEAD, (gi + 1) * HEAD)
        vg = _rms(v[:, sl], g[:, sl])
        parts = []
        for n in range(v.shape[0] // SG_CHUNK):
            vc = vg[n * SG_CHUNK:(n + 1) * SG_CHUNK]
            parts.append(jnp.dot(w[gi].astype(bf16), vc.astype(bf16), preferred_element_type=f32) + bs[gi])
        cols.append(jnp.concatenate(parts, axis=0))
    return (u * jnp.concatenate(cols, axis=1),)


def attention(q, k, v, T, name, tq=ROW_BLOCK):
    NT = q.shape[0]
    nqb, nlb = NT // tq, T // tq
    scale = HEAD ** -0.5
    q_spec = pl.BlockSpec((tq, HEAD), lambda kv, g, i: (i, kv * ATTN_GROUP + g))
    kv_spec = pl.BlockSpec((NT, HEAD), lambda kv, g, i: (0, kv))
    lse_spec = pl.BlockSpec((None, tq, 1), lambda kv, g, i: (kv * ATTN_GROUP + g, i, 0))
    grid = (ATTN_KV, ATTN_GROUP, nqb)
    nt_dims = (((1,), (1,)), ((), ()))
    tn_dims = (((0,), (0,)), ((), ()))

    def on_keys(i, fn):
        @pl.when(i < nlb)
        def _():
            fn(pl.ds(0, NT))

        @pl.when(i >= nlb)
        def _():
            fn(pl.ds(T, NT - T))

    def run_fwd(q, k, v):
        def body(q_ref, k_ref, v_ref, o_ref, lse_ref):
            def run(rows):
                s = lax.dot_general(q_ref[...], k_ref[rows, :], nt_dims, preferred_element_type=f32)
                m = jnp.max(s, axis=-1, keepdims=True) * scale
                p = jnp.exp(s * scale - m)
                l = jnp.sum(p, axis=-1, keepdims=True)
                o = jnp.dot(p.astype(bf16), v_ref[rows, :], preferred_element_type=f32) / l
                o_ref[...] = o.astype(o_ref.dtype)
                lse_ref[...] = m + jnp.log(l)

            on_keys(pl.program_id(2), run)

        return pl.pallas_call(
            body, name=name + "_fwd", grid=grid, in_specs=[q_spec, kv_spec, kv_spec], out_specs=[q_spec, lse_spec],
            out_shape=[jax.ShapeDtypeStruct(q.shape, bf16), jax.ShapeDtypeStruct((ATTN_HEADS, NT, 1), f32)],
            compiler_params=pltpu.CompilerParams(dimension_semantics=("parallel", "parallel", "arbitrary")),
        )(q, k, v)

    def run_bwd(q, k, v, lse, do):
        def body(q_ref, k_ref, v_ref, lse_ref, do_ref, dq_ref, dk_ref, dv_ref):
            g, i = pl.program_id(1), pl.program_id(2)

            @pl.when((g == 0) & (i == 0))
            def _():
                dk_ref[...] = jnp.zeros_like(dk_ref)
                dv_ref[...] = jnp.zeros_like(dv_ref)

            def run(rows):
                qb, kb, vb, dob = q_ref[...], k_ref[rows, :], v_ref[rows, :], do_ref[...]
                s = lax.dot_general(qb, kb, nt_dims, preferred_element_type=f32)
                p = jnp.exp(s * scale - lse_ref[...])
                dp = lax.dot_general(dob, vb, nt_dims, preferred_element_type=f32)
                ds = (p * (dp - jnp.sum(p * dp, axis=-1, keepdims=True)) * scale).astype(bf16)
                dq_ref[...] = jnp.dot(ds, kb, preferred_element_type=f32).astype(dq_ref.dtype)
                dk_ref[rows, :] += lax.dot_general(ds, qb, tn_dims, preferred_element_type=f32)
                dv_ref[rows, :] += lax.dot_general(p.astype(bf16), dob, tn_dims, preferred_element_type=f32)

            on_keys(i, run)

        return pl.pallas_call(
            body, name=name + "_bwd", grid=grid, in_specs=[q_spec, kv_spec, kv_spec, lse_spec, q_spec],
            out_specs=[q_spec, kv_spec, kv_spec],
            out_shape=[jax.ShapeDtypeStruct(q.shape, bf16), jax.ShapeDtypeStruct(k.shape, f32), jax.ShapeDtypeStruct(v.shape, f32)],
            compiler_params=pltpu.CompilerParams(dimension_semantics=("parallel", "arbitrary", "arbitrary")),
        )(q, k, v, lse, do)

    @jax.custom_vjp
    def op(q, k, v):
        return run_fwd(q, k, v)[0]

    def fwd(q, k, v):
        o, lse = run_fwd(q, k, v)
        return o, (q, k, v, lse)

    def bwd(res, do):
        dq, dk, dv = run_bwd(*res, do)
        return dq, dk.astype(bf16), dv.astype(bf16)

    op.defvjp(fwd, bwd)
    return op(q, k, v)


def _bdot(a, b, ca, cb):
    fa, fb = 3 - ca, 3 - cb

    def dot(x, y, cx, cy):
        return lax.dot_general(x.astype(bf16), y.astype(bf16), (((cx,), (cy,)), ((0,), (0,))), preferred_element_type=f32)

    @jax.custom_vjp
    def op(a, b):
        return dot(a, b, ca, cb)

    def fwd(a, b):
        return op(a, b), (a, b)

    def bwd(res, ct):
        a, b = res
        da = dot(ct, b, 2, fb) if ca == 2 else dot(b, ct, fb, 2)
        db = dot(a, ct, fa, 1) if cb == 1 else dot(ct, a, 1, fa)
        return da, db

    op.defvjp(fwd, bwd)
    return op(a, b)


def _chunk_cumsum(x, rev):
    def impl(x, rev):
        n = x.shape[0]
        pos = lax.broadcasted_iota(jnp.int32, x.shape, 0) % HG_CHUNK
        s = 1
        while s < HG_CHUNK:
            if rev:
                x = x + jnp.where(pos < HG_CHUNK - s, pltpu.roll(x, n - s, 0), 0.0)
            else:
                x = x + jnp.where(pos >= s, pltpu.roll(x, s, 0), 0.0)
            s *= 2
        return x

    @jax.custom_vjp
    def op(x):
        return impl(x, rev)

    op.defvjp(lambda x: (op(x), None), lambda _, ct: (impl(ct, not rev),))
    return op(x)


def _hg_group(St, hq, hf, hi, lb, *, rev):
    G, C = HG_GROUP, HG_CHUNK
    R = G * C
    q = _silu(hq)
    f = lb + (1.0 - lb) * jax.nn.sigmoid(hf)
    logf = jnp.log(jnp.maximum(f, F_MIN))
    kk = (1.0 - lb) * jax.nn.sigmoid(-hf)
    b3 = _chunk_cumsum(logf, rev).reshape(G, C, HEAD)
    q3, k3, v3 = q.reshape(G, C, HEAD), kk.reshape(G, C, HEAD), hi.reshape(G, C, HEAD)
    btot = jnp.sum(logf.reshape(G, C, HEAD), axis=1)
    tt = lax.broadcasted_iota(jnp.int32, (G, C, C, HEAD), 1)
    ss = lax.broadcasted_iota(jnp.int32, (G, C, C, HEAD), 2)
    mask = (ss >= tt) if rev else (ss <= tt)
    diff = b3[:, :, None, :] - b3[:, None, :, :]
    dec = jnp.where(mask, jnp.exp(jnp.where(mask, diff, 0.0)), 0.0)
    scores = jnp.sum(q3[:, :, None, :] * k3[:, None, :, :] * dec, axis=-1)
    o_intra = _bdot(scores, v3, 2, 1)
    q_dec = q3 * jnp.exp(b3)
    k_dec = k3 * jnp.exp(btot[:, None, :] - b3)
    kvt = _bdot(v3, k_dec, 1, 1)
    dl = jnp.exp(btot)
    states = [None] * G
    for g in (range(G - 1, -1, -1) if rev else range(G)):
        states[g] = St
        St = St * dl[g:g + 1, :] + kvt[g]
    o_inter = _bdot(q_dec, jnp.stack(states), 2, 2)
    return St, (o_intra + o_inter).reshape(R, HEAD)


def hgrn(hq, hf, hi, lb, T, rev, name):
    NT, W = hq.shape
    R = HG_GROUP * HG_CHUNK
    n_lat, n_ctx = T // R, (NT - T) // R
    nG = n_lat + n_ctx

    def group_of(j):
        if rev:
            return jnp.where(j < n_ctx, nG - 1 - j, n_lat - 1 - (j - n_ctx))
        return jnp.where(j < n_ctx, n_lat + j, j - n_ctx)

    def rows_of(j):
        return pl.ds(pl.multiple_of(group_of(j) * R, R), R)

    col_spec = pl.BlockSpec((NT, HEAD), lambda h: (0, h))
    lb_spec = pl.BlockSpec((1, HEAD), lambda h: (0, h))
    st_spec = pl.BlockSpec((None, nG, HEAD, HEAD), lambda h: (h, 0, 0, 0))
    sem = pltpu.CompilerParams(dimension_semantics=("parallel",))

    def run_fwd(hq, hf, hi, lb):
        def body(hq_ref, hf_ref, hi_ref, lb_ref, o_ref, st_ref):
            def step(j, St):
                st_ref[j] = St
                rows = rows_of(j)
                St, o = _hg_group(St, hq_ref[rows, :], hf_ref[rows, :], hi_ref[rows, :], lb_ref[...], rev=rev)
                o_ref[rows, :] = o
                return St

            lax.fori_loop(0, nG, step, jnp.zeros((HEAD, HEAD), f32))

        return pl.pallas_call(
            body, name=name + "_fwd", grid=(W // HEAD,), in_specs=[col_spec, col_spec, col_spec, lb_spec],
            out_specs=[col_spec, st_spec],
            out_shape=[jax.ShapeDtypeStruct((NT, W), f32), jax.ShapeDtypeStruct((W // HEAD, nG, HEAD, HEAD), f32)],
            compiler_params=sem,
        )(hq, hf, hi, lb)

    def run_bwd(hq, hf, hi, lb, st, do):
        def body(hq_ref, hf_ref, hi_ref, lb_ref, st_ref, do_ref, dq_ref, df_ref, di_ref, dlb_ref):
            def step(jj, carry):
                dS, dlb = carry
                j = nG - 1 - jj
                rows = rows_of(j)
                _, vjp = jax.vjp(functools.partial(_hg_group, rev=rev), st_ref[j], hq_ref[rows, :], hf_ref[rows, :],
                                 hi_ref[rows, :], lb_ref[...])
                dS, dq, df, di, dl = vjp((dS, do_ref[rows, :]))
                dq_ref[rows, :] = dq
                df_ref[rows, :] = df
                di_ref[rows, :] = di
                return dS, dlb + dl

            _, dlb = lax.fori_loop(0, nG, step, (jnp.zeros((HEAD, HEAD), f32), jnp.zeros((1, HEAD), f32)))
            dlb_ref[...] = dlb

        return pl.pallas_call(
            body, name=name + "_bwd", grid=(W // HEAD,),
            in_specs=[col_spec, col_spec, col_spec, lb_spec, st_spec, col_spec],
            out_specs=[col_spec, col_spec, col_spec, lb_spec],
            out_shape=[jax.ShapeDtypeStruct((NT, W), f32)] * 3 + [jax.ShapeDtypeStruct((1, W), f32)],
            compiler_params=sem,
        )(hq, hf, hi, lb, st, do)

    @jax.custom_vjp
    def op(hq, hf, hi, lb):
        return run_fwd(hq, hf, hi, lb)[0]

    def fwd(hq, hf, hi, lb):
        o, st = run_fwd(hq, hf, hi, lb)
        return o, (hq, hf, hi, lb, st)

    def bwd(res, do):
        return tuple(run_bwd(*res, do))

    op.defvjp(fwd, bwd)
    return op(hq, hf, hi, lb)


def lower_bounds(params):
    n = len(params)

    def f(*a):
        m = functools.reduce(jnp.maximum, a)
        e = [jnp.exp(x - m) for x in a]
        s = functools.reduce(lambda u, v: u + v, e)
        p = [x / s for x in e]
        out, run = [], jnp.zeros_like(p[0])
        for l in range(n):
            out.append(run)
            run = run + p[l]
        return tuple(out[l] + p[l] - p[0] for l in range(n))

    shape = [jax.ShapeDtypeStruct(params[0].shape, f32)] * n

    @jax.custom_vjp
    def op(*a):
        def body(*refs):
            for o_ref, r in zip(refs[n:], f(*[x[...] for x in refs[:n]])):
                o_ref[...] = r
        return tuple(pl.pallas_call(body, name="lower_bounds_fwd", out_shape=shape)(*a))

    def fwd(*a):
        return op(*a), a

    def bwd(a, cts):
        def body(*refs):
            _, vjp = jax.vjp(f, *[x[...] for x in refs[:n]])
            for o_ref, r in zip(refs[2 * n:], vjp(tuple(x[...] for x in refs[n:2 * n]))):
                o_ref[...] = r
        return tuple(pl.pallas_call(body, name="lower_bounds_bwd", out_shape=shape)(*a, *cts))

    op.defvjp(fwd, bwd)
    return op(*params)


def _shift_rows(x, d, T):
    n = x.shape[0]
    t = lax.broadcasted_iota(jnp.int32, x.shape, 0)
    y = pltpu.roll(x, d % n, 0)
    edge = ((t == 0) | (t == T)) if d == 1 else ((t == T - 1) | (t == n - 1))
    return jnp.where(edge, 0.0, y)


def _conv(x, w, b, T):
    return b + w[0:1] * _shift_rows(x, 1, T) + w[1:2] * x + w[2:3] * _shift_rows(x, -1, T)


def convact(up, cw, cb, T, name, tc=128):
    NT, F2 = up.shape
    F = F2 // 2
    tc = _pick(F, (tc, 128))
    nf = F // tc
    g_spec = lambda r: pl.BlockSpec((r, tc), lambda j: (0, j))
    v_spec = lambda r: pl.BlockSpec((r, tc), lambda j: (0, j + nf))
    sem = pltpu.CompilerParams(dimension_semantics=("parallel",))

    def run_fwd(up, cw, cb):
        def body(xg_ref, xv_ref, wg_ref, wv_ref, bg_ref, bv_ref, o_ref):
            yg = _conv(xg_ref[...].astype(f32), wg_ref[...], bg_ref[...], T)
            yv = _conv(xv_ref[...].astype(f32), wv_ref[...], bv_ref[...], T)
            o_ref[...] = (_silu(yg) * yv).astype(o_ref.dtype)

        return pl.pallas_call(
            body, name=name + "_fwd", grid=(nf,),
            in_specs=[g_spec(NT), v_spec(NT), g_spec(3), v_spec(3), g_spec(1), v_spec(1)], out_specs=g_spec(NT),
            out_shape=jax.ShapeDtypeStruct((NT, F), bf16), compiler_params=sem,
        )(up, up, cw, cw, cb, cb)

    def run_bwd(up, cw, cb, dact):
        def body(xg_ref, xv_ref, wg_ref, wv_ref, bg_ref, bv_ref, da_ref, dxg_ref, dxv_ref, dwg_ref, dwv_ref, dbg_ref, dbv_ref):
            xg, xv, wg, wv = xg_ref[...].astype(f32), xv_ref[...].astype(f32), wg_ref[...], wv_ref[...]
            yg = _conv(xg, wg, bg_ref[...], T)
            yv = _conv(xv, wv, bv_ref[...], T)
            da = da_ref[...].astype(f32)
            sg = jax.nn.sigmoid(yg)
            dyv = da * yg * sg
            dyg = da * yv * sg * (1.0 + yg * (1.0 - sg))
            for x, w, dy, dx_ref, dw_ref, db_ref in ((xg, wg, dyg, dxg_ref, dwg_ref, dbg_ref), (xv, wv, dyv, dxv_ref, dwv_ref, dbv_ref)):
                dx_ref[...] = (w[0:1] * _shift_rows(dy, -1, T) + w[1:2] * dy + w[2:3] * _shift_rows(dy, 1, T)).astype(dx_ref.dtype)
                dw_ref[...] = jnp.concatenate([
                    jnp.sum(dy * _shift_rows(x, 1, T), axis=0, keepdims=True),
                    jnp.sum(dy * x, axis=0, keepdims=True),
                    jnp.sum(dy * _shift_rows(x, -1, T), axis=0, keepdims=True)], axis=0)
                db_ref[...] = jnp.sum(dy, axis=0, keepdims=True)

        return pl.pallas_call(
            body, name=name + "_bwd", grid=(nf,),
            in_specs=[g_spec(NT), v_spec(NT), g_spec(3), v_spec(3), g_spec(1), v_spec(1), g_spec(NT)],
            out_specs=[g_spec(NT), g_spec(NT), g_spec(3), g_spec(3), g_spec(1), g_spec(1)],
            out_shape=[jax.ShapeDtypeStruct((NT, F), up.dtype)] * 2 + [jax.ShapeDtypeStruct((3, F), f32)] * 2 + [jax.ShapeDtypeStruct((1, F), f32)] * 2,
            compiler_params=sem,
        )(up, up, cw, cw, cb, cb, dact)

    @jax.custom_vjp
    def op(up, cw, cb):
        return run_fwd(up, cw, cb)

    def fwd(up, cw, cb):
        return op(up, cw, cb), (up, cw, cb)

    def bwd(res, dact):
        dxg, dxv, dwg, dwv, dbg, dbv = run_bwd(*res, dact)
        return (jnp.concatenate([dxg, dxv], axis=1), jnp.concatenate([dwg, dwv], axis=1), jnp.concatenate([dbg, dbv], axis=1))

    op.defvjp(fwd, bwd)
    return op(up, cw, cb)


def _peers():
    x, y, c = lax.axis_index("x"), lax.axis_index("y"), lax.axis_index("c")
    return (x, y, c), [(x, y, 1 - c), (1 - x, y, c), (x, 1 - y, c), (1 - x, 1 - y, c),
                       (1 - x, y, 1 - c), (x, 1 - y, 1 - c), (1 - x, 1 - y, 1 - c)]


def _index(dev):
    return 4 * dev[0] + 2 * dev[1] + dev[2]


def allgather_small(x, name):
    m, n = x.shape

    def body(x_ref, out_ref, send_sems, recv_sems, local_sem):
        me, peers = _peers()

        def rows(dev):
            return out_ref.at[pl.ds(pl.multiple_of(_index(dev) * m, 8), m), :]

        mine = pltpu.make_async_copy(x_ref, rows(me), local_sem)
        mine.start()
        sends = [pltpu.make_async_remote_copy(src_ref=x_ref, dst_ref=rows(me), send_sem=send_sems.at[k], recv_sem=recv_sems.at[k],
                                              device_id=p, device_id_type=MESH) for k, p in enumerate(peers)]
        for cp in sends:
            cp.start()
        for k, p in enumerate(peers):
            pltpu.make_async_remote_copy(src_ref=x_ref, dst_ref=rows(p), send_sem=send_sems.at[k], recv_sem=recv_sems.at[k],
                                         device_id=p, device_id_type=MESH).wait_recv()
        for cp in sends:
            cp.wait_send()
        mine.wait()

    return pl.pallas_call(
        body, name=name, out_shape=jax.ShapeDtypeStruct((N_DEV * m, n), x.dtype),
        in_specs=[pl.BlockSpec(memory_space=pltpu.VMEM)], out_specs=pl.BlockSpec(memory_space=pltpu.VMEM),
        scratch_shapes=[pltpu.SemaphoreType.DMA((7,)), pltpu.SemaphoreType.DMA((7,)), pltpu.SemaphoreType.DMA],
    )(x)


HBM_SPEC = pl.BlockSpec(memory_space=pltpu.HBM)
SEM_SPEC = pl.BlockSpec(memory_space=pltpu.SEMAPHORE)
SPLIT_PARAMS = dict(compiler_params=pltpu.CompilerParams(has_side_effects=pltpu.SideEffectType.DATAFLOW_SIDE_EFFECTING))
N_PEERS = N_DEV - 1


def _part(ref, kind, j, width):
    if kind == "col":
        return ref.at[:, pl.ds(pl.multiple_of(j * width, 128), width)]
    return ref.at[pl.ds(pl.multiple_of(j * width, 8), width), :]


def _in_hbm(a):
    return pltpu.with_memory_space_constraint(a, pltpu.HBM)


def gather_start(shards, lands, kinds, after):
    n = len(shards)
    widths = [s.shape[1] if k == "col" else s.shape[0] for s, k in zip(shards, kinds)]
    n_in = 2 * n + len(after)

    def body(*refs):
        srcs, lnds, send_sems, recv_sems, token = refs[:n], refs[n:2 * n], refs[n_in], refs[n_in + 1], refs[-1]
        me, peers = _peers()
        for a in range(n):
            for k, p in enumerate(peers):
                pltpu.make_async_remote_copy(
                    src_ref=srcs[a], dst_ref=_part(lnds[a], kinds[a], _index(me), widths[a]),
                    send_sem=send_sems.at[N_PEERS * a + k], recv_sem=recv_sems.at[N_PEERS * a + k],
                    device_id=p, device_id_type=MESH).start()
        token[...] = jnp.zeros_like(token)

    res = pl.pallas_call(
        body, name="gather_start",
        out_shape=(pltpu.SemaphoreType.DMA((N_PEERS * n,)), pltpu.SemaphoreType.DMA((N_PEERS * n,)),
                   *[pltpu.HBM(a.shape, a.dtype) for a in (*shards, *lands)], jax.ShapeDtypeStruct((8, 128), f32)),
        in_specs=[HBM_SPEC] * (2 * n) + [pl.BlockSpec(memory_space=pl.ANY)] * len(after),
        out_specs=(SEM_SPEC, SEM_SPEC, *[HBM_SPEC] * (2 * n), pl.BlockSpec(memory_space=pltpu.VMEM)),
        input_output_aliases={i: 2 + i for i in range(2 * n)}, **SPLIT_PARAMS,
    )(*[_in_hbm(a) for a in (*shards, *lands)], *after)
    return res[0], res[1], res[2:2 + n], res[2 + n:2 + 2 * n], res[-1]


def gather_wait(a, shard, land, kind, send_sems, recv_sems, after, name):
    width = shard.shape[1] if kind == "col" else shard.shape[0]

    def body(src_ref, land_ref, send_ref, recv_ref, after_ref, src_out, land_out):
        _, peers = _peers()
        for k, p in enumerate(peers):
            cp = pltpu.make_async_remote_copy(
                src_ref=src_ref, dst_ref=_part(land_ref, kind, _index(p), width),
                send_sem=send_ref.at[N_PEERS * a + k], recv_sem=recv_ref.at[N_PEERS * a + k], device_id=p, device_id_type=MESH)
            cp.wait_send()
            cp.wait_recv()

    return pl.pallas_call(
        body, name=name, out_shape=(pltpu.HBM(shard.shape, shard.dtype), pltpu.HBM(land.shape, land.dtype)),
        in_specs=(HBM_SPEC, HBM_SPEC, SEM_SPEC, SEM_SPEC, pl.BlockSpec(memory_space=pl.ANY)), out_specs=(HBM_SPEC, HBM_SPEC),
        input_output_aliases={0: 0, 1: 1}, **SPLIT_PARAMS,
    )(shard, land, send_sems, recv_sems, after)[1]


def scatter_start(full, slots, kind, name, carry):
    width = slots.shape[2] if kind == "col" else slots.shape[1]

    def body(full_ref, slots_ref, carry_ref, send_sems, recv_sems, full_out, slots_out, carry_out):
        me, peers = _peers()
        for k, p in enumerate(peers):
            pltpu.make_async_remote_copy(
                src_ref=_part(full_ref, kind, _index(p), width), dst_ref=slots_ref.at[_index(me)],
                send_sem=send_sems.at[k], recv_sem=recv_sems.at[k], device_id=p, device_id_type=MESH).start()

    return pl.pallas_call(
        body, name=name,
        out_shape=(pltpu.SemaphoreType.DMA((N_PEERS,)), pltpu.SemaphoreType.DMA((N_PEERS,)), pltpu.HBM(full.shape, full.dtype),
                   pltpu.HBM(slots.shape, slots.dtype), pltpu.HBM(carry.shape, carry.dtype)),
        in_specs=(HBM_SPEC, HBM_SPEC, HBM_SPEC), out_specs=(SEM_SPEC, SEM_SPEC, HBM_SPEC, HBM_SPEC, HBM_SPEC),
        input_output_aliases={0: 2, 1: 3, 2: 4}, **SPLIT_PARAMS,
    )(_in_hbm(full), _in_hbm(slots), _in_hbm(carry))


def scatter_wait(full, slots, kind, send_sems, recv_sems, after, name):
    width = slots.shape[2] if kind == "col" else slots.shape[1]

    def body(full_ref, slots_ref, send_ref, recv_ref, after_ref, full_out, slots_out):
        me, peers = _peers()
        for k, p in enumerate(peers):
            cp = pltpu.make_async_remote_copy(
                src_ref=_part(full_ref, kind, _index(p), width), dst_ref=slots_ref.at[_index(p)],
                send_sem=send_ref.at[k], recv_sem=recv_ref.at[k], device_id=p, device_id_type=MESH)
            cp.wait_send()
            cp.wait_recv()

    return pl.pallas_call(
        body, name=name, out_shape=(pltpu.HBM(full.shape, full.dtype), pltpu.HBM(slots.shape, slots.dtype)),
        in_specs=(HBM_SPEC, HBM_SPEC, SEM_SPEC, SEM_SPEC, pl.BlockSpec(memory_space=pl.ANY)), out_specs=(HBM_SPEC, HBM_SPEC),
        input_output_aliases={0: 0, 1: 1}, **SPLIT_PARAMS,
    )(full, slots, send_sems, recv_sems, after)[1]


def sum_slots(x, name):
    _, R, C = x.shape
    tr = _pick(R, (256, 128, 64, 32, 16, 8))

    def body(x_ref, o_ref):
        acc = x_ref[0].astype(f32)
        for d in range(1, N_DEV):
            acc = acc + x_ref[d].astype(f32)
        o_ref[...] = acc

    return pl.pallas_call(
        body, name=name, grid=(R // tr,), in_specs=[pl.BlockSpec((N_DEV, tr, C), lambda i: (0, i, 0))],
        out_specs=pl.BlockSpec((tr, C), lambda i: (i, 0)), out_shape=jax.ShapeDtypeStruct((R, C), f32),
        compiler_params=pltpu.CompilerParams(dimension_semantics=("parallel",)),
    )(x)


BIG = (("w_in", "col"), ("w_out", "row"), ("w_up", "col"), ("w_down", "row"))


class BigWeights:
    def __init__(self, given, me, after):
        self.me = me
        self.kinds = dict(BIG)
        depth = given["w_in"].shape[0]
        self.keys = [(l, n) for l in range(depth) for n, _ in BIG]
        kinds = [self.kinds[n] for _, n in self.keys]
        shards = [given[n][l].astype(bf16) for l, n in self.keys]
        lands = []
        for s, k in zip(shards, kinds):
            r, c = s.shape
            if k == "col":
                lands.append(lax.dynamic_update_slice(lax.empty((r, N_DEV * c), bf16), s, (0, me * c)))
            else:
                lands.append(lax.dynamic_update_slice(lax.empty((N_DEV * r, c), bf16), s, (me * r, 0)))
        self.send, self.recv, self.shards, self.lands, _ = gather_start(shards, lands, kinds, after)
        self.pending = {}

    def get(self, l, n, after):
        a = self.keys.index((l, n))
        return gather_wait(a, self.shards[a], self.lands[a], self.kinds[n], self.send, self.recv, after, "gather_wait_%s%d" % (n, l))

    def start_scatter(self, l, n, dw, carry):
        kind = self.kinds[n]
        R, C = dw.shape
        r, c = (R, C // N_DEV) if kind == "col" else (R // N_DEV, C)
        own = lax.dynamic_slice(dw, (0, self.me * c) if kind == "col" else (self.me * r, 0), (r, c))
        slots = lax.dynamic_update_slice(lax.empty((N_DEV, r, c), bf16), own[None], (self.me, 0, 0))
        send, recv, full, slots, carry = scatter_start(dw, slots, kind, "scatter_start_%s%d" % (n, l), carry)
        self.pending[(l, n)] = (full, slots, send, recv)
        return carry

    def contributions(self, l, n, after):
        full, slots, send, recv = self.pending[(l, n)]
        return scatter_wait(full, slots, self.kinds[n], send, recv, after, "scatter_wait_%s%d" % (n, l))


def allreduce_small(vals, name):
    flat = jnp.concatenate([v.reshape(-1) for v in vals])
    n = flat.shape[0]
    cols = 1024
    m = -(-n // (cols * 8)) * 8
    packed = jnp.pad(flat, (0, m * cols - n)).reshape(m, cols)
    total = sum_slots(allgather_small(packed, name).reshape(N_DEV, m, cols), name + "_sum").reshape(-1)
    out, off = [], 0
    for v in vals:
        out.append(total[off:off + v.size].reshape(v.shape))
        off += v.size
    return out


def ada_mod(c_all, c_ctx, w_ada, b_ada):
    L, D, S = w_ada.shape
    me = _my_index()

    def stacked(c_ctx):
        return jnp.concatenate([c_all, jnp.broadcast_to(c_ctx, (N_DEV, D))], axis=0)

    ts = _pick(S, (512, 384, 256, 128, 64))
    w_spec = pl.BlockSpec((None, D, ts), lambda l, j: (l, 0, j))
    c_spec = pl.BlockSpec((16, D), lambda l, j: (0, 0))
    p_spec = pl.BlockSpec((None, 16, ts), lambda l, j: (l, 0, j))

    def run_fwd(cin, w_ada):
        def body(c_ref, w_ref, o_ref):
            o_ref[...] = jnp.dot(_silu(c_ref[...]).astype(bf16), w_ref[...].astype(bf16), preferred_element_type=f32)

        return pl.pallas_call(
            body, name="ada_fwd", grid=(L, S // ts), in_specs=[c_spec, w_spec], out_specs=p_spec,
            out_shape=jax.ShapeDtypeStruct((L, 16, S), f32),
            compiler_params=pltpu.CompilerParams(dimension_semantics=("parallel", "parallel")),
        )(cin, w_ada)

    def run_bwd(cin, w_ada, dm):
        def body(c_ref, w_ref, dm_ref, gw_ref, dc_ref):
            first = (pl.program_id(0) == 0) & (pl.program_id(1) == 0)
            cv = c_ref[...]
            sg = jax.nn.sigmoid(cv)
            dmv = dm_ref[...].astype(bf16)
            gw_ref[...] = lax.dot_general((cv * sg).astype(bf16), dmv, (((0,), (0,)), ((), ())), preferred_element_type=f32)
            ds = lax.dot_general(dmv, w_ref[...].astype(bf16), (((1,), (1,)), ((), ())), preferred_element_type=f32)
            dc = ds * sg * (1.0 + cv * (1.0 - sg))

            @pl.when(first)
            def _():
                dc_ref[...] = dc

            @pl.when(jnp.logical_not(first))
            def _():
                dc_ref[...] += dc

        return pl.pallas_call(
            body, name="ada_bwd", grid=(L, S // ts), in_specs=[c_spec, w_spec, p_spec], out_specs=[w_spec, c_spec],
            out_shape=[jax.ShapeDtypeStruct((L, D, S), f32), jax.ShapeDtypeStruct((16, D), f32)],
            compiler_params=pltpu.CompilerParams(dimension_semantics=("arbitrary", "arbitrary")),
        )(cin, w_ada, dm)

    def bias_grad(dm_full):
        def body(x_ref, o_ref):
            o_ref[...] = jnp.sum(x_ref[...], axis=0, keepdims=True)

        return pl.pallas_call(
            body, name="ada_bias_grad", grid=(L,), in_specs=[pl.BlockSpec((None, 16, 6 * D), lambda l: (l, 0, 0))],
            out_specs=pl.BlockSpec((None, 1, 6 * D), lambda l: (l, 0, 0)), out_shape=jax.ShapeDtypeStruct((L, 1, 6 * D), f32),
        )(dm_full).reshape(L, 6 * D)

    @jax.custom_vjp
    def op(c_ctx, w_ada, b_ada):
        prod = run_fwd(stacked(c_ctx), w_ada)
        allp = allgather_small(prod.reshape(L * 16, S), "ada_gather").reshape(N_DEV, L, 16, S)
        allp = allp.transpose(1, 2, 0, 3).reshape(L, 16, N_DEV * S)
        mine = lax.dynamic_index_in_dim(allp, me, axis=1, keepdims=False) + b_ada
        ctx = allp[:, N_DEV] + b_ada
        return jnp.stack([mine, ctx], axis=1).reshape(L, 2, 6, D)

    def fwd(c_ctx, w_ada, b_ada):
        return op(c_ctx, w_ada, b_ada), (c_ctx, w_ada)

    def bwd(res, dmod):
        c_ctx, w_ada = res
        dm = dmod.reshape(L * 2, 6 * D)
        gathered = allgather_small(jnp.pad(dm, ((0, (-2 * L) % 8), (0, 0))), "ada_grad_gather")
        gathered = gathered.reshape(N_DEV, -1, 6 * D)[:, :2 * L].reshape(N_DEV, L, 2, 6 * D)
        dm_full = gathered.transpose(1, 2, 0, 3).reshape(L, 16, 6 * D)
        dm_mine = lax.dynamic_slice_in_dim(dm_full, me * S, S, axis=2)
        gw, dc = run_bwd(stacked(c_ctx), w_ada, dm_mine)
        d_cctx = jnp.sum(dc[N_DEV:], axis=0, keepdims=True)
        return d_cctx, gw, bias_grad(dm_full)

    op.defvjp(fwd, bwd)
    return op(c_ctx, w_ada, b_ada)


def _adamw_math(w, gv, m, v):
    c1 = 1.0 / (1.0 - ADAM_B1 ** ADAM_STEP)
    c2 = 1.0 / (1.0 - ADAM_B2 ** ADAM_STEP)
    mn = ADAM_B1 * m + (1.0 - ADAM_B1) * gv
    vn = ADAM_B2 * v + (1.0 - ADAM_B2) * gv * gv
    return -ADAM_LR * ((mn * c1) / (jnp.sqrt(vn * c2) + ADAM_EPS) + ADAM_WD * w), mn, vn


def adamw_layer(l, slots, w, m, v, outs, name):
    depth, r, c = w.shape
    tr = _pick(r, (128, 64, 32, 16, 8))
    if outs is None:
        outs = tuple(lax.empty((depth, r, c), f32) for _ in range(4))

    def body(s_ref, w_ref, m_ref, v_ref, *rest):
        g_ref, d_ref, mo_ref, vo_ref = rest[4:]
        gv = s_ref[0].astype(f32)
        for d in range(1, N_DEV):
            gv = gv + s_ref[d].astype(f32)
        g_ref[...] = gv
        d_ref[...], mo_ref[...], vo_ref[...] = _adamw_math(w_ref[...], gv, m_ref[...], v_ref[...])

    lay = pl.BlockSpec((None, tr, c), lambda i: (l, i, 0))
    return pl.pallas_call(
        body, name=name, grid=(r // tr,),
        in_specs=[pl.BlockSpec((N_DEV, tr, c), lambda i: (0, i, 0)), lay, lay, lay] + [pl.BlockSpec(memory_space=pl.ANY)] * 4,
        out_specs=[lay] * 4, out_shape=[jax.ShapeDtypeStruct((depth, r, c), f32)] * 4,
        input_output_aliases={4 + k: k for k in range(4)},
        compiler_params=pltpu.CompilerParams(dimension_semantics=("arbitrary",)),
    )(slots, w, m, v, *outs)


def adamw(w, g, m, v, name):
    shape = w.shape
    C = shape[-1]
    R = w.size // C
    tr = _pick(R, (256, 128, 64, 32, 16, 8)) if R * C * 4 > (1 << 20) else R

    def body(w_ref, g_ref, m_ref, v_ref, d_ref, mo_ref, vo_ref):
        d_ref[...], mo_ref[...], vo_ref[...] = _adamw_math(w_ref[...], g_ref[...], m_ref[...], v_ref[...])

    spec = pl.BlockSpec((tr, C), lambda i: (i, 0))
    res = pl.pallas_call(
        body, name=name, grid=(R // tr,), in_specs=[spec] * 4, out_specs=[spec] * 3,
        out_shape=[jax.ShapeDtypeStruct((R, C), f32)] * 3,
        compiler_params=pltpu.CompilerParams(dimension_semantics=("parallel",)),
    )(*[a.reshape(R, C) for a in (w, g, m, v)])
    return tuple(r.reshape(shape) for r in res)


def _rope_tables(T, L):
    rows = T // GRID_W
    row = jnp.repeat(jnp.arange(rows, dtype=f32), GRID_W)
    col = jnp.tile(jnp.arange(GRID_W, dtype=f32), rows)
    n_freq = HEAD // 4
    inv = ROPE_THETA ** (-jnp.arange(n_freq, dtype=f32) / n_freq)
    ang = jnp.concatenate([row[:, None] * inv, col[:, None] * inv], axis=-1)
    cos = jnp.repeat(jnp.cos(ang), 2, axis=-1)
    sin = jnp.repeat(jnp.sin(ang), 2, axis=-1) * jnp.tile(jnp.array([-1.0, 1.0], f32), HEAD // 2)
    return (jnp.concatenate([cos, jnp.ones((L, HEAD), f32)]), jnp.concatenate([sin, jnp.zeros((L, HEAD), f32)]))


def _sel(mod, l, idx):
    return jnp.stack([mod[l, :, i] for i in idx], axis=1)


def _row(a, l):
    return a[l][None, :]


def _first_segment(T, ctx):
    def seg(x, mod, p):
        xs = jnp.concatenate([x, ctx], axis=0)
        (h,) = rowwise("modnorm", f_modnorm, [xs], sels=[_sel(mod, 0, (0, 1))], pars=[_row(p["norm1_g"], 0)],
                       outs=[(x.shape[1], bf16)], n_lat=T)
        return xs, h

    return seg


def _mixer_segment(l, T, cosf, sinf):
    row = _row

    def seg(xs, h, mod, lbs, p, w_in, w_out):
        D = xs.shape[1]
        aq, ak, av, hq, hff, hfb, hi, hgt, su, sv = mm(h, w_in, "w_in%d" % l, split=IN_SIZES)
        q, k, v = rowwise("qkprep%d" % l, f_qkprep, [aq, ak, av], consts=[cosf, sinf],
                          pars=[row(p["q_norm_g"], l), row(p["k_norm_g"], l)],
                          outs=[(aq.shape[1], bf16), (ak.shape[1], bf16), (av.shape[1], bf16)], n_lat=T)
        attn = attention(q, k, v, T, "attn%d" % l)
        o_f = hgrn(hq, hff, hi, lbs[0:1], T, False, "hgrn_f%d" % l)
        o_b = hgrn(hq, hfb, hi, lbs[1:2], T, True, "hgrn_b%d" % l)
        (hg,) = rowwise("hgout%d" % l, f_hgout, [o_f, o_b, hgt], pars=[row(p["hg_norm_g"], l)], outs=[(hgt.shape[1], bf16)], n_lat=T)
        (sg,) = rowwise("sgate%d" % l, f_sgate, [su, sv],
                        pars=[row(p["sg_norm_g"], l), p["sg_w"][l]] + [p["sg_b"][l, gi][:, None] for gi in range(SG_GROUPS)],
                        outs=[(su.shape[1], bf16)], n_lat=T)
        mix = jnp.concatenate([attn, hg, sg], axis=1)
        y = mm(mix, w_out, "w_out%d" % l)
        return rowwise("resid_a%d" % l, f_resid_modnorm, [xs, y], sels=[_sel(mod, l, (2, 3, 4))], pars=[row(p["norm2_g"], l)],
                       outs=[(D, f32), (D, bf16)], n_lat=T)

    return seg


def _ffn_segment(l, depth, T, tgt):
    row = _row

    def seg(xs, h2, mod, p, w_up, w_down):
        D = xs.shape[1]
        up = mm(h2, w_up, "w_up%d" % l, out_dtype=bf16)
        act = convact(up, p["conv_w"][l], row(p["conv_b"], l), T, "convact%d" % l)
        z = mm(act, w_down, "w_down%d" % l)
        if l + 1 < depth:
            return rowwise("resid_b%d" % l, f_resid_modnorm, [xs, z],
                           sels=[jnp.concatenate([_sel(mod, l, (5,)), _sel(mod, l + 1, (0, 1))], axis=1)],
                           pars=[row(p["norm1_g"], l + 1)], outs=[(D, f32), (D, bf16)], n_lat=T)
        return rowwise("resid_final", f_resid_final, [xs[:T], z[:T]], consts=[tgt], sels=[_sel(mod, l, (5,))[0:1]],
                       pars=[p["final_norm_g"][None, :]], outs=[(1, f32)])

    return seg


SEGMENT_PARAMS = ("norm1_g", "q_norm_g", "k_norm_g", "hg_norm_g", "sg_norm_g", "sg_w", "sg_b", "norm2_g", "conv_w", "conv_b",
                  "final_norm_g")


def _prologue(p, c_all):
    depth = p["norm1_g"].shape[0]
    mod, vjp_mod = jax.vjp(lambda cc, wa, ba: ada_mod(c_all, cc, wa, ba), p["c_ctx"], p["w_ada"], p["b_ada"])
    lbs, vjp_lbs = jax.vjp(lambda hg: lower_bounds([hg[:, l] for l in range(depth)]), p["hg_lower_bounds"])
    return mod, vjp_mod, lbs, vjp_lbs


def _loss_and_grads(p, prologue, big, x, ctx, tgt, cosf, sinf):
    T = x.shape[0]
    depth = p["norm1_g"].shape[0]
    add = lambda a, b: jax.tree.map(jnp.add, a, b)
    small = {n: p[n] for n in SEGMENT_PARAMS}
    mod, vjp_mod, lbs, vjp_lbs = prologue
    (xs, h), vjp_first = jax.vjp(_first_segment(T, ctx), x, mod, small)
    vjps = []
    for l in range(depth):
        w_in, w_out = big.get(l, "w_in", h), big.get(l, "w_out", h)
        (xs, h2), vj = jax.vjp(_mixer_segment(l, T, cosf, sinf), xs, h, mod, lbs[l], small, w_in, w_out)
        vjps.append(vj)
        w_up, w_down = big.get(l, "w_up", h2), big.get(l, "w_down", h2)
        out, vj = jax.vjp(_ffn_segment(l, depth, T, tgt), xs, h2, mod, small, w_up, w_down)
        vjps.append(vj)
        if l + 1 < depth:
            xs, h = out
    (rowloss,) = out
    loss = 0.5 * jnp.sum(rowloss)

    ct = (jnp.full(rowloss.shape, 0.5, f32),)
    d_mod, d_small, d_lbs = jnp.zeros_like(mod), jax.tree.map(jnp.zeros_like, small), [None] * depth
    for l in range(depth - 1, -1, -1):
        dxs, dh2, dm, ds, d_up, d_down = vjps[2 * l + 1](ct)
        dh2 = big.start_scatter(l, "w_down", d_down, big.start_scatter(l, "w_up", d_up, dh2))
        d_mod, d_small = d_mod + dm, add(d_small, ds)
        dxs, dh, dm, d_lbs[l], ds, d_in, d_out = vjps[2 * l]((dxs, dh2))
        dh = big.start_scatter(l, "w_out", d_out, big.start_scatter(l, "w_in", d_in, dh))
        d_mod, d_small = d_mod + dm, add(d_small, ds)
        ct = (dxs, dh)
    dx, dm, ds = vjp_first(ct)
    d_cc, d_wada, d_bada = vjp_mod(d_mod + dm)
    (d_hg,) = vjp_lbs(tuple(d_lbs))
    grads = dict(add(d_small, ds), c_ctx=d_cc, w_ada=d_wada, b_ada=d_bada, hg_lower_bounds=d_hg)
    return loss, grads, dx


def kernel(x, c, ctx, c_ctx, w_ada, b_ada, norm1_g, w_in, q_norm_g, k_norm_g, hg_lower_bounds, hg_norm_g, sg_norm_g, sg_w, sg_b, w_out, norm2_g, w_up, conv_w, conv_b, w_down, final_norm_g, loss_target, m_c_ctx, m_w_ada, m_b_ada, m_norm1_g, m_w_in, m_q_norm_g, m_k_norm_g, m_hg_lower_bounds, m_hg_norm_g, m_sg_norm_g, m_sg_w, m_sg_b, m_w_out, m_norm2_g, m_w_up, m_conv_w, m_conv_b, m_w_down, m_final_norm_g, v_c_ctx, v_w_ada, v_b_ada, v_norm1_g, v_w_in, v_q_norm_g, v_k_norm_g, v_hg_lower_bounds, v_hg_norm_g, v_sg_norm_g, v_sg_w, v_sg_b, v_w_out, v_norm2_g, v_w_up, v_conv_w, v_conv_b, v_w_down, v_final_norm_g):
    given = dict(c_ctx=c_ctx, w_ada=w_ada, b_ada=b_ada, norm1_g=norm1_g, w_in=w_in, q_norm_g=q_norm_g, k_norm_g=k_norm_g,
                 hg_lower_bounds=hg_lower_bounds, hg_norm_g=hg_norm_g, sg_norm_g=sg_norm_g, sg_w=sg_w, sg_b=sg_b, w_out=w_out,
                 norm2_g=norm2_g, w_up=w_up, conv_w=conv_w, conv_b=conv_b, w_down=w_down, final_norm_g=final_norm_g)
    moments_m = dict(c_ctx=m_c_ctx, w_ada=m_w_ada, b_ada=m_b_ada, norm1_g=m_norm1_g, w_in=m_w_in, q_norm_g=m_q_norm_g,
                     k_norm_g=m_k_norm_g, hg_lower_bounds=m_hg_lower_bounds, hg_norm_g=m_hg_norm_g, sg_norm_g=m_sg_norm_g,
                     sg_w=m_sg_w, sg_b=m_sg_b, w_out=m_w_out, norm2_g=m_norm2_g, w_up=m_w_up, conv_w=m_conv_w, conv_b=m_conv_b,
                     w_down=m_w_down, final_norm_g=m_final_norm_g)
    moments_v = dict(c_ctx=v_c_ctx, w_ada=v_w_ada, b_ada=v_b_ada, norm1_g=v_norm1_g, w_in=v_w_in, q_norm_g=v_q_norm_g,
                     k_norm_g=v_k_norm_g, hg_lower_bounds=v_hg_lower_bounds, hg_norm_g=v_hg_norm_g, sg_norm_g=v_sg_norm_g,
                     sg_w=v_sg_w, sg_b=v_sg_b, w_out=v_w_out, norm2_g=v_norm2_g, w_up=v_w_up, conv_w=v_conv_w, conv_b=v_conv_b,
                     w_down=v_w_down, final_norm_g=v_final_norm_g)
    T, D = x.shape[1], x.shape[2]
    L = ctx.shape[1]
    me = _my_index()
    axes = ("x", "y", "c")

    c_all = allgather_small(jnp.pad(c, ((0, 7), (0, 0))), "gather_c").reshape(N_DEV, 8, D)[:, 0]
    depth, hw = hg_lower_bounds.shape[1], hg_lower_bounds.shape[2]
    cw = conv_w.shape[2]
    small = jnp.concatenate([jnp.pad(hg_lower_bounds.reshape(2 * depth, hw), ((0, 0), (0, cw - hw))), conv_w.reshape(3 * depth, cw)], axis=0)
    rows_small = small.shape[0]
    small = allgather_small(jnp.pad(small, ((0, (-rows_small) % 8), (0, 0))), "gather_small").reshape(N_DEV, -1, cw)
    hg_full = small[:, :2 * depth, :hw].reshape(N_DEV, 2, depth, hw).transpose(1, 2, 0, 3).reshape(2, depth, N_DEV * hw)
    cw_full = small[:, 2 * depth:2 * depth + 3 * depth].reshape(N_DEV, depth, 3, cw).transpose(1, 2, 0, 3).reshape(depth, 3, N_DEV * cw)

    p = {n: a for n, a in given.items() if n not in dict(BIG)}
    p.update(hg_lower_bounds=hg_full, conv_w=cw_full, c_ctx=c_ctx[None, :])
    prologue = _prologue(p, c_all)
    big = BigWeights(given, me, after=(prologue[0], cw_full))
    cosf, sinf = _rope_tables(T, L)
    loss, gp, gx = _loss_and_grads(p, prologue, big, x[0], ctx[0], loss_target[0], cosf, sinf)
    loss = lax.psum(loss, axes)

    grads, delta, new_m, new_v = dict(gp), {}, {}, {}
    partial = ['c_ctx', 'norm1_g', 'q_norm_g', 'k_norm_g', 'hg_lower_bounds', 'hg_norm_g', 'sg_norm_g', 'sg_w', 'sg_b',
               'norm2_g', 'conv_w', 'conv_b', 'final_norm_g']
    grads.update(zip(partial, allreduce_small([gp[n] for n in partial], "reduce_small")))
    grads['c_ctx'] = grads['c_ctx'][0]
    grads['hg_lower_bounds'] = lax.dynamic_slice_in_dim(grads['hg_lower_bounds'], me * hw, hw, axis=2)
    grads['conv_w'] = lax.dynamic_slice_in_dim(grads['conv_w'], me * cw, cw, axis=2)
    last = gx
    for n in [w for w in WEIGHTS if w not in dict(BIG)]:
        delta[n], new_m[n], new_v[n] = adamw(given[n], grads[n], moments_m[n], moments_v[n], "adamw_" + n)
        last = delta[n]

    big_outs = {n: None for n, _ in BIG}
    for l in range(depth - 1, -1, -1):
        for n in ("w_down", "w_up", "w_out", "w_in"):
            big_outs[n] = adamw_layer(l, big.contributions(l, n, last), given[n], moments_m[n], moments_v[n], big_outs[n],
                                      "adamw_%s%d" % (n, l))
            last = big_outs[n][1]
    for n, _ in BIG:
        grads[n], delta[n], new_m[n], new_v[n] = big_outs[n]
    return (loss, gx[None], *[grads[n] for n in WEIGHTS], *[delta[n] for n in WEIGHTS],
            *[new_m[n] for n in WEIGHTS], *[new_v[n] for n in WEIGHTS])
```

```python
import functools

import jax
import jax.numpy as jnp
from jax import lax
from jax.experimental import pallas as pl
from jax.experimental.pallas import tpu as pltpu

f32 = jnp.float32
bf16 = jnp.bfloat16
HI = lax.Precision.HIGHEST
MESH = pl.DeviceIdType.MESH

EPS = 1e-6
F_MIN = 1e-30
GRID_W = 64
ROPE_THETA = 10000.0
HEAD = 128
ATTN_HEADS, ATTN_KV = 8, 2
ATTN_GROUP = ATTN_HEADS // ATTN_KV
HG_HEADS = 4
SG_GROUPS = 4
SG_CHUNK = 128
HG_CHUNK = 16
HG_GROUP = 16
IN_SIZES = (1024, 256, 256, 512, 512, 512, 512, 512, 512, 512)
N_DEV = 8
ROW_BLOCK = 256
MAX_TK = 2816
ADAM_LR, ADAM_B1, ADAM_B2, ADAM_EPS, ADAM_WD, ADAM_STEP = 0.001, 0.9, 0.999, 1e-08, 0.01, 10

WEIGHTS = ['c_ctx', 'w_ada', 'b_ada', 'norm1_g', 'w_in', 'q_norm_g', 'k_norm_g', 'hg_lower_bounds', 'hg_norm_g',
           'sg_norm_g', 'sg_w', 'sg_b', 'w_out', 'norm2_g', 'w_up', 'conv_w', 'conv_b', 'w_down', 'final_norm_g']


def _pick(dim, cands):
    for t in cands:
        if dim % t == 0:
            return t
    return dim


def _my_index():
    return 4 * lax.axis_index("x") + 2 * lax.axis_index("y") + lax.axis_index("c")


def _mm_call(a, b, mode, out_dtype, name):
    if mode == "nn":
        (M, K), N = a.shape, b.shape[1]
    elif mode == "nt":
        (M, K), N = a.shape, b.shape[0]
    else:
        (K, M), N = a.shape, b.shape[1]
    tm = _pick(M, (1088, 1024, 512, 256, 128))
    tn = _pick(N, (1024, 512, 256, 128))
    tk = K if K <= MAX_TK else _pick(K, (2816, 2560, 2176, 2048, 1408, 1088, 1024, 512, 256, 128))
    nk = K // tk
    dims = {"nn": (((1,), (0,)), ((), ())), "nt": (((1,), (1,)), ((), ())), "tn": (((0,), (0,)), ((), ()))}[mode]

    def body(a_ref, b_ref, o_ref, *acc):
        prod = lax.dot_general(a_ref[...].astype(bf16), b_ref[...].astype(bf16), dims, preferred_element_type=f32)
        if nk == 1:
            o_ref[...] = prod.astype(o_ref.dtype)
            return
        k = pl.program_id(2)

        @pl.when(k == 0)
        def _():
            acc[0][...] = prod

        @pl.when((k > 0) & (k < nk - 1))
        def _():
            acc[0][...] += prod

        @pl.when(k == nk - 1)
        def _():
            o_ref[...] = (acc[0][...] + prod).astype(o_ref.dtype)

    a_spec = pl.BlockSpec((tk, tm), lambda i, j, k: (k, i)) if mode == "tn" else pl.BlockSpec((tm, tk), lambda i, j, k: (i, k))
    b_spec = pl.BlockSpec((tn, tk), lambda i, j, k: (j, k)) if mode == "nt" else pl.BlockSpec((tk, tn), lambda i, j, k: (k, j))
    return pl.pallas_call(
        body, name=name, grid=(M // tm, N // tn, nk),
        in_specs=[a_spec, b_spec], out_specs=pl.BlockSpec((tm, tn), lambda i, j, k: (i, j)),
        out_shape=jax.ShapeDtypeStruct((M, N), out_dtype),
        scratch_shapes=[pltpu.VMEM((tm, tn), f32)] if nk > 1 else [],
        compiler_params=pltpu.CompilerParams(dimension_semantics=("parallel", "parallel", "arbitrary")),
    )(a, b)


def mm(a, w, name, out_dtype=f32, split=None):
    def parts(y):
        if split is None:
            return y
        offs = [sum(split[:i]) for i in range(len(split))]
        return tuple(y[:, o:o + s] for o, s in zip(offs, split))

    @jax.custom_vjp
    def op(a, w):
        return parts(_mm_call(a, w, "nn", out_dtype, name + "_fwd"))

    def fwd(a, w):
        return op(a, w), (a, w)

    def bwd(res, dy):
        a, w = res
        dy = dy.astype(bf16) if split is None else jnp.concatenate([d.astype(bf16) for d in dy], axis=1)
        return _mm_call(dy, w, "nt", a.dtype, name + "_bwd_a"), _mm_call(a, dy, "tn", w.dtype, name + "_bwd_w")

    op.defvjp(fwd, bwd)
    return op(a, w)


def _rowwise_specs(rows, consts, sels, pars, tb, nlb):
    specs = [pl.BlockSpec((tb, a.shape[1]), lambda i: (i, 0)) for a in (*rows, *consts)]
    specs += [pl.BlockSpec((None,) + a.shape[1:], lambda i: (jnp.where(i >= nlb, 1, 0), 0, 0)) for a in sels]
    specs += [pl.BlockSpec(a.shape, functools.partial(lambda i, n: (0,) * n, n=a.ndim)) for a in pars]
    return specs


def rowwise(name, f, rows, consts=(), sels=(), pars=(), outs=(), n_lat=None, tb=ROW_BLOCK):
    rows, consts, sels, pars = tuple(rows), tuple(consts), tuple(sels), tuple(pars)
    R = rows[0].shape[0]
    nb = R // tb
    nlb = nb if n_lat is None else n_lat // tb
    n_in = len(rows) + len(consts) + len(sels) + len(pars)
    n_out = len(outs)
    out_dtypes = [d for _, d in outs]
    out_specs = [pl.BlockSpec((tb, w), lambda i: (i, 0)) for w, _ in outs]
    out_shape = [jax.ShapeDtypeStruct((R, w), d) for w, d in outs]
    sem = pltpu.CompilerParams(dimension_semantics=("arbitrary",))

    def run_fwd(rows, consts, sels, pars):
        def body(*refs):
            res = f(*[r[...] for r in refs[:n_in]])
            for o_ref, r in zip(refs[n_in:], res):
                o_ref[...] = r.astype(o_ref.dtype)

        return tuple(pl.pallas_call(
            body, name=name + "_fwd", grid=(nb,), in_specs=_rowwise_specs(rows, consts, sels, pars, tb, nlb),
            out_specs=out_specs, out_shape=out_shape, compiler_params=sem,
        )(*rows, *consts, *sels, *pars))

    def run_bwd(rows, consts, sels, pars, cts):
        nr, nc, ns, npar = len(rows), len(consts), len(sels), len(pars)

        def body(*refs):
            i = pl.program_id(0)
            ins = [r[...] for r in refs[:n_in]]
            ct = tuple(r[...] for r in refs[n_in:n_in + n_out])
            o_refs = refs[n_in + n_out:]
            cvals = ins[nr:nr + nc]

            def g(*d):
                res = f(*d[:nr], *cvals, *d[nr:])
                return tuple(r.astype(t) for r, t in zip(res, out_dtypes))

            _, vjp = jax.vjp(g, *ins[:nr], *ins[nr + nc:])
            grads = vjp(ct)
            for k in range(nr):
                o_refs[k][...] = grads[k].astype(o_refs[k].dtype)
            for k in range(nr, nr + ns + npar):
                first = (i == 0) | (i == nlb) if k < nr + ns else (i == 0)
                gk = grads[k].astype(f32)

                @pl.when(first)
                def _(k=k, gk=gk):
                    o_refs[k][...] = gk

                @pl.when(jnp.logical_not(first))
                def _(k=k, gk=gk):
                    o_refs[k][...] += gk

        in_specs = _rowwise_specs(rows, consts, sels, pars, tb, nlb) + out_specs
        o_specs = [pl.BlockSpec((tb, a.shape[1]), lambda i: (i, 0)) for a in rows]
        o_specs += [pl.BlockSpec((None,) + a.shape[1:], lambda i: (jnp.where(i >= nlb, 1, 0), 0, 0)) for a in sels]
        o_specs += [pl.BlockSpec(a.shape, functools.partial(lambda i, n: (0,) * n, n=a.ndim)) for a in pars]
        o_shape = [jax.ShapeDtypeStruct(a.shape, a.dtype) for a in rows]
        o_shape += [jax.ShapeDtypeStruct(a.shape, f32) for a in (*sels, *pars)]
        res = pl.pallas_call(
            body, name=name + "_bwd", grid=(nb,), in_specs=in_specs, out_specs=o_specs, out_shape=o_shape,
            compiler_params=sem,
        )(*rows, *consts, *sels, *pars, *cts)
        return tuple(res[:nr]), tuple(res[nr:nr + ns]), tuple(res[nr + ns:])

    @jax.custom_vjp
    def op(rows, consts, sels, pars):
        return run_fwd(rows, consts, sels, pars)

    def fwd(rows, consts, sels, pars):
        return op(rows, consts, sels, pars), (rows, consts, sels, pars)

    def bwd(res, cts):
        rows, consts, sels, pars = res
        drows, dsels, dpars = run_bwd(rows, consts, sels, pars, tuple(cts))
        return drows, tuple(jnp.zeros_like(c) for c in consts), dsels, dpars

    op.defvjp(fwd, bwd)
    return op(rows, consts, sels, pars)


def _rms(x, g):
    return x * lax.rsqrt(jnp.mean(x * x, axis=-1, keepdims=True) + EPS) * g


def _silu(x):
    return x * jax.nn.sigmoid(x)


def f_modnorm(x, mods, g):
    return (_rms(x, g) * (1.0 + mods[1:2]) + mods[0:1],)


def f_resid_modnorm(x, y, mods, g):
    xn = x + mods[0:1] * y
    return xn, _rms(xn, g) * (1.0 + mods[2:3]) + mods[1:2]


def f_resid_final(x, y, tgt, mods, g):
    xn = x + mods[0:1] * y
    err = _rms(xn, g) - tgt
    return (jnp.mean(err * err, axis=-1, keepdims=True),)


def f_qkprep(aq, ak, av, cosf, sinf, qg, kg):
    r = lax.broadcasted_iota(jnp.int32, (HEAD, HEAD), 0)
    c = lax.broadcasted_iota(jnp.int32, (HEAD, HEAD), 1)
    swap = jnp.where((r ^ 1) == c, 1.0, 0.0).astype(f32)

    def head(xh, g):
        y = _rms(xh, g)
        ys = jnp.dot(y, swap, precision=HI, preferred_element_type=f32)
        return y * cosf + ys * sinf

    q = jnp.concatenate([head(aq[:, h * HEAD:(h + 1) * HEAD], qg) for h in range(ATTN_HEADS)], axis=1)
    k = jnp.concatenate([head(ak[:, h * HEAD:(h + 1) * HEAD], kg) for h in range(ATTN_KV)], axis=1)
    return q, k, av


def f_hgout(of, ob, gt, g):
    o = of + ob
    y = jnp.concatenate([_rms(o[:, h * HEAD:(h + 1) * HEAD], g) for h in range(HG_HEADS)], axis=1)
    return (y * _silu(gt),)


def f_sgate(u, v, g, w, b0, b1, b2, b3):
    u = jax.nn.gelu(u)
    v = jax.nn.gelu(v)
    bs = (b0, b1, b2, b3)
    cols = []
    for gi in range(SG_GROUPS):
        sl = slice(gi * HEAD, (gi + 1) * HEAD)
        vg = _rms(v[:, sl], g[:, sl])
        parts = []
        for n in range(v.shape[0] // SG_CHUNK):
            vc = vg[n * SG_CHUNK:(n + 1) * SG_CHUNK]
            parts.append(jnp.dot(w[gi].astype(bf16), vc.astype(bf16), preferred_element_type=f32) + bs[gi])
        cols.append(jnp.concatenate(parts, axis=0))
    return (u * jnp.concatenate(cols, axis=1),)


def attention(q, k, v, T, name, tq=ROW_BLOCK):
    NT = q.shape[0]
    nqb, nlb = NT // tq, T // tq
    scale = HEAD ** -0.5
    q_spec = pl.BlockSpec((tq, HEAD), lambda kv, g, i: (i, kv * ATTN_GROUP + g))
    kv_spec = pl.BlockSpec((NT, HEAD), lambda kv, g, i: (0, kv))
    lse_spec = pl.BlockSpec((None, tq, 1), lambda kv, g, i: (kv * ATTN_GROUP + g, i, 0))
    grid = (ATTN_KV, ATTN_GROUP, nqb)
    nt_dims = (((1,), (1,)), ((), ()))
    tn_dims = (((0,), (0,)), ((), ()))

    def on_keys(i, fn):
        @pl.when(i < nlb)
        def _():
            fn(pl.ds(0, NT))

        @pl.when(i >= nlb)
        def _():
            fn(pl.ds(T, NT - T))

    def run_fwd(q, k, v):
        def body(q_ref, k_ref, v_ref, o_ref, lse_ref):
            def run(rows):
                s = lax.dot_general(q_ref[...], k_ref[rows, :], nt_dims, preferred_element_type=f32)
                m = jnp.max(s, axis=-1, keepdims=True) * scale
                p = jnp.exp(s * scale - m)
                l = jnp.sum(p, axis=-1, keepdims=True)
                o = jnp.dot(p.astype(bf16), v_ref[rows, :], preferred_element_type=f32) / l
                o_ref[...] = o.astype(o_ref.dtype)
                lse_ref[...] = m + jnp.log(l)

            on_keys(pl.program_id(2), run)

        return pl.pallas_call(
            body, name=name + "_fwd", grid=grid, in_specs=[q_spec, kv_spec, kv_spec], out_specs=[q_spec, lse_spec],
            out_shape=[jax.ShapeDtypeStruct(q.shape, bf16), jax.ShapeDtypeStruct((ATTN_HEADS, NT, 1), f32)],
            compiler_params=pltpu.CompilerParams(dimension_semantics=("parallel", "parallel", "arbitrary")),
        )(q, k, v)

    def run_bwd(q, k, v, lse, do):
        def body(q_ref, k_ref, v_ref, lse_ref, do_ref, dq_ref, dk_ref, dv_ref):
            g, i = pl.program_id(1), pl.program_id(2)

            @pl.when((g == 0) & (i == 0))
            def _():
                dk_ref[...] = jnp.zeros_like(dk_ref)
                dv_ref[...] = jnp.zeros_like(dv_ref)

            def run(rows):
                qb, kb, vb, dob = q_ref[...], k_ref[rows, :], v_ref[rows, :], do_ref[...]
                s = lax.dot_general(qb, kb, nt_dims, preferred_element_type=f32)
                p = jnp.exp(s * scale - lse_ref[...])
                dp = lax.dot_general(dob, vb, nt_dims, preferred_element_type=f32)
                ds = (p * (dp - jnp.sum(p * dp, axis=-1, keepdims=True)) * scale).astype(bf16)
                dq_ref[...] = jnp.dot(ds, kb, preferred_element_type=f32).astype(dq_ref.dtype)
                dk_ref[rows, :] += lax.dot_general(ds, qb, tn_dims, preferred_element_type=f32)
                dv_ref[rows, :] += lax.dot_general(p.astype(bf16), dob, tn_dims, preferred_element_type=f32)

            on_keys(i, run)

        return pl.pallas_call(
            body, name=name + "_bwd", grid=grid, in_specs=[q_spec, kv_spec, kv_spec, lse_spec, q_spec],
            out_specs=[q_spec, kv_spec, kv_spec],
            out_shape=[jax.ShapeDtypeStruct(q.shape, bf16), jax.ShapeDtypeStruct(k.shape, f32), jax.ShapeDtypeStruct(v.shape, f32)],
            compiler_params=pltpu.CompilerParams(dimension_semantics=("parallel", "arbitrary", "arbitrary")),
        )(q, k, v, lse, do)

    @jax.custom_vjp
    def op(q, k, v):
        return run_fwd(q, k, v)[0]

    def fwd(q, k, v):
        o, lse = run_fwd(q, k, v)
        return o, (q, k, v, lse)

    def bwd(res, do):
        dq, dk, dv = run_bwd(*res, do)
        return dq, dk.astype(bf16), dv.astype(bf16)

    op.defvjp(fwd, bwd)
    return op(q, k, v)


def _bdot(a, b, ca, cb):
    fa, fb = 3 - ca, 3 - cb

    def dot(x, y, cx, cy):
        return lax.dot_general(x.astype(bf16), y.astype(bf16), (((cx,), (cy,)), ((0,), (0,))), preferred_element_type=f32)

    @jax.custom_vjp
    def op(a, b):
        return dot(a, b, ca, cb)

    def fwd(a, b):
        return op(a, b), (a, b)

    def bwd(res, ct):
        a, b = res
        da = dot(ct, b, 2, fb) if ca == 2 else dot(b, ct, fb, 2)
        db = dot(a, ct, fa, 1) if cb == 1 else dot(ct, a, 1, fa)
        return da, db

    op.defvjp(fwd, bwd)
    return op(a, b)


def _chunk_cumsum(x, rev):
    def impl(x, rev):
        n = x.shape[0]
        pos = lax.broadcasted_iota(jnp.int32, x.shape, 0) % HG_CHUNK
        s = 1
        while s < HG_CHUNK:
            if rev:
                x = x + jnp.where(pos < HG_CHUNK - s, pltpu.roll(x, n - s, 0), 0.0)
            else:
                x = x + jnp.where(pos >= s, pltpu.roll(x, s, 0), 0.0)
            s *= 2
        return x

    @jax.custom_vjp
    def op(x):
        return impl(x, rev)

    op.defvjp(lambda x: (op(x), None), lambda _, ct: (impl(ct, not rev),))
    return op(x)


def _hg_group(St, hq, hf, hi, lb, *, rev):
    G, C = HG_GROUP, HG_CHUNK
    R = G * C
    q = _silu(hq)
    f = lb + (1.0 - lb) * jax.nn.sigmoid(hf)
    logf = jnp.log(jnp.maximum(f, F_MIN))
    kk = (1.0 - lb) * jax.nn.sigmoid(-hf)
    b3 = _chunk_cumsum(logf, rev).reshape(G, C, HEAD)
    q3, k3, v3 = q.reshape(G, C, HEAD), kk.reshape(G, C, HEAD), hi.reshape(G, C, HEAD)
    btot = jnp.sum(logf.reshape(G, C, HEAD), axis=1)
    tt = lax.broadcasted_iota(jnp.int32, (G, C, C, HEAD), 1)
    ss = lax.broadcasted_iota(jnp.int32, (G, C, C, HEAD), 2)
    mask = (ss >= tt) if rev else (ss <= tt)
    diff = b3[:, :, None, :] - b3[:, None, :, :]
    dec = jnp.where(mask, jnp.exp(jnp.where(mask, diff, 0.0)), 0.0)
    scores = jnp.sum(q3[:, :, None, :] * k3[:, None, :, :] * dec, axis=-1)
    o_intra = _bdot(scores, v3, 2, 1)
    q_dec = q3 * jnp.exp(b3)
    k_dec = k3 * jnp.exp(btot[:, None, :] - b3)
    kvt = _bdot(v3, k_dec, 1, 1)
    dl = jnp.exp(btot)
    states = [None] * G
    for g in (range(G - 1, -1, -1) if rev else range(G)):
        states[g] = St
        St = St * dl[g:g + 1, :] + kvt[g]
    o_inter = _bdot(q_dec, jnp.stack(states), 2, 2)
    return St, (o_intra + o_inter).reshape(R, HEAD)


def hgrn(hq, hf, hi, lb, T, rev, name):
    NT, W = hq.shape
    R = HG_GROUP * HG_CHUNK
    n_lat, n_ctx = T // R, (NT - T) // R
    nG = n_lat + n_ctx

    def group_of(j):
        if rev:
            return jnp.where(j < n_ctx, nG - 1 - j, n_lat - 1 - (j - n_ctx))
        return jnp.where(j < n_ctx, n_lat + j, j - n_ctx)

    def rows_of(j):
        return pl.ds(pl.multiple_of(group_of(j) * R, R), R)

    col_spec = pl.BlockSpec((NT, HEAD), lambda h: (0, h))
    lb_spec = pl.BlockSpec((1, HEAD), lambda h: (0, h))
    st_spec = pl.BlockSpec((None, nG, HEAD, HEAD), lambda h: (h, 0, 0, 0))
    sem = pltpu.CompilerParams(dimension_semantics=("parallel",))

    def run_fwd(hq, hf, hi, lb):
        def body(hq_ref, hf_ref, hi_ref, lb_ref, o_ref, st_ref):
            def step(j, St):
                st_ref[j] = St
                rows = rows_of(j)
                St, o = _hg_group(St, hq_ref[rows, :], hf_ref[rows, :], hi_ref[rows, :], lb_ref[...], rev=rev)
                o_ref[rows, :] = o
                return St

            lax.fori_loop(0, nG, step, jnp.zeros((HEAD, HEAD), f32))

        return pl.pallas_call(
            body, name=name + "_fwd", grid=(W // HEAD,), in_specs=[col_spec, col_spec, col_spec, lb_spec],
            out_specs=[col_spec, st_spec],
            out_shape=[jax.ShapeDtypeStruct((NT, W), f32), jax.ShapeDtypeStruct((W // HEAD, nG, HEAD, HEAD), f32)],
            compiler_params=sem,
        )(hq, hf, hi, lb)

    def run_bwd(hq, hf, hi, lb, st, do):
        def body(hq_ref, hf_ref, hi_ref, lb_ref, st_ref, do_ref, dq_ref, df_ref, di_ref, dlb_ref):
            def step(jj, carry):
                dS, dlb = carry
                j = nG - 1 - jj
                rows = rows_of(j)
                _, vjp = jax.vjp(functools.partial(_hg_group, rev=rev), st_ref[j], hq_ref[rows, :], hf_ref[rows, :],
                                 hi_ref[rows, :], lb_ref[...])
                dS, dq, df, di, dl = vjp((dS, do_ref[rows, :]))
                dq_ref[rows, :] = dq
                df_ref[rows, :] = df
                di_ref[rows, :] = di
                return dS, dlb + dl

            _, dlb = lax.fori_loop(0, nG, step, (jnp.zeros((HEAD, HEAD), f32), jnp.zeros((1, HEAD), f32)))
            dlb_ref[...] = dlb

        return pl.pallas_call(
            body, name=name + "_bwd", grid=(W // HEAD,),
            in_specs=[col_spec, col_spec, col_spec, lb_spec, st_spec, col_spec],
            out_specs=[col_spec, col_spec, col_spec, lb_spec],
            out_shape=[jax.ShapeDtypeStruct((NT, W), f32)] * 3 + [jax.ShapeDtypeStruct((1, W), f32)],
            compiler_params=sem,
        )(hq, hf, hi, lb, st, do)

    @jax.custom_vjp
    def op(hq, hf, hi, lb):
        return run_fwd(hq, hf, hi, lb)[0]

    def fwd(hq, hf, hi, lb):
        o, st = run_fwd(hq, hf, hi, lb)
        return o, (hq, hf, hi, lb, st)

    def bwd(res, do):
        return tuple(run_bwd(*res, do))

    op.defvjp(fwd, bwd)
    return op(hq, hf, hi, lb)


def lower_bounds(params):
    n = len(params)

    def f(*a):
        m = functools.reduce(jnp.maximum, a)
        e = [jnp.exp(x - m) for x in a]
        s = functools.reduce(lambda u, v: u + v, e)
        p = [x / s for x in e]
        out, run = [], jnp.zeros_like(p[0])
        for l in range(n):
            out.append(run)
            run = run + p[l]
        return tuple(out[l] + p[l] - p[0] for l in range(n))

    shape = [jax.ShapeDtypeStruct(params[0].shape, f32)] * n

    @jax.custom_vjp
    def op(*a):
        def body(*refs):
            for o_ref, r in zip(refs[n:], f(*[x[...] for x in refs[:n]])):
                o_ref[...] = r
        return tuple(pl.pallas_call(body, name="lower_bounds_fwd", out_shape=shape)(*a))

    def fwd(*a):
        return op(*a), a

    def bwd(a, cts):
        def body(*refs):
            _, vjp = jax.vjp(f, *[x[...] for x in refs[:n]])
            for o_ref, r in zip(refs[2 * n:], vjp(tuple(x[...] for x in refs[n:2 * n]))):
                o_ref[...] = r
        return tuple(pl.pallas_call(body, name="lower_bounds_bwd", out_shape=shape)(*a, *cts))

    op.defvjp(fwd, bwd)
    return op(*params)


def _shift_rows(x, d, T):
    n = x.shape[0]
    t = lax.broadcasted_iota(jnp.int32, x.shape, 0)
    y = pltpu.roll(x, d % n, 0)
    edge = ((t == 0) | (t == T)) if d == 1 else ((t == T - 1) | (t == n - 1))
    return jnp.where(edge, 0.0, y)


def _conv(x, w, b, T):
    return b + w[0:1] * _shift_rows(x, 1, T) + w[1:2] * x + w[2:3] * _shift_rows(x, -1, T)


def convact(up, cw, cb, T, name, tc=128):
    NT, F2 = up.shape
    F = F2 // 2
    tc = _pick(F, (tc, 128))
    nf = F // tc
    g_spec = lambda r: pl.BlockSpec((r, tc), lambda j: (0, j))
    v_spec = lambda r: pl.BlockSpec((r, tc), lambda j: (0, j + nf))
    sem = pltpu.CompilerParams(dimension_semantics=("parallel",))

    def run_fwd(up, cw, cb):
        def body(xg_ref, xv_ref, wg_ref, wv_ref, bg_ref, bv_ref, o_ref):
            yg = _conv(xg_ref[...].astype(f32), wg_ref[...], bg_ref[...], T)
            yv = _conv(xv_ref[...].astype(f32), wv_ref[...], bv_ref[...], T)
            o_ref[...] = (_silu(yg) * yv).astype(o_ref.dtype)

        return pl.pallas_call(
            body, name=name + "_fwd", grid=(nf,),
            in_specs=[g_spec(NT), v_spec(NT), g_spec(3), v_spec(3), g_spec(1), v_spec(1)], out_specs=g_spec(NT),
            out_shape=jax.ShapeDtypeStruct((NT, F), bf16), compiler_params=sem,
        )(up, up, cw, cw, cb, cb)

    def run_bwd(up, cw, cb, dact):
        def body(xg_ref, xv_ref, wg_ref, wv_ref, bg_ref, bv_ref, da_ref, dxg_ref, dxv_ref, dwg_ref, dwv_ref, dbg_ref, dbv_ref):
            xg, xv, wg, wv = xg_ref[...].astype(f32), xv_ref[...].astype(f32), wg_ref[...], wv_ref[...]
            yg = _conv(xg, wg, bg_ref[...], T)
            yv = _conv(xv, wv, bv_ref[...], T)
            da = da_ref[...].astype(f32)
            sg = jax.nn.sigmoid(yg)
            dyv = da * yg * sg
            dyg = da * yv * sg * (1.0 + yg * (1.0 - sg))
            for x, w, dy, dx_ref, dw_ref, db_ref in ((xg, wg, dyg, dxg_ref, dwg_ref, dbg_ref), (xv, wv, dyv, dxv_ref, dwv_ref, dbv_ref)):
                dx_ref[...] = (w[0:1] * _shift_rows(dy, -1, T) + w[1:2] * dy + w[2:3] * _shift_rows(dy, 1, T)).astype(dx_ref.dtype)
                dw_ref[...] = jnp.concatenate([
                    jnp.sum(dy * _shift_rows(x, 1, T), axis=0, keepdims=True),
                    jnp.sum(dy * x, axis=0, keepdims=True),
                    jnp.sum(dy * _shift_rows(x, -1, T), axis=0, keepdims=True)], axis=0)
                db_ref[...] = jnp.sum(dy, axis=0, keepdims=True)

        return pl.pallas_call(
            body, name=name + "_bwd", grid=(nf,),
            in_specs=[g_spec(NT), v_spec(NT), g_spec(3), v_spec(3), g_spec(1), v_spec(1), g_spec(NT)],
            out_specs=[g_spec(NT), g_spec(NT), g_spec(3), g_spec(3), g_spec(1), g_spec(1)],
            out_shape=[jax.ShapeDtypeStruct((NT, F), up.dtype)] * 2 + [jax.ShapeDtypeStruct((3, F), f32)] * 2 + [jax.ShapeDtypeStruct((1, F), f32)] * 2,
            compiler_params=sem,
        )(up, up, cw, cw, cb, cb, dact)

    @jax.custom_vjp
    def op(up, cw, cb):
        return run_fwd(up, cw, cb)

    def fwd(up, cw, cb):
        return op(up, cw, cb), (up, cw, cb)

    def bwd(res, dact):
        dxg, dxv, dwg, dwv, dbg, dbv = run_bwd(*res, dact)
        return (jnp.concatenate([dxg, dxv], axis=1), jnp.concatenate([dwg, dwv], axis=1), jnp.concatenate([dbg, dbv], axis=1))

    op.defvjp(fwd, bwd)
    return op(up, cw, cb)


def _peers():
    x, y, c = lax.axis_index("x"), lax.axis_index("y"), lax.axis_index("c")
    return (x, y, c), [(x, y, 1 - c), (1 - x, y, c), (x, 1 - y, c), (1 - x, 1 - y, c),
                       (1 - x, y, 1 - c), (x, 1 - y, 1 - c), (1 - x, 1 - y, 1 - c)]


def _index(dev):
    return 4 * dev[0] + 2 * dev[1] + dev[2]


def allgather_small(x, name):
    m, n = x.shape

    def body(x_ref, out_ref, send_sems, recv_sems, local_sem):
        me, peers = _peers()

        def rows(dev):
            return out_ref.at[pl.ds(pl.multiple_of(_index(dev) * m, 8), m), :]

        mine = pltpu.make_async_copy(x_ref, rows(me), local_sem)
        mine.start()
        sends = [pltpu.make_async_remote_copy(src_ref=x_ref, dst_ref=rows(me), send_sem=send_sems.at[k], recv_sem=recv_sems.at[k],
                                              device_id=p, device_id_type=MESH) for k, p in enumerate(peers)]
        for cp in sends:
            cp.start()
        for k, p in enumerate(peers):
            pltpu.make_async_remote_copy(src_ref=x_ref, dst_ref=rows(p), send_sem=send_sems.at[k], recv_sem=recv_sems.at[k],
                                         device_id=p, device_id_type=MESH).wait_recv()
        for cp in sends:
            cp.wait_send()
        mine.wait()

    return pl.pallas_call(
        body, name=name, out_shape=jax.ShapeDtypeStruct((N_DEV * m, n), x.dtype),
        in_specs=[pl.BlockSpec(memory_space=pltpu.VMEM)], out_specs=pl.BlockSpec(memory_space=pltpu.VMEM),
        scratch_shapes=[pltpu.SemaphoreType.DMA((7,)), pltpu.SemaphoreType.DMA((7,)), pltpu.SemaphoreType.DMA],
    )(x)


HBM_SPEC = pl.BlockSpec(memory_space=pltpu.HBM)
SEM_SPEC = pl.BlockSpec(memory_space=pltpu.SEMAPHORE)
SPLIT_PARAMS = dict(compiler_params=pltpu.CompilerParams(has_side_effects=pltpu.SideEffectType.DATAFLOW_SIDE_EFFECTING))
N_PEERS = N_DEV - 1


def _part(ref, kind, j, width):
    if kind == "col":
        return ref.at[:, pl.ds(pl.multiple_of(j * width, 128), width)]
    return ref.at[pl.ds(pl.multiple_of(j * width, 8), width), :]


def _in_hbm(a):
    return pltpu.with_memory_space_constraint(a, pltpu.HBM)


def gather_start(shards, lands, kinds, after):
    n = len(shards)
    widths = [s.shape[1] if k == "col" else s.shape[0] for s, k in zip(shards, kinds)]
    n_in = 2 * n + len(after)

    def body(*refs):
        srcs, lnds, send_sems, recv_sems, token = refs[:n], refs[n:2 * n], refs[n_in], refs[n_in + 1], refs[-1]
        me, peers = _peers()
        for a in range(n):
            for k, p in enumerate(peers):
                pltpu.make_async_remote_copy(
                    src_ref=srcs[a], dst_ref=_part(lnds[a], kinds[a], _index(me), widths[a]),
                    send_sem=send_sems.at[N_PEERS * a + k], recv_sem=recv_sems.at[N_PEERS * a + k],
                    device_id=p, device_id_type=MESH).start()
        token[...] = jnp.zeros_like(token)

    res = pl.pallas_call(
        body, name="gather_start",
        out_shape=(pltpu.SemaphoreType.DMA((N_PEERS * n,)), pltpu.SemaphoreType.DMA((N_PEERS * n,)),
                   *[pltpu.HBM(a.shape, a.dtype) for a in (*shards, *lands)], jax.ShapeDtypeStruct((8, 128), f32)),
        in_specs=[HBM_SPEC] * (2 * n) + [pl.BlockSpec(memory_space=pl.ANY)] * len(after),
        out_specs=(SEM_SPEC, SEM_SPEC, *[HBM_SPEC] * (2 * n), pl.BlockSpec(memory_space=pltpu.VMEM)),
        input_output_aliases={i: 2 + i for i in range(2 * n)}, **SPLIT_PARAMS,
    )(*[_in_hbm(a) for a in (*shards, *lands)], *after)
    return res[0], res[1], res[2:2 + n], res[2 + n:2 + 2 * n], res[-1]


def gather_wait(a, shard, land, kind, send_sems, recv_sems, after, name):
    width = shard.shape[1] if kind == "col" else shard.shape[0]

    def body(src_ref, land_ref, send_ref, recv_ref, after_ref, src_out, land_out):
        _, peers = _peers()
        for k, p in enumerate(peers):
            cp = pltpu.make_async_remote_copy(
                src_ref=src_ref, dst_ref=_part(land_ref, kind, _index(p), width),
                send_sem=send_ref.at[N_PEERS * a + k], recv_sem=recv_ref.at[N_PEERS * a + k], device_id=p, device_id_type=MESH)
            cp.wait_send()
            cp.wait_recv()

    return pl.pallas_call(
        body, name=name, out_shape=(pltpu.HBM(shard.shape, shard.dtype), pltpu.HBM(land.shape, land.dtype)),
        in_specs=(HBM_SPEC, HBM_SPEC, SEM_SPEC, SEM_SPEC, pl.BlockSpec(memory_space=pl.ANY)), out_specs=(HBM_SPEC, HBM_SPEC),
        input_output_aliases={0: 0, 1: 1}, **SPLIT_PARAMS,
    )(shard, land, send_sems, recv_sems, after)[1]


def scatter_start(full, slots, kind, name, carry):
    width = slots.shape[2] if kind == "col" else slots.shape[1]

    def body(full_ref, slots_ref, carry_ref, send_sems, recv_sems, full_out, slots_out, carry_out):
        me, peers = _peers()
        for k, p in enumerate(peers):
            pltpu.make_async_remote_copy(
                src_ref=_part(full_ref, kind, _index(p), width), dst_ref=slots_ref.at[_index(me)],
                send_sem=send_sems.at[k], recv_sem=recv_sems.at[k], device_id=p, device_id_type=MESH).start()

    return pl.pallas_call(
        body, name=name,
        out_shape=(pltpu.SemaphoreType.DMA((N_PEERS,)), pltpu.SemaphoreType.DMA((N_PEERS,)), pltpu.HBM(full.shape, full.dtype),
                   pltpu.HBM(slots.shape, slots.dtype), pltpu.HBM(carry.shape, carry.dtype)),
        in_specs=(HBM_SPEC, HBM_SPEC, HBM_SPEC), out_specs=(SEM_SPEC, SEM_SPEC, HBM_SPEC, HBM_SPEC, HBM_SPEC),
        input_output_aliases={0: 2, 1: 3, 2: 4}, **SPLIT_PARAMS,
    )(_in_hbm(full), _in_hbm(slots), _in_hbm(carry))


def scatter_wait(full, slots, kind, send_sems, recv_sems, after, name):
    width = slots.shape[2] if kind == "col" else slots.shape[1]

    def body(full_ref, slots_ref, send_ref, recv_ref, after_ref, full_out, slots_out):
        me, peers = _peers()
        for k, p in enumerate(peers):
            cp = pltpu.make_async_remote_copy(
                src_ref=_part(full_ref, kind, _index(p), width), dst_ref=slots_ref.at[_index(p)],
                send_sem=send_ref.at[k], recv_sem=recv_ref.at[k], device_id=p, device_id_type=MESH)
            cp.wait_send()
            cp.wait_recv()

    return pl.pallas_call(
        body, name=name, out_shape=(pltpu.HBM(full.shape, full.dtype), pltpu.HBM(slots.shape, slots.dtype)),
        in_specs=(HBM_SPEC, HBM_SPEC, SEM_SPEC, SEM_SPEC, pl.BlockSpec(memory_space=pl.ANY)), out_specs=(HBM_SPEC, HBM_SPEC),
        input_output_aliases={0: 0, 1: 1}, **SPLIT_PARAMS,
    )(full, slots, send_sems, recv_sems, after)[1]


def sum_slots(x, name):
    _, R, C = x.shape
    tr = _pick(R, (256, 128, 64, 32, 16, 8))

    def body(x_ref, o_ref):
        acc = x_ref[0].astype(f32)
        for d in range(1, N_DEV):
            acc = acc + x_ref[d].astype(f32)
        o_ref[...] = acc

    return pl.pallas_call(
        body, name=name, grid=(R // tr,), in_specs=[pl.BlockSpec((N_DEV, tr, C), lambda i: (0, i, 0))],
        out_specs=pl.BlockSpec((tr, C), lambda i: (i, 0)), out_shape=jax.ShapeDtypeStruct((R, C), f32),
        compiler_params=pltpu.CompilerParams(dimension_semantics=("parallel",)),
    )(x)


BIG = (("w_in", "col"), ("w_out", "row"), ("w_up", "col"), ("w_down", "row"))


class BigWeights:
    def __init__(self, given, me, after):
        self.me = me
        self.kinds = dict(BIG)
        depth = given["w_in"].shape[0]
        self.keys = [(l, n) for l in range(depth) for n, _ in BIG]
        kinds = [self.kinds[n] for _, n in self.keys]
        shards = [given[n][l].astype(bf16) for l, n in self.keys]
        lands = []
        for s, k in zip(shards, kinds):
            r, c = s.shape
            if k == "col":
                lands.append(lax.dynamic_update_slice(jnp.zeros((r, N_DEV * c), bf16), s, (0, me * c)))
            else:
                lands.append(lax.dynamic_update_slice(jnp.zeros((N_DEV * r, c), bf16), s, (me * r, 0)))
        self.send, self.recv, self.shards, self.lands, _ = gather_start(shards, lands, kinds, after)
        self.pending = {}

    def get(self, l, n, after):
        a = self.keys.index((l, n))
        return gather_wait(a, self.shards[a], self.lands[a], self.kinds[n], self.send, self.recv, after, "gather_wait_%s%d" % (n, l))

    def start_scatter(self, l, n, dw, carry):
        kind = self.kinds[n]
        R, C = dw.shape
        r, c = (R, C // N_DEV) if kind == "col" else (R // N_DEV, C)
        own = lax.dynamic_slice(dw, (0, self.me * c) if kind == "col" else (self.me * r, 0), (r, c))
        slots = lax.dynamic_update_slice(lax.empty((N_DEV, r, c), bf16), own[None], (self.me, 0, 0))
        send, recv, full, slots, carry = scatter_start(dw, slots, kind, "scatter_start_%s%d" % (n, l), carry)
        self.pending[(l, n)] = (full, slots, send, recv)
        return carry

    def contributions(self, l, n, after):
        full, slots, send, recv = self.pending[(l, n)]
        return scatter_wait(full, slots, self.kinds[n], send, recv, after, "scatter_wait_%s%d" % (n, l))


def allreduce_small(vals, name):
    flat = jnp.concatenate([v.reshape(-1) for v in vals])
    n = flat.shape[0]
    cols = 1024
    m = -(-n // (cols * 8)) * 8
    packed = jnp.pad(flat, (0, m * cols - n)).reshape(m, cols)
    total = sum_slots(allgather_small(packed, name).reshape(N_DEV, m, cols), name + "_sum").reshape(-1)
    out, off = [], 0
    for v in vals:
        out.append(total[off:off + v.size].reshape(v.shape))
        off += v.size
    return out


def ada_mod(c_all, c_ctx, w_ada, b_ada):
    L, D, S = w_ada.shape
    me = _my_index()

    def stacked(c_ctx):
        return jnp.concatenate([c_all, jnp.broadcast_to(c_ctx, (N_DEV, D))], axis=0)

    ts = _pick(S, (512, 384, 256, 128, 64))
    w_spec = pl.BlockSpec((None, D, ts), lambda l, j: (l, 0, j))
    c_spec = pl.BlockSpec((16, D), lambda l, j: (0, 0))
    p_spec = pl.BlockSpec((None, 16, ts), lambda l, j: (l, 0, j))

    def run_fwd(cin, w_ada):
        def body(c_ref, w_ref, o_ref):
            o_ref[...] = jnp.dot(_silu(c_ref[...]).astype(bf16), w_ref[...].astype(bf16), preferred_element_type=f32)

        return pl.pallas_call(
            body, name="ada_fwd", grid=(L, S // ts), in_specs=[c_spec, w_spec], out_specs=p_spec,
            out_shape=jax.ShapeDtypeStruct((L, 16, S), f32),
            compiler_params=pltpu.CompilerParams(dimension_semantics=("parallel", "parallel")),
        )(cin, w_ada)

    def run_bwd(cin, w_ada, dm):
        def body(c_ref, w_ref, dm_ref, gw_ref, dc_ref):
            first = (pl.program_id(0) == 0) & (pl.program_id(1) == 0)
            cv = c_ref[...]
            sg = jax.nn.sigmoid(cv)
            dmv = dm_ref[...].astype(bf16)
            gw_ref[...] = lax.dot_general((cv * sg).astype(bf16), dmv, (((0,), (0,)), ((), ())), preferred_element_type=f32)
            ds = lax.dot_general(dmv, w_ref[...].astype(bf16), (((1,), (1,)), ((), ())), preferred_element_type=f32)
            dc = ds * sg * (1.0 + cv * (1.0 - sg))

            @pl.when(first)
            def _():
                dc_ref[...] = dc

            @pl.when(jnp.logical_not(first))
            def _():
                dc_ref[...] += dc

        return pl.pallas_call(
            body, name="ada_bwd", grid=(L, S // ts), in_specs=[c_spec, w_spec, p_spec], out_specs=[w_spec, c_spec],
            out_shape=[jax.ShapeDtypeStruct((L, D, S), f32), jax.ShapeDtypeStruct((16, D), f32)],
            compiler_params=pltpu.CompilerParams(dimension_semantics=("arbitrary", "arbitrary")),
        )(cin, w_ada, dm)

    def bias_grad(dm_full):
        def body(x_ref, o_ref):
            o_ref[...] = jnp.sum(x_ref[...], axis=0, keepdims=True)

        return pl.pallas_call(
            body, name="ada_bias_grad", grid=(L,), in_specs=[pl.BlockSpec((None, 16, 6 * D), lambda l: (l, 0, 0))],
            out_specs=pl.BlockSpec((None, 1, 6 * D), lambda l: (l, 0, 0)), out_shape=jax.ShapeDtypeStruct((L, 1, 6 * D), f32),
        )(dm_full).reshape(L, 6 * D)

    @jax.custom_vjp
    def op(c_ctx, w_ada, b_ada):
        prod = run_fwd(stacked(c_ctx), w_ada)
        allp = allgather_small(prod.reshape(L * 16, S), "ada_gather").reshape(N_DEV, L, 16, S)
        allp = allp.transpose(1, 2, 0, 3).reshape(L, 16, N_DEV * S)
        mine = lax.dynamic_index_in_dim(allp, me, axis=1, keepdims=False) + b_ada
        ctx = allp[:, N_DEV] + b_ada
        return jnp.stack([mine, ctx], axis=1).reshape(L, 2, 6, D)

    def fwd(c_ctx, w_ada, b_ada):
        return op(c_ctx, w_ada, b_ada), (c_ctx, w_ada)

    def bwd(res, dmod):
        c_ctx, w_ada = res
        dm = dmod.reshape(L * 2, 6 * D)
        gathered = allgather_small(jnp.pad(dm, ((0, (-2 * L) % 8), (0, 0))), "ada_grad_gather")
        gathered = gathered.reshape(N_DEV, -1, 6 * D)[:, :2 * L].reshape(N_DEV, L, 2, 6 * D)
        dm_full = gathered.transpose(1, 2, 0, 3).reshape(L, 16, 6 * D)
        dm_mine = lax.dynamic_slice_in_dim(dm_full, me * S, S, axis=2)
        gw, dc = run_bwd(stacked(c_ctx), w_ada, dm_mine)
        d_cctx = jnp.sum(dc[N_DEV:], axis=0, keepdims=True)
        return d_cctx, gw, bias_grad(dm_full)

    op.defvjp(fwd, bwd)
    return op(c_ctx, w_ada, b_ada)


def _adamw_math(w, gv, m, v):
    c1 = 1.0 / (1.0 - ADAM_B1 ** ADAM_STEP)
    c2 = 1.0 / (1.0 - ADAM_B2 ** ADAM_STEP)
    mn = ADAM_B1 * m + (1.0 - ADAM_B1) * gv
    vn = ADAM_B2 * v + (1.0 - ADAM_B2) * gv * gv
    return -ADAM_LR * ((mn * c1) / (jnp.sqrt(vn * c2) + ADAM_EPS) + ADAM_WD * w), mn, vn


def adamw_layer(l, slots, w, m, v, outs, name):
    depth, r, c = w.shape
    tr = _pick(r, (128, 64, 32, 16, 8))
    if outs is None:
        outs = tuple(lax.empty((depth, r, c), f32) for _ in range(4))

    def body(s_ref, w_ref, m_ref, v_ref, *rest):
        g_ref, d_ref, mo_ref, vo_ref = rest[4:]
        gv = s_ref[0].astype(f32)
        for d in range(1, N_DEV):
            gv = gv + s_ref[d].astype(f32)
        g_ref[...] = gv
        d_ref[...], mo_ref[...], vo_ref[...] = _adamw_math(w_ref[...], gv, m_ref[...], v_ref[...])

    lay = pl.BlockSpec((None, tr, c), lambda i: (l, i, 0))
    return pl.pallas_call(
        body, name=name, grid=(r // tr,),
        in_specs=[pl.BlockSpec((N_DEV, tr, c), lambda i: (0, i, 0)), lay, lay, lay] + [pl.BlockSpec(memory_space=pl.ANY)] * 4,
        out_specs=[lay] * 4, out_shape=[jax.ShapeDtypeStruct((depth, r, c), f32)] * 4,
        input_output_aliases={4 + k: k for k in range(4)},
        compiler_params=pltpu.CompilerParams(dimension_semantics=("arbitrary",)),
    )(slots, w, m, v, *outs)


def adamw(w, g, m, v, name):
    shape = w.shape
    C = shape[-1]
    R = w.size // C
    tr = _pick(R, (256, 128, 64, 32, 16, 8)) if R * C * 4 > (1 << 20) else R

    def body(w_ref, g_ref, m_ref, v_ref, d_ref, mo_ref, vo_ref):
        d_ref[...], mo_ref[...], vo_ref[...] = _adamw_math(w_ref[...], g_ref[...], m_ref[...], v_ref[...])

    spec = pl.BlockSpec((tr, C), lambda i: (i, 0))
    res = pl.pallas_call(
        body, name=name, grid=(R // tr,), in_specs=[spec] * 4, out_specs=[spec] * 3,
        out_shape=[jax.ShapeDtypeStruct((R, C), f32)] * 3,
        compiler_params=pltpu.CompilerParams(dimension_semantics=("parallel",)),
    )(*[a.reshape(R, C) for a in (w, g, m, v)])
    return tuple(r.reshape(shape) for r in res)


def _rope_tables(T, L):
    rows = T // GRID_W
    row = jnp.repeat(jnp.arange(rows, dtype=f32), GRID_W)
    col = jnp.tile(jnp.arange(GRID_W, dtype=f32), rows)
    n_freq = HEAD // 4
    inv = ROPE_THETA ** (-jnp.arange(n_freq, dtype=f32) / n_freq)
    ang = jnp.concatenate([row[:, None] * inv, col[:, None] * inv], axis=-1)
    cos = jnp.repeat(jnp.cos(ang), 2, axis=-1)
    sin = jnp.repeat(jnp.sin(ang), 2, axis=-1) * jnp.tile(jnp.array([-1.0, 1.0], f32), HEAD // 2)
    return (jnp.concatenate([cos, jnp.ones((L, HEAD), f32)]), jnp.concatenate([sin, jnp.zeros((L, HEAD), f32)]))


def _sel(mod, l, idx):
    return jnp.stack([mod[l, :, i] for i in idx], axis=1)


def _row(a, l):
    return a[l][None, :]


def _first_segment(T, ctx):
    def seg(x, mod, p):
        xs = jnp.concatenate([x, ctx], axis=0)
        (h,) = rowwise("modnorm", f_modnorm, [xs], sels=[_sel(mod, 0, (0, 1))], pars=[_row(p["norm1_g"], 0)],
                       outs=[(x.shape[1], bf16)], n_lat=T)
        return xs, h

    return seg


def _mixer_segment(l, T, cosf, sinf):
    row = _row

    def seg(xs, h, mod, lbs, p, w_in, w_out):
        D = xs.shape[1]
        aq, ak, av, hq, hff, hfb, hi, hgt, su, sv = mm(h, w_in, "w_in%d" % l, split=IN_SIZES)
        q, k, v = rowwise("qkprep%d" % l, f_qkprep, [aq, ak, av], consts=[cosf, sinf],
                          pars=[row(p["q_norm_g"], l), row(p["k_norm_g"], l)],
                          outs=[(aq.shape[1], bf16), (ak.shape[1], bf16), (av.shape[1], bf16)], n_lat=T)
        attn = attention(q, k, v, T, "attn%d" % l)
        o_f = hgrn(hq, hff, hi, lbs[0:1], T, False, "hgrn_f%d" % l)
        o_b = hgrn(hq, hfb, hi, lbs[1:2], T, True, "hgrn_b%d" % l)
        (hg,) = rowwise("hgout%d" % l, f_hgout, [o_f, o_b, hgt], pars=[row(p["hg_norm_g"], l)], outs=[(hgt.shape[1], bf16)], n_lat=T)
        (sg,) = rowwise("sgate%d" % l, f_sgate, [su, sv],
                        pars=[row(p["sg_norm_g"], l), p["sg_w"][l]] + [p["sg_b"][l, gi][:, None] for gi in range(SG_GROUPS)],
                        outs=[(su.shape[1], bf16)], n_lat=T)
        mix = jnp.concatenate([attn, hg, sg], axis=1)
        y = mm(mix, w_out, "w_out%d" % l)
        return rowwise("resid_a%d" % l, f_resid_modnorm, [xs, y], sels=[_sel(mod, l, (2, 3, 4))], pars=[row(p["norm2_g"], l)],
                       outs=[(D, f32), (D, bf16)], n_lat=T)

    return seg


def _ffn_segment(l, depth, T, tgt):
    row = _row

    def seg(xs, h2, mod, p, w_up, w_down):
        D = xs.shape[1]
        up = mm(h2, w_up, "w_up%d" % l, out_dtype=bf16)
        act = convact(up, p["conv_w"][l], row(p["conv_b"], l), T, "convact%d" % l)
        z = mm(act, w_down, "w_down%d" % l)
        if l + 1 < depth:
            return rowwise("resid_b%d" % l, f_resid_modnorm, [xs, z],
                           sels=[jnp.concatenate([_sel(mod, l, (5,)), _sel(mod, l + 1, (0, 1))], axis=1)],
                           pars=[row(p["norm1_g"], l + 1)], outs=[(D, f32), (D, bf16)], n_lat=T)
        return rowwise("resid_final", f_resid_final, [xs[:T], z[:T]], consts=[tgt], sels=[_sel(mod, l, (5,))[0:1]],
                       pars=[p["final_norm_g"][None, :]], outs=[(1, f32)])

    return seg


SEGMENT_PARAMS = ("norm1_g", "q_norm_g", "k_norm_g", "hg_norm_g", "sg_norm_g", "sg_w", "sg_b", "norm2_g", "conv_w", "conv_b",
                  "final_norm_g")


def _prologue(p, c_all):
    depth = p["norm1_g"].shape[0]
    mod, vjp_mod = jax.vjp(lambda cc, wa, ba: ada_mod(c_all, cc, wa, ba), p["c_ctx"], p["w_ada"], p["b_ada"])
    lbs, vjp_lbs = jax.vjp(lambda hg: lower_bounds([hg[:, l] for l in range(depth)]), p["hg_lower_bounds"])
    return mod, vjp_mod, lbs, vjp_lbs


def _loss_and_grads(p, prologue, big, x, ctx, tgt, cosf, sinf):
    T = x.shape[0]
    depth = p["norm1_g"].shape[0]
    add = lambda a, b: jax.tree.map(jnp.add, a, b)
    small = {n: p[n] for n in SEGMENT_PARAMS}
    mod, vjp_mod, lbs, vjp_lbs = prologue
    (xs, h), vjp_first = jax.vjp(_first_segment(T, ctx), x, mod, small)
    vjps = []
    for l in range(depth):
        w_in, w_out = big.get(l, "w_in", h), big.get(l, "w_out", h)
        (xs, h2), vj = jax.vjp(_mixer_segment(l, T, cosf, sinf), xs, h, mod, lbs[l], small, w_in, w_out)
        vjps.append(vj)
        w_up, w_down = big.get(l, "w_up", h2), big.get(l, "w_down", h2)
        out, vj = jax.vjp(_ffn_segment(l, depth, T, tgt), xs, h2, mod, small, w_up, w_down)
        vjps.append(vj)
        if l + 1 < depth:
            xs, h = out
    (rowloss,) = out
    loss = 0.5 * jnp.sum(rowloss)

    ct = (jnp.full(rowloss.shape, 0.5, f32),)
    d_mod, d_small, d_lbs = jnp.zeros_like(mod), jax.tree.map(jnp.zeros_like, small), [None] * depth
    for l in range(depth - 1, -1, -1):
        dxs, dh2, dm, ds, d_up, d_down = vjps[2 * l + 1](ct)
        dh2 = big.start_scatter(l, "w_down", d_down, big.start_scatter(l, "w_up", d_up, dh2))
        d_mod, d_small = d_mod + dm, add(d_small, ds)
        dxs, dh, dm, d_lbs[l], ds, d_in, d_out = vjps[2 * l]((dxs, dh2))
        dh = big.start_scatter(l, "w_out", d_out, big.start_scatter(l, "w_in", d_in, dh))
        d_mod, d_small = d_mod + dm, add(d_small, ds)
        ct = (dxs, dh)
    dx, dm, ds = vjp_first(ct)
    d_cc, d_wada, d_bada = vjp_mod(d_mod + dm)
    (d_hg,) = vjp_lbs(tuple(d_lbs))
    grads = dict(add(d_small, ds), c_ctx=d_cc, w_ada=d_wada, b_ada=d_bada, hg_lower_bounds=d_hg)
    return loss, grads, dx


def kernel(x, c, ctx, c_ctx, w_ada, b_ada, norm1_g, w_in, q_norm_g, k_norm_g, hg_lower_bounds, hg_norm_g, sg_norm_g, sg_w, sg_b, w_out, norm2_g, w_up, conv_w, conv_b, w_down, final_norm_g, loss_target, m_c_ctx, m_w_ada, m_b_ada, m_norm1_g, m_w_in, m_q_norm_g, m_k_norm_g, m_hg_lower_bounds, m_hg_norm_g, m_sg_norm_g, m_sg_w, m_sg_b, m_w_out, m_norm2_g, m_w_up, m_conv_w, m_conv_b, m_w_down, m_final_norm_g, v_c_ctx, v_w_ada, v_b_ada, v_norm1_g, v_w_in, v_q_norm_g, v_k_norm_g, v_hg_lower_bounds, v_hg_norm_g, v_sg_norm_g, v_sg_w, v_sg_b, v_w_out, v_norm2_g, v_w_up, v_conv_w, v_conv_b, v_w_down, v_final_norm_g):
    given = dict(c_ctx=c_ctx, w_ada=w_ada, b_ada=b_ada, norm1_g=norm1_g, w_in=w_in, q_norm_g=q_norm_g, k_norm_g=k_norm_g,
                 hg_lower_bounds=hg_lower_bounds, hg_norm_g=hg_norm_g, sg_norm_g=sg_norm_g, sg_w=sg_w, sg_b=sg_b, w_out=w_out,
                 norm2_g=norm2_g, w_up=w_up, conv_w=conv_w, conv_b=conv_b, w_down=w_down, final_norm_g=final_norm_g)
    moments_m = dict(c_ctx=m_c_ctx, w_ada=m_w_ada, b_ada=m_b_ada, norm1_g=m_norm1_g, w_in=m_w_in, q_norm_g=m_q_norm_g,
                     k_norm_g=m_k_norm_g, hg_lower_bounds=m_hg_lower_bounds, hg_norm_g=m_hg_norm_g, sg_norm_g=m_sg_norm_g,
                     sg_w=m_sg_w, sg_b=m_sg_b, w_out=m_w_out, norm2_g=m_norm2_g, w_up=m_w_up, conv_w=m_conv_w, conv_b=m_conv_b,
                     w_down=m_w_down, final_norm_g=m_final_norm_g)
    moments_v = dict(c_ctx=v_c_ctx, w_ada=v_w_ada, b_ada=v_b_ada, norm1_g=v_norm1_g, w_in=v_w_in, q_norm_g=v_q_norm_g,
                     k_norm_g=v_k_norm_g, hg_lower_bounds=v_hg_lower_bounds, hg_norm_g=v_hg_norm_g, sg_norm_g=v_sg_norm_g,
                     sg_w=v_sg_w, sg_b=v_sg_b, w_out=v_w_out, norm2_g=v_norm2_g, w_up=v_w_up, conv_w=v_conv_w, conv_b=v_conv_b,
                     w_down=v_w_down, final_norm_g=v_final_norm_g)
    T, D = x.shape[1], x.shape[2]
    L = ctx.shape[1]
    me = _my_index()
    axes = ("x", "y", "c")

    c_all = allgather_small(jnp.pad(c, ((0, 7), (0, 0))), "gather_c").reshape(N_DEV, 8, D)[:, 0]
    depth, hw = hg_lower_bounds.shape[1], hg_lower_bounds.shape[2]
    cw = conv_w.shape[2]
    small = jnp.concatenate([jnp.pad(hg_lower_bounds.reshape(2 * depth, hw), ((0, 0), (0, cw - hw))), conv_w.reshape(3 * depth, cw)], axis=0)
    rows_small = small.shape[0]
    small = allgather_small(jnp.pad(small, ((0, (-rows_small) % 8), (0, 0))), "gather_small").reshape(N_DEV, -1, cw)
    hg_full = small[:, :2 * depth, :hw].reshape(N_DEV, 2, depth, hw).transpose(1, 2, 0, 3).reshape(2, depth, N_DEV * hw)
    cw_full = small[:, 2 * depth:2 * depth + 3 * depth].reshape(N_DEV, depth, 3, cw).transpose(1, 2, 0, 3).reshape(depth, 3, N_DEV * cw)

    p = {n: a for n, a in given.items() if n not in dict(BIG)}
    p.update(hg_lower_bounds=hg_full, conv_w=cw_full, c_ctx=c_ctx[None, :])
    prologue = _prologue(p, c_all)
    big = BigWeights(given, me, after=(prologue[0], cw_full))
    cosf, sinf = _rope_tables(T, L)
    loss, gp, gx = _loss_and_grads(p, prologue, big, x[0], ctx[0], loss_target[0], cosf, sinf)
    loss = lax.psum(loss, axes)

    grads, delta, new_m, new_v = dict(gp), {}, {}, {}
    partial = ['c_ctx', 'norm1_g', 'q_norm_g', 'k_norm_g', 'hg_lower_bounds', 'hg_norm_g', 'sg_norm_g', 'sg_w', 'sg_b',
               'norm2_g', 'conv_w', 'conv_b', 'final_norm_g']
    grads.update(zip(partial, allreduce_small([gp[n] for n in partial], "reduce_small")))
    grads['c_ctx'] = grads['c_ctx'][0]
    grads['hg_lower_bounds'] = lax.dynamic_slice_in_dim(grads['hg_lower_bounds'], me * hw, hw, axis=2)
    grads['conv_w'] = lax.dynamic_slice_in_dim(grads['conv_w'], me * cw, cw, axis=2)
    last = gx
    for n in [w for w in WEIGHTS if w not in dict(BIG)]:
        delta[n], new_m[n], new_v[n] = adamw(given[n], grads[n], moments_m[n], moments_v[n], "adamw_" + n)
        last = delta[n]

    big_outs = {n: None for n, _ in BIG}
    for l in range(depth - 1, -1, -1):
        for n in ("w_down", "w_up", "w_out", "w_in"):
            big_outs[n] = adamw_layer(l, big.contributions(l, n, last), given[n], moments_m[n], moments_v[n], big_outs[n],
                                      "adamw_%s%d" % (n, l))
            last = big_outs[n][1]
    for n, _ in BIG:
        grads[n], delta[n], new_m[n], new_v[n] = big_outs[n]
    return (loss, gx[None], *[grads[n] for n in WEIGHTS], *[delta[n] for n in WEIGHTS],
            *[new_m[n] for n in WEIGHTS], *[new_v[n] for n in WEIGHTS])
```
